```python
import jax
import jax.numpy as jnp
from jax import lax
import numpy as np

D_MODEL = 2048
BATCH = 2
SEQ = 4096
DEPTH = 2
DEC_BATCH = 32
DEC_SEQ = 64
PAST_LEN = 2048

CHUNK = 64
N_MIXERS = 2
N_MLA_LAYERS = (DEPTH + 1) // 2
N_GLA_LAYERS = DEPTH // 2

MLA_HEADS = D_MODEL // 128
Q_LORA = 512
KV_LORA = 512
NOPE_DIM = 128
ROPE_DIM = 64
QK_DIM = NOPE_DIM + ROPE_DIM
V_DIM = 128
ROPE_THETA = 10000.0
ATTN_SCALE = QK_DIM ** -0.5
Q_BLOCK = 128
MLA_IN = Q_LORA + KV_LORA + ROPE_DIM

GLA_HEADS = 4
GLA_DK = D_MODEL // (2 * GLA_HEADS)
GLA_DV = D_MODEL // GLA_HEADS
GLA_HK = GLA_HEADS * GLA_DK
GLA_HV = GLA_HEADS * GLA_DV
GATE_RANK = 16
GATE_TAU = 16.0
GLA_IN = 2 * GLA_HK + GLA_HV + GATE_RANK + GLA_HV

N_GROUPS = 8
EXPERTS_PER_GROUP = 8
N_EXPERTS = N_GROUPS * EXPERTS_PER_GROUP
TOP_K = 2
D_EXPERT = 512
EXPERT_BLOCK = 128

EPS = 1e-6

kernel_name = 'hybrid_mla_gla_hmoe_stream_step'


def rmsnorm(x, g):
    xf = x.astype(jnp.float32)
    y = xf * lax.rsqrt(jnp.mean(xf * xf, axis=-1, keepdims=True) + EPS)
    return (y * g.astype(jnp.float32)).astype(x.dtype)


def rope(x, pos):
    half = ROPE_DIM // 2
    inv = jnp.power(ROPE_THETA, -jnp.arange(half, dtype=jnp.float32) * (2.0 / ROPE_DIM))
    ang = pos[:, None] * inv[None, :]
    cos = jnp.cos(ang)[None, :, None, :]
    sin = jnp.sin(ang)[None, :, None, :]
    xf = x.astype(jnp.float32)
    x1, x2 = xf[..., :half], xf[..., half:]
    return jnp.concatenate([x1 * cos - x2 * sin, x2 * cos + x1 * sin], axis=-1).astype(x.dtype)


def qk_norm_rope(t, g, pos):
    t = rmsnorm(t, g)
    return jnp.concatenate([t[..., :NOPE_DIM], rope(t[..., NOPE_DIM:], pos)], axis=-1)


def mla_project(h, w_in, g_q_lat, g_kv_lat, w_uq, g_q, pos):
    B, T, _ = h.shape
    z = h @ w_in
    c_q = rmsnorm(z[..., :Q_LORA], g_q_lat)
    c_kv = rmsnorm(z[..., Q_LORA:Q_LORA + KV_LORA], g_kv_lat)
    k_pe = z[..., Q_LORA + KV_LORA:]
    q = (c_q @ w_uq).reshape(B, T, MLA_HEADS, QK_DIM)
    return qk_norm_rope(q, g_q, pos), c_kv, k_pe


def mla_keys(c_kv, k_pe, w_uk, g_k, pos):
    k_nope = jnp.einsum('btr,rhd->bthd', c_kv, w_uk)
    k_pe_h = jnp.broadcast_to(k_pe[:, :, None, :], k_nope.shape[:3] + (ROPE_DIM,))
    return qk_norm_rope(jnp.concatenate([k_nope, k_pe_h], axis=-1), g_k, pos)


def mla_attend(q, k, c_kv, w_uv, mask):
    s = jnp.einsum('bqhd,bkhd->bhqk', q, k, preferred_element_type=jnp.float32) * ATTN_SCALE
    if mask is not None:
        s = jnp.where(mask, s, -jnp.inf)
    p = jax.nn.softmax(s, axis=-1).astype(c_kv.dtype)
    o_lat = jnp.einsum('bhqk,bkr->bqhr', p, c_kv)
    return jnp.einsum('bqhr,rhv->bqhv', o_lat, w_uv)


def mla_prompt(h, w_in, g_q_lat, g_kv_lat, w_uq, w_uk, w_uv, g_q, g_k, w_o):
    B, T, _ = h.shape
    pos = jnp.arange(T, dtype=jnp.float32)
    q, c_kv, k_pe = mla_project(h, w_in, g_q_lat, g_kv_lat, w_uq, g_q, pos)
    k = mla_keys(c_kv, k_pe, w_uk, g_k, pos)
    n_blk = T // Q_BLOCK
    q_blocks = q.reshape(B, n_blk, Q_BLOCK, MLA_HEADS, QK_DIM).swapaxes(0, 1)
    k_chunk = jnp.arange(T) // CHUNK

    def block(args):
        q_i, i = args
        q_chunk = (i * Q_BLOCK + jnp.arange(Q_BLOCK)) // CHUNK
        mask = k_chunk[None, :] <= q_chunk[:, None]
        return mla_attend(q_i, k, c_kv, w_uv, mask)

    o = lax.map(block, (q_blocks, jnp.arange(n_blk)))
    o = o.swapaxes(0, 1).reshape(B, T, MLA_HEADS * V_DIM)
    return o @ w_o, c_kv, k_pe


def mla_sample(h, cache_ckv, cache_kpe, w_in, g_q_lat, g_kv_lat, w_uq, w_uk, w_uv, g_q, g_k, w_o):
    B, T, _ = h.shape
    P = cache_ckv.shape[1]
    pos_all = jnp.arange(P + T, dtype=jnp.float32)
    q, c_new, kpe_new = mla_project(h, w_in, g_q_lat, g_kv_lat, w_uq, g_q, pos_all[P:])
    c_all = jnp.concatenate([cache_ckv.astype(c_new.dtype), c_new], axis=1)
    kpe_all = jnp.concatenate([cache_kpe.astype(kpe_new.dtype), kpe_new], axis=1)
    k = mla_keys(c_all, kpe_all, w_uk, g_k, pos_all)
    o = mla_attend(q, k, c_all, w_uv, None)
    return o.reshape(B, T, MLA_HEADS * V_DIM) @ w_o, c_new, kpe_new


def gla_project(h, w_in, w_a, b_a):
    B, T, _ = h.shape
    z = h @ w_in
    q = z[..., :GLA_HK].reshape(B, T, GLA_HEADS, GLA_DK) * (GLA_DK ** -0.5)
    k = z[..., GLA_HK:2 * GLA_HK].reshape(B, T, GLA_HEADS, GLA_DK)
    v = z[..., 2 * GLA_HK:2 * GLA_HK + GLA_HV].reshape(B, T, GLA_HEADS, GLA_DV)
    a = z[..., 2 * GLA_HK + GLA_HV:2 * GLA_HK + GLA_HV + GATE_RANK]
    gate = z[..., 2 * GLA_HK + GLA_HV + GATE_RANK:]
    log_a = jax.nn.log_sigmoid((a @ w_a + b_a).astype(jnp.float32)) / GATE_TAU
    return q, k, v, log_a.reshape(B, T, GLA_HEADS, GLA_DK), gate


def gla_scan(q, k, v, log_a, s0):
    B, T, H, _ = q.shape
    L = min(CHUNK, T)
    n = T // L

    def chunks(t):
        return t.astype(jnp.float32).reshape((B, n, L) + t.shape[2:]).swapaxes(0, 1)

    causal = jnp.tril(jnp.ones((L, L), dtype=bool))

    def step(s, xs):
        qc, kc, vc, ac = xs
        b = lax.cumsum(ac, axis=1)
        b_last = b[:, -1:]
        q_s = qc * jnp.exp(b)
        k_s = kc * jnp.exp(-b)
        att = jnp.where(causal, jnp.einsum('blhd,bmhd->bhlm', q_s, k_s), 0.0)
        o = jnp.einsum('bhlm,bmhv->blhv', att, vc) + jnp.einsum('blhd,bhdv->blhv', q_s, s)
        s_new = (jnp.exp(b_last[:, 0])[..., None] * s
                 + jnp.einsum('blhd,blhv->bhdv', kc * jnp.exp(b_last - b), vc))
        return s_new, o

    s, o = lax.scan(step, s0.astype(jnp.float32), (chunks(q), chunks(k), chunks(v), chunks(log_a)))
    return o.swapaxes(0, 1).reshape(B, T, H, v.shape[-1]), s


def gla_mixer(h, s0, w_in, w_a, b_a, g_o, w_o):
    B, T, _ = h.shape
    q, k, v, log_a, gate = gla_project(h, w_in, w_a, b_a)
    o, s = gla_scan(q, k, v, log_a, s0)
    o = rmsnorm(o, g_o).astype(h.dtype).reshape(B, T, GLA_HV)
    return (o * jax.nn.silu(gate)) @ w_o, s.astype(h.dtype)


def routed_experts(x, experts, gates, w_gate, w_up, w_down):
    N, D = x.shape
    A = N * TOP_K
    e_flat = experts.reshape(-1)
    tok = jnp.repeat(jnp.arange(N, dtype=jnp.int32), TOP_K)
    g_flat = gates.reshape(-1)
    order = jnp.argsort(e_flat)
    e_s, tok_s, g_s = e_flat[order], tok[order], g_flat[order]
    counts = jnp.bincount(e_flat, length=N_EXPERTS)
    padded = (counts + EXPERT_BLOCK - 1) // EXPERT_BLOCK * EXPERT_BLOCK
    start = jnp.cumsum(counts) - counts
    p_end = jnp.cumsum(padded)
    p_start = p_end - padded
    dest = p_start[e_s] + (jnp.arange(A) - start[e_s])
    n_blocks = (A + N_EXPERTS * (EXPERT_BLOCK - 1) + EXPERT_BLOCK - 1) // EXPERT_BLOCK
    P = n_blocks * EXPERT_BLOCK
    slot_tok = jnp.full((P,), N, dtype=jnp.int32).at[dest].set(tok_s)
    slot_g = jnp.zeros((P,), dtype=gates.dtype).at[dest].set(g_s)
    block_start = jnp.arange(n_blocks) * EXPERT_BLOCK
    block_e = jnp.minimum(jnp.searchsorted(p_end, block_start, side='right'), N_EXPERTS - 1)
    x_pad = jnp.concatenate([x, jnp.zeros((1, D), x.dtype)], axis=0)
    xb = x_pad[slot_tok].reshape(n_blocks, EXPERT_BLOCK, D)

    def run(args):
        xb_i, e = args
        return (jax.nn.silu(xb_i @ w_gate[e]) * (xb_i @ w_up[e])) @ w_down[e]

    yb = lax.map(run, (xb, block_e)).reshape(P, D)
    y = jax.ops.segment_sum(yb * slot_g[:, None], slot_tok, num_segments=N + 1)
    return y[:N]


def hier_moe(h, w_router, b_router, w_gate, w_up, w_down):
    B, T, D = h.shape
    x = h.reshape(B * T, D)
    logits = (x @ w_router).astype(jnp.float32) + b_router.astype(jnp.float32)
    lg = logits[:, :N_GROUPS]
    le = logits[:, N_GROUPS:].reshape(-1, N_GROUPS, EXPERTS_PER_GROUP)
    grp = jnp.argmax(lg, axis=-1)
    p_grp = jnp.max(jax.nn.softmax(lg, axis=-1), axis=-1, keepdims=True)
    le_sel = le[jnp.arange(x.shape[0]), grp]
    top_p, top_i = lax.top_k(jax.nn.softmax(le_sel, axis=-1), TOP_K)
    gates = p_grp * top_p / jnp.sum(top_p, axis=-1, keepdims=True)
    experts = (grp[:, None] * EXPERTS_PER_GROUP + top_i).astype(jnp.int32)
    y = routed_experts(x, experts, gates.astype(x.dtype), w_gate, w_up, w_down)
    return y.reshape(B, T, D)


def setup_inputs(seed: int = 0) -> dict:
    key = jax.random.key(seed)
    ks = jax.random.split(key, 27)
    f32 = jnp.float32

    def nrm(i, shape, scale):
        return scale * jax.random.normal(ks[i], shape, f32)

    def gain(i, shape):
        return 1.0 + 0.01 * jax.random.normal(ks[i], shape, f32)

    return {
        'x_prompt': nrm(0, (BATCH, SEQ, D_MODEL), 1.0),
        'x_sample': nrm(1, (DEC_BATCH, DEC_SEQ, D_MODEL), 1.0),
        'cache_mla_ckv': nrm(2, (N_MLA_LAYERS, DEC_BATCH, PAST_LEN, KV_LORA), 1.0),
        'cache_mla_kpe': nrm(3, (N_MLA_LAYERS, DEC_BATCH, PAST_LEN, ROPE_DIM), 1.0),
        'state_gla': nrm(4, (N_GLA_LAYERS, DEC_BATCH, GLA_HEADS, GLA_DK, GLA_DV), 1.0),
        'norm_mix': gain(5, (DEPTH, D_MODEL)),
        'norm_ffn': gain(6, (DEPTH, D_MODEL)),
        'norm_out': gain(7, (D_MODEL,)),
        'mla_w_in': nrm(8, (N_MLA_LAYERS, D_MODEL, MLA_IN), D_MODEL ** -0.5),
        'mla_g_q_lat': gain(9, (N_MLA_LAYERS, Q_LORA)),
        'mla_g_kv_lat': gain(10, (N_MLA_LAYERS, KV_LORA)),
        'mla_w_uq': nrm(11, (N_MLA_LAYERS, Q_LORA, MLA_HEADS * QK_DIM), Q_LORA ** -0.5),
        'mla_w_uk': nrm(12, (N_MLA_LAYERS, KV_LORA, MLA_HEADS, NOPE_DIM), KV_LORA ** -0.5),
        'mla_w_uv': nrm(13, (N_MLA_LAYERS, KV_LORA, MLA_HEADS, V_DIM), KV_LORA ** -0.5),
        'mla_g_q': gain(14, (N_MLA_LAYERS, QK_DIM)),
        'mla_g_k': gain(15, (N_MLA_LAYERS, QK_DIM)),
        'mla_w_o': nrm(16, (N_MLA_LAYERS, MLA_HEADS * V_DIM, D_MODEL), (MLA_HEADS * V_DIM) ** -0.5),
        'gla_w_in': nrm(17, (N_GLA_LAYERS, D_MODEL, GLA_IN), D_MODEL ** -0.5),
        'gla_w_a': nrm(18, (N_GLA_LAYERS, GATE_RANK, GLA_HK), GATE_RANK ** -0.5),
        'gla_b_a': nrm(19, (N_GLA_LAYERS, GLA_HK), 0.1),
        'gla_g_o': gain(20, (N_GLA_LAYERS, GLA_DV)),
        'gla_w_o': nrm(21, (N_GLA_LAYERS, GLA_HV, D_MODEL), GLA_HV ** -0.5),
        'moe_w_router': nrm(22, (DEPTH, D_MODEL, N_GROUPS + N_EXPERTS), D_MODEL ** -0.5),
        'moe_b_router': nrm(23, (DEPTH, N_GROUPS + N_EXPERTS), 0.01),
        'moe_w_gate': nrm(24, (DEPTH, N_EXPERTS, D_MODEL, D_EXPERT), D_MODEL ** -0.5),
        'moe_w_up': nrm(25, (DEPTH, N_EXPERTS, D_MODEL, D_EXPERT), D_MODEL ** -0.5),
        'moe_w_down': nrm(26, (DEPTH, N_EXPERTS, D_EXPERT, D_MODEL), D_EXPERT ** -0.5),
    }


def reference(x_prompt, x_sample, cache_mla_ckv, cache_mla_kpe, state_gla,
              norm_mix, norm_ffn, norm_out,
              mla_w_in, mla_g_q_lat, mla_g_kv_lat, mla_w_uq, mla_w_uk, mla_w_uv,
              mla_g_q, mla_g_k, mla_w_o,
              gla_w_in, gla_w_a, gla_b_a, gla_g_o, gla_w_o,
              moe_w_router, moe_b_router, moe_w_gate, moe_w_up, moe_w_down):
    xp, xs = x_prompt, x_sample
    ckv_p, kpe_p, gla_p = [], [], []
    ckv_s, kpe_s, gla_s = [], [], []
    for i in range(DEPTH):
        j = i // N_MIXERS
        hp = rmsnorm(xp, norm_mix[i])
        hs = rmsnorm(xs, norm_mix[i])
        if i % N_MIXERS == 0:
            mw = (mla_w_in[j], mla_g_q_lat[j], mla_g_kv_lat[j], mla_w_uq[j], mla_w_uk[j],
                  mla_w_uv[j], mla_g_q[j], mla_g_k[j], mla_w_o[j])
            yp, c_p, k_p = mla_prompt(hp, *mw)
            ys, c_s, k_s = mla_sample(hs, cache_mla_ckv[j], cache_mla_kpe[j], *mw)
            ckv_p.append(c_p)
            kpe_p.append(k_p)
            ckv_s.append(c_s)
            kpe_s.append(k_s)
        else:
            gw = (gla_w_in[j], gla_w_a[j], gla_b_a[j], gla_g_o[j], gla_w_o[j])
            s0 = jnp.zeros((xp.shape[0], GLA_HEADS, GLA_DK, GLA_DV), xp.dtype)
            yp, st_p = gla_mixer(hp, s0, *gw)
            ys, st_s = gla_mixer(hs, state_gla[j], *gw)
            gla_p.append(st_p)
            gla_s.append(st_s)
        xp = xp + yp
        xs = xs + ys
        ew = (moe_w_router[i], moe_b_router[i], moe_w_gate[i], moe_w_up[i], moe_w_down[i])
        xp = xp + hier_moe(rmsnorm(xp, norm_ffn[i]), *ew)
        xs = xs + hier_moe(rmsnorm(xs, norm_ffn[i]), *ew)
    y_prompt = rmsnorm(xp, norm_out)
    y_sample = rmsnorm(xs, norm_out)
    new_ckv_prompt = jnp.stack(ckv_p)
    new_kpe_prompt = jnp.stack(kpe_p)
    new_gla_prompt = jnp.stack(gla_p)
    new_ckv_sample = jnp.stack(ckv_s)
    new_kpe_sample = jnp.stack(kpe_s)
    new_gla_sample = jnp.stack(gla_s)
    return (y_prompt, y_sample, new_ckv_prompt, new_kpe_prompt, new_gla_prompt,
            new_ckv_sample, new_kpe_sample, new_gla_sample)
```

```python
import functools

import jax
import jax.numpy as jnp
from jax import lax
from jax.experimental import pallas as pl
from jax.experimental.pallas import tpu as pltpu

F32 = jnp.float32
BF16 = jnp.bfloat16

LANES = 128
V7X_VMEM_BYTES = 64 * 1024 * 1024
VMEM_LIMIT = V7X_VMEM_BYTES * 3 // 4

EPS = 1e-6
CHUNK = 64
MLA_HEADS = 16
NOPE_DIM = 128
ROPE_DIM = 64
ROPE_HALF = ROPE_DIM // 2
QK_DIM = NOPE_DIM + ROPE_DIM
QK_PAD = 2 * LANES
V_DIM = 128
Q_LORA = 512
KV_LORA = 512
ROPE_THETA = 10000.0
ATTN_SCALE = QK_DIM ** -0.5
GLA_HEADS = 4
GLA_DK = 256
GLA_DV = 512
GATE_RANK = 16
GATE_TAU = 16.0
N_GROUPS = 8
EXPERTS_PER_GROUP = 8
N_EXPERTS = N_GROUPS * EXPERTS_PER_GROUP
TOP_K = 2

ROW_TILE = 512
MM_ROW_TILE = 1024
FLASH_TQ = 512
FLASH_TK = 512
SAMPLE_HEAD_GROUP = 4
GLA_SUBCHUNKS = 4
EXPERT_TILE = 128
COMBINE_TILE = 128

NT_DIMS = (((1,), (1,)), ((), ()))
TN_DIMS = (((0,), (0,)), ((), ()))


def _params(*sem):
    return pltpu.CompilerParams(dimension_semantics=sem, vmem_limit_bytes=VMEM_LIMIT)


def _rms(x, g):
    return x * lax.rsqrt(jnp.mean(x * x, axis=-1, keepdims=True) + EPS) * g


def _mm_kernel(*refs, has_norm, has_res):
    it = iter(refs)
    x_ref = next(it)
    g_ref = next(it) if has_norm else None
    w_ref = next(it)
    r_ref = next(it) if has_res else None
    o_ref = next(it)
    if has_norm:
        xn_ref = next(it)

        @pl.when(pl.program_id(1) == 0)
        def _():
            xn_ref[...] = _rms(x_ref[...], g_ref[...]).astype(BF16)

        xb = xn_ref[...]
    else:
        xb = x_ref[...]
    acc = jnp.dot(xb, w_ref[...], preferred_element_type=F32)
    if has_res:
        acc = acc + r_ref[...]
    o_ref[...] = acc.astype(o_ref.dtype)


def _mm(x, w, *, gain=None, res=None, out_dtype, tm, tn):
    m, k = x.shape
    n = w.shape[1]
    has_norm, has_res = gain is not None, res is not None
    in_specs = [pl.BlockSpec((tm, k), lambda i, j: (i, 0))]
    args = [x]
    if has_norm:
        in_specs.append(pl.BlockSpec((1, k), lambda i, j: (0, 0)))
        args.append(gain.reshape(1, k))
    in_specs.append(pl.BlockSpec((k, tn), lambda i, j: (0, j)))
    args.append(w)
    if has_res:
        in_specs.append(pl.BlockSpec((tm, tn), lambda i, j: (i, j)))
        args.append(res)
    return pl.pallas_call(
        functools.partial(_mm_kernel, has_norm=has_norm, has_res=has_res),
        grid=(m // tm, n // tn),
        in_specs=in_specs,
        out_specs=pl.BlockSpec((tm, tn), lambda i, j: (i, j)),
        out_shape=jax.ShapeDtypeStruct((m, n), out_dtype),
        scratch_shapes=[pltpu.VMEM((tm, k), BF16)] if has_norm else [],
        compiler_params=_params("parallel", "arbitrary"),
        name="mm",
    )(*args)


def _mla_in_kernel(x_ref, g_ref, w_ref, gq_ref, gkv_ref, cq_ref, ckv_ref, ckvb_ref, kpe_ref):
    xn = _rms(x_ref[...], g_ref[...]).astype(BF16)
    z = jnp.dot(xn, w_ref[...], preferred_element_type=F32)
    cq_ref[...] = _rms(z[:, :Q_LORA], gq_ref[...]).astype(BF16)
    ckv = _rms(z[:, Q_LORA:Q_LORA + KV_LORA], gkv_ref[...])
    ckv_ref[...] = ckv
    ckvb_ref[...] = ckv.astype(BF16)
    kpe_ref[...] = z[:, Q_LORA + KV_LORA:Q_LORA + KV_LORA + ROPE_DIM]


def _mla_in(x, g_mix, w_in_pad, g_q_lat, g_kv_lat):
    n, d = x.shape
    tm = ROW_TILE
    wn = w_in_pad.shape[1]
    row = lambda i: (i, 0)
    fixed = lambda i: (0, 0)
    return pl.pallas_call(
        _mla_in_kernel,
        grid=(n // tm,),
        in_specs=[pl.BlockSpec((tm, d), row), pl.BlockSpec((1, d), fixed),
                  pl.BlockSpec((d, wn), fixed), pl.BlockSpec((1, Q_LORA), fixed),
                  pl.BlockSpec((1, KV_LORA), fixed)],
        out_specs=[pl.BlockSpec((tm, Q_LORA), row), pl.BlockSpec((tm, KV_LORA), row),
                   pl.BlockSpec((tm, KV_LORA), row), pl.BlockSpec((tm, ROPE_DIM), row)],
        out_shape=[jax.ShapeDtypeStruct((n, Q_LORA), BF16), jax.ShapeDtypeStruct((n, KV_LORA), F32),
                   jax.ShapeDtypeStruct((n, KV_LORA), BF16), jax.ShapeDtypeStruct((n, ROPE_DIM), F32)],
        compiler_params=_params("parallel"),
        name="mla_in",
    )(x, g_mix.reshape(1, d), w_in_pad, g_q_lat.reshape(1, -1), g_kv_lat.reshape(1, -1))


def _mla_q_kernel(cq_ref, w_ref, gn_ref, g2_ref, tab_ref, q_ref):
    t = jnp.dot(cq_ref[...], w_ref[...], preferred_element_type=F32)
    t1 = t[:, :NOPE_DIM]
    t2 = t[:, NOPE_DIM:]
    is_rope = lax.broadcasted_iota(jnp.int32, t2.shape, 1) < ROPE_DIM
    ss = (jnp.sum(t1 * t1, axis=-1, keepdims=True)
          + jnp.sum(jnp.where(is_rope, t2 * t2, 0.0), axis=-1, keepdims=True))
    rs = lax.rsqrt(ss * (1.0 / QK_DIM) + EPS) * ATTN_SCALE
    u = t2 * g2_ref[...] * tab_ref[...]
    u = u + pltpu.roll(u, ROPE_DIM, axis=1)
    u = jnp.where(is_rope, u, 0.0)
    q_ref[:, :NOPE_DIM] = (t1 * gn_ref[...] * rs).astype(BF16)
    q_ref[:, NOPE_DIM:] = (u * rs).astype(BF16)


def _mla_q(cq, w_q_heads, g_q, tab):
    n = cq.shape[0]
    tm = ROW_TILE
    g1, g2 = g_q[NOPE_DIM:NOPE_DIM + ROPE_HALF], g_q[NOPE_DIM + ROPE_HALF:]
    g_rope = jnp.concatenate([g1, g2, g2, g1]).reshape(1, LANES)
    return pl.pallas_call(
        _mla_q_kernel,
        grid=(n // tm, MLA_HEADS),
        in_specs=[pl.BlockSpec((tm, Q_LORA), lambda i, h: (i, 0)),
                  pl.BlockSpec((Q_LORA, QK_PAD), lambda i, h: (0, h)),
                  pl.BlockSpec((1, NOPE_DIM), lambda i, h: (0, 0)),
                  pl.BlockSpec((1, LANES), lambda i, h: (0, 0)),
                  pl.BlockSpec((tm, LANES), lambda i, h: (i, 0))],
        out_specs=pl.BlockSpec((tm, QK_PAD), lambda i, h: (i, h)),
        out_shape=jax.ShapeDtypeStruct((n, MLA_HEADS * QK_PAD), BF16),
        compiler_params=_params("parallel", "arbitrary"),
        name="mla_q",
    )(cq, w_q_heads, g_q[:NOPE_DIM].reshape(1, NOPE_DIM), g_rope, tab)


def _rotate_half_rows(x):
    return jnp.concatenate([-x[ROPE_HALF:], x[:ROPE_HALF]], axis=0)


def _mla_kv_kernel(c_ref, kpe_ref, w_ref, gn_ref, gr_ref, tab_ref, k_ref, v_ref):
    t = jnp.dot(c_ref[...], w_ref[...], preferred_element_type=F32)
    kn = t[:, :NOPE_DIM]
    kpe = kpe_ref[...]
    ss = jnp.sum(kn * kn, axis=-1, keepdims=True) + jnp.sum(kpe * kpe, axis=-1, keepdims=True)
    rs = lax.rsqrt(ss * (1.0 / QK_DIM) + EPS)
    kg = kpe * gr_ref[...]
    rot = jnp.concatenate([-kg[:, ROPE_HALF:], kg[:, :ROPE_HALF]], axis=1)
    tab = tab_ref[...]
    kr = kg * tab[:, :ROPE_DIM] + rot * tab[:, ROPE_DIM:]
    k_ref[:, :NOPE_DIM] = (kn * gn_ref[...] * rs).astype(BF16)
    k_ref[:, NOPE_DIM:QK_DIM] = (kr * rs).astype(BF16)
    k_ref[:, QK_DIM:] = jnp.zeros((kn.shape[0], QK_PAD - QK_DIM), BF16)
    v_ref[...] = t[:, NOPE_DIM:].astype(BF16)


def _mla_kv(ckv_b, kpe, w_kv_heads, g_k, tab, n_rows):
    tm = ROW_TILE
    return pl.pallas_call(
        _mla_kv_kernel,
        grid=(n_rows // tm, MLA_HEADS),
        in_specs=[pl.BlockSpec((tm, KV_LORA), lambda i, h: (i, 0)),
                  pl.BlockSpec((tm, ROPE_DIM), lambda i, h: (i, 0)),
                  pl.BlockSpec((KV_LORA, NOPE_DIM + V_DIM), lambda i, h: (0, h)),
                  pl.BlockSpec((1, NOPE_DIM), lambda i, h: (0, 0)),
                  pl.BlockSpec((1, ROPE_DIM), lambda i, h: (0, 0)),
                  pl.BlockSpec((tm, LANES), lambda i, h: (i, 0))],
        out_specs=[pl.BlockSpec((tm, QK_PAD), lambda i, h: (i, h)),
                   pl.BlockSpec((tm, V_DIM), lambda i, h: (i, h))],
        out_shape=[jax.ShapeDtypeStruct((n_rows, MLA_HEADS * QK_PAD), BF16),
                   jax.ShapeDtypeStruct((n_rows, MLA_HEADS * V_DIM), BF16)],
        compiler_params=_params("parallel", "arbitrary"),
        name="mla_kv",
    )(ckv_b, kpe, w_kv_heads, g_k[:NOPE_DIM].reshape(1, NOPE_DIM),
      g_k[NOPE_DIM:].reshape(1, ROPE_DIM), tab)


def _flash_kernel(q_ref, k_ref, v_ref, o_ref, *, tq, tk):
    qi = pl.program_id(2)
    q = q_ref[...]

    def step(j, carry, masked):
        m, l, acc = carry
        ks = pl.multiple_of(j * tk, tk)
        s = lax.dot_general(q, k_ref[pl.ds(ks, tk), :], NT_DIMS, preferred_element_type=F32)
        if masked:
            row = lax.broadcasted_iota(jnp.int32, s.shape, 0) // CHUNK
            col = lax.broadcasted_iota(jnp.int32, s.shape, 1) // CHUNK
            s = jnp.where(col <= row, s, -jnp.inf)
        m_new = jnp.maximum(m, jnp.max(s, axis=-1, keepdims=True))
        p = jnp.exp(s - m_new)
        alpha = jnp.exp(m - m_new)
        l = alpha * l + jnp.sum(p, axis=-1, keepdims=True)
        acc = alpha * acc + jnp.dot(p.astype(BF16), v_ref[pl.ds(ks, tk), :],
                                    preferred_element_type=F32)
        return m_new, l, acc

    init = (jnp.full((tq, 1), -jnp.inf, F32), jnp.zeros((tq, 1), F32), jnp.zeros((tq, V_DIM), F32))
    carry = lax.fori_loop(0, qi, lambda j, c: step(j, c, False), init)
    _, l, acc = step(qi, carry, True)
    o_ref[...] = (acc / l).astype(BF16)


def _flash_prompt(q, k, v, batch, seq):
    tq, tk = FLASH_TQ, FLASH_TK
    assert tq == tk
    nq = seq // tq
    return pl.pallas_call(
        functools.partial(_flash_kernel, tq=tq, tk=tk),
        grid=(batch, MLA_HEADS, nq),
        in_specs=[pl.BlockSpec((tq, QK_PAD), lambda b, h, i: (b * nq + i, h)),
                  pl.BlockSpec((seq, QK_PAD), lambda b, h, i: (b, h)),
                  pl.BlockSpec((seq, V_DIM), lambda b, h, i: (b, h))],
        out_specs=pl.BlockSpec((tq, V_DIM), lambda b, h, i: (b * nq + i, h)),
        out_shape=jax.ShapeDtypeStruct((batch * seq, MLA_HEADS * V_DIM), BF16),
        compiler_params=_params("parallel", "parallel", "arbitrary"),
        name="flash_prompt",
    )(q, k, v)


def _sample_attn_kernel(q_ref, c_ref, kpet_ref, wukt_ref, gkn_ref, gkrt_ref, cost_ref, sint_ref,
                        olat_ref, qabs_ref, qr_ref, *, n_keys, t_new):
    c = c_ref[0]
    kp = c.shape[0]
    kpet = kpet_ref[0]
    sspe = jnp.sum(kpet * kpet, axis=0, keepdims=True)
    kg = kpet * gkrt_ref[...]
    krt = kg * cost_ref[...] + _rotate_half_rows(kg) * sint_ref[...]
    krt = jnp.concatenate([krt, jnp.zeros_like(krt)], axis=0).astype(BF16)
    gkn = gkn_ref[...]
    for h in range(MLA_HEADS):
        qh = q_ref[:, h * QK_PAD:(h + 1) * QK_PAD]
        qn = (qh[:, :NOPE_DIM].astype(F32) * gkn).astype(BF16)
        qa = jnp.dot(qn, wukt_ref[h * NOPE_DIM:(h + 1) * NOPE_DIM, :], preferred_element_type=F32)
        qabs_ref[h * t_new:(h + 1) * t_new, :] = qa.astype(BF16)
        qr_ref[h * t_new:(h + 1) * t_new, :] = qh[:, NOPE_DIM:]
    valid = lax.broadcasted_iota(jnp.int32, (t_new, kp), 1) < n_keys
    hg = SAMPLE_HEAD_GROUP
    for g in range(MLA_HEADS // hg):
        knt = lax.dot_general(wukt_ref[g * hg * NOPE_DIM:(g + 1) * hg * NOPE_DIM, :], c, NT_DIMS,
                              preferred_element_type=F32)
        ss = jnp.sum((knt * knt).reshape(hg, NOPE_DIM, kp), axis=1) + sspe
        rst = lax.rsqrt(ss * (1.0 / QK_DIM) + EPS)
        rows = slice(g * hg * t_new, (g + 1) * hg * t_new)
        s = (lax.dot_general(qabs_ref[rows, :], c, NT_DIMS, preferred_element_type=F32)
             + jnp.dot(qr_ref[rows, :], krt, preferred_element_type=F32))
        ps = []
        for hh in range(hg):
            sh = s[hh * t_new:(hh + 1) * t_new] * rst[hh:hh + 1]
            sh = jnp.where(valid, sh, -jnp.inf)
            e = jnp.exp(sh - jnp.max(sh, axis=-1, keepdims=True))
            ps.append((e / jnp.sum(e, axis=-1, keepdims=True)).astype(BF16))
        p = jnp.concatenate(ps, axis=0)
        olat_ref[0, rows, :] = jnp.dot(p, c, preferred_element_type=F32).astype(BF16)


def _sample_attn(q, c_all, kpet_all, w_ukt, g_k, cost, sint, *, row_block0, n_keys, t_new):
    nb, kp, _ = c_all.shape
    fixed = lambda b: (0, 0)
    return pl.pallas_call(
        functools.partial(_sample_attn_kernel, n_keys=n_keys, t_new=t_new),
        grid=(nb,),
        in_specs=[pl.BlockSpec((t_new, MLA_HEADS * QK_PAD), lambda b: (row_block0 + b, 0)),
                  pl.BlockSpec((1, kp, KV_LORA), lambda b: (b, 0, 0)),
                  pl.BlockSpec((1, ROPE_DIM, kp), lambda b: (b, 0, 0)),
                  pl.BlockSpec((MLA_HEADS * NOPE_DIM, KV_LORA), fixed),
                  pl.BlockSpec((1, NOPE_DIM), fixed),
                  pl.BlockSpec((ROPE_DIM, 1), fixed),
                  pl.BlockSpec((ROPE_DIM, kp), fixed),
                  pl.BlockSpec((ROPE_DIM, kp), fixed)],
        out_specs=pl.BlockSpec((1, MLA_HEADS * t_new, KV_LORA), lambda b: (b, 0, 0)),
        out_shape=jax.ShapeDtypeStruct((nb, MLA_HEADS * t_new, KV_LORA), BF16),
        scratch_shapes=[pltpu.VMEM((MLA_HEADS * t_new, KV_LORA), BF16),
                        pltpu.VMEM((MLA_HEADS * t_new, LANES), BF16)],
        compiler_params=_params("parallel"),
        name="sample_attn",
    )(q, c_all, kpet_all, w_ukt, g_k[:NOPE_DIM].reshape(1, NOPE_DIM),
      g_k[NOPE_DIM:].reshape(ROPE_DIM, 1), cost, sint)


def _head_mm_kernel(x_ref, w_ref, o_ref):
    nb, t, r = x_ref.shape
    o_ref[...] = jnp.dot(x_ref[...].reshape(nb * t, r), w_ref[...],
                         preferred_element_type=F32).astype(o_ref.dtype)


def _latent_to_values(o_lat, w_uv2d, t_new):
    nb = o_lat.shape[0]
    return pl.pallas_call(
        _head_mm_kernel,
        grid=(MLA_HEADS,),
        in_specs=[pl.BlockSpec((nb, t_new, KV_LORA), lambda h: (0, h, 0)),
                  pl.BlockSpec((KV_LORA, V_DIM), lambda h: (0, h))],
        out_specs=pl.BlockSpec((nb * t_new, V_DIM), lambda h: (0, h)),
        out_shape=jax.ShapeDtypeStruct((nb * t_new, MLA_HEADS * V_DIM), BF16),
        compiler_params=_params("parallel"),
        name="latent_to_values",
    )(o_lat, w_uv2d)


def _gla_kernel(q_ref, k_ref, v_ref, gate_ref, a_ref, wa_ref, ba_ref, go_ref, s0_ref,
                o_ref, sout_ref, st_ref, *, nsub):
    c = pl.program_id(2)

    @pl.when(c == 0)
    def _():
        st_ref[...] = s0_ref[0, 0].T

    tril = (lax.broadcasted_iota(jnp.int32, (CHUNK, CHUNK), 0)
            >= lax.broadcasted_iota(jnp.int32, (CHUNK, CHUNK), 1))
    tril_b = jnp.where(tril, 1.0, 0.0).astype(BF16)
    st = st_ref[...]
    for j in range(nsub):
        sl = slice(j * CHUNK, (j + 1) * CHUNK)
        x = jnp.dot(a_ref[sl, :], wa_ref[...], preferred_element_type=F32) + ba_ref[...]
        la = (jnp.minimum(x, 0.0) - jnp.log(1.0 + jnp.exp(-jnp.abs(x)))) * (1.0 / GATE_TAU)
        la_hi = la.astype(BF16)
        la_lo = (la - la_hi.astype(F32)).astype(BF16)
        b = (jnp.dot(tril_b, la_hi, preferred_element_type=F32)
             + jnp.dot(tril_b, la_lo, preferred_element_type=F32))
        b_last = b[CHUNK - 1:CHUNK, :]
        q = q_ref[sl, :].astype(F32) * (GLA_DK ** -0.5)
        k = k_ref[sl, :].astype(F32)
        v = v_ref[sl, :]
        qs = (q * jnp.exp(b)).astype(BF16)
        ks = (k * jnp.exp(-b)).astype(BF16)
        att = lax.dot_general(qs, ks, NT_DIMS, preferred_element_type=F32)
        att = jnp.where(tril, att, 0.0).astype(BF16)
        o = (jnp.dot(att, v, preferred_element_type=F32)
             + lax.dot_general(qs, st.astype(BF16), NT_DIMS, preferred_element_type=F32))
        kd = (k * jnp.exp(b_last - b)).astype(BF16)
        st = st * jnp.exp(b_last) + lax.dot_general(v, kd, TN_DIMS, preferred_element_type=F32)
        gt = gate_ref[sl, :].astype(F32)
        o_ref[sl, :] = (_rms(o, go_ref[...]) * (gt / (1.0 + jnp.exp(-gt)))).astype(BF16)
    st_ref[...] = st

    @pl.when(c == pl.num_programs(2) - 1)
    def _():
        sout_ref[0, 0] = st.T


def _gla_scan(z, w_a_pad, b_a, g_o, s0, *, row0, n_streams, t_len, nsub):
    tc = nsub * CHUNK
    nc = t_len // tc
    rb0 = row0 // tc
    rows = lambda b, h, c: rb0 + b * nc + c
    hk, hv = GLA_HEADS * GLA_DK, GLA_HEADS * GLA_DV
    a_blk = (2 * hk + 2 * hv) // LANES
    n = z.shape[0]
    del n
    return pl.pallas_call(
        functools.partial(_gla_kernel, nsub=nsub),
        grid=(n_streams, GLA_HEADS, nc),
        in_specs=[pl.BlockSpec((tc, GLA_DK), lambda b, h, c: (rows(b, h, c), h)),
                  pl.BlockSpec((tc, GLA_DK), lambda b, h, c: (rows(b, h, c), hk // GLA_DK + h)),
                  pl.BlockSpec((tc, GLA_DV), lambda b, h, c: (rows(b, h, c), 2 * hk // GLA_DV + h)),
                  pl.BlockSpec((tc, GLA_DV), lambda b, h, c: (rows(b, h, c), (2 * hk + hv) // GLA_DV + h)),
                  pl.BlockSpec((tc, LANES), lambda b, h, c: (rows(b, h, c), a_blk)),
                  pl.BlockSpec((LANES, GLA_DK), lambda b, h, c: (0, h)),
                  pl.BlockSpec((1, GLA_DK), lambda b, h, c: (0, h)),
                  pl.BlockSpec((1, GLA_DV), lambda b, h, c: (0, 0)),
                  pl.BlockSpec((1, 1, GLA_DK, GLA_DV), lambda b, h, c: (b, h, 0, 0))],
        out_specs=[pl.BlockSpec((tc, GLA_DV), lambda b, h, c: (b * nc + c, h)),
                   pl.BlockSpec((1, 1, GLA_DK, GLA_DV), lambda b, h, c: (b, h, 0, 0))],
        out_shape=[jax.ShapeDtypeStruct((n_streams * t_len, hv), BF16),
                   jax.ShapeDtypeStruct((n_streams, GLA_HEADS, GLA_DK, GLA_DV), F32)],
        scratch_shapes=[pltpu.VMEM((GLA_DV, GLA_DK), F32)],
        compiler_params=_params("parallel", "parallel", "arbitrary"),
        name="gla_scan",
    )(z, z, z, z, z, w_a_pad, b_a.reshape(1, hk), g_o.reshape(1, GLA_DV), s0)


def _router_kernel(x_ref, g_ref, w_ref, b_ref, ids_ref, gates_ref):
    xn = _rms(x_ref[...], g_ref[...]).astype(BF16)
    logits = jnp.dot(xn, w_ref[...], preferred_element_type=F32) + b_ref[...]
    lane = lax.broadcasted_iota(jnp.int32, logits.shape, 1)
    neg = -jnp.inf

    def top(mask):
        vals = jnp.where(mask, logits, neg)
        m = jnp.max(vals, axis=-1, keepdims=True)
        idx = jnp.min(jnp.where(vals == m, lane, LANES), axis=-1, keepdims=True)
        return m, idx

    is_grp = lane < N_GROUPS
    m_g, grp = top(is_grp)
    p_grp = 1.0 / jnp.sum(jnp.where(is_grp, jnp.exp(logits - m_g), 0.0), axis=-1, keepdims=True)
    lo = N_GROUPS + grp * EXPERTS_PER_GROUP
    in_grp = (lane >= lo) & (lane < lo + EXPERTS_PER_GROUP)
    m1, i1 = top(in_grp)
    m2, i2 = top(in_grp & (lane != i1))
    e2 = jnp.exp(m2 - m1)
    g1 = p_grp / (1.0 + e2)
    g2 = p_grp * e2 / (1.0 + e2)
    ids_ref[...] = jnp.where(lane == 0, i1 - N_GROUPS, jnp.where(lane == 1, i2 - N_GROUPS, 0))
    gates_ref[...] = jnp.where(lane == 0, g1, jnp.where(lane == 1, g2, 0.0))


def _router(x, g_ffn, w_r_pad, b_r_pad):
    n, d = x.shape
    tm = ROW_TILE
    row = lambda i: (i, 0)
    fixed = lambda i: (0, 0)
    return pl.pallas_call(
        _router_kernel,
        grid=(n // tm,),
        in_specs=[pl.BlockSpec((tm, d), row), pl.BlockSpec((1, d), fixed),
                  pl.BlockSpec((d, LANES), fixed), pl.BlockSpec((1, LANES), fixed)],
        out_specs=[pl.BlockSpec((tm, LANES), row), pl.BlockSpec((tm, LANES), row)],
        out_shape=[jax.ShapeDtypeStruct((n, LANES), jnp.int32), jax.ShapeDtypeStruct((n, LANES), F32)],
        compiler_params=_params("parallel"),
        name="router",
    )(x, g_ffn.reshape(1, d), w_r_pad, b_r_pad)


def _row_copy(src_hbm, row, dst_vmem, r, sem):
    return pltpu.make_async_copy(src_hbm.at[pl.ds(row, 1)], dst_vmem.at[pl.ds(r, 1)], sem)


def _ffn_kernel(tile_e_ref, slot_tok_ref, n_used_ref, x_hbm, g_ref, wg_ref, wu_ref, wd_ref,
                y_ref, xbuf, wg_b, wu_b, wd_b, sem):
    i = pl.program_id(0)
    tm = xbuf.shape[0]

    @pl.when(i < n_used_ref[0])
    def _():
        def start(r, _):
            _row_copy(x_hbm, slot_tok_ref[i * tm + r], xbuf, r, sem).start()
            return 0

        lax.fori_loop(0, tm, start, 0)

        @pl.when((i == 0) | (tile_e_ref[i] != tile_e_ref[jnp.maximum(i - 1, 0)]))
        def _():
            wg_b[...] = wg_ref[0].astype(BF16)
            wu_b[...] = wu_ref[0].astype(BF16)
            wd_b[...] = wd_ref[0].astype(BF16)

        def wait(r, _):
            _row_copy(x_hbm, 0, xbuf, r, sem).wait()
            return 0

        lax.fori_loop(0, tm, wait, 0)
        xn = _rms(xbuf[...], g_ref[...]).astype(BF16)
        hg = jnp.dot(xn, wg_b[...], preferred_element_type=F32)
        hu = jnp.dot(xn, wu_b[...], preferred_element_type=F32)
        hid = (hg / (1.0 + jnp.exp(-hg)) * hu).astype(BF16)
        y_ref[...] = jnp.dot(hid, wd_b[...], preferred_element_type=F32)

    @pl.when(i >= n_used_ref[0])
    def _():
        y_ref[...] = jnp.zeros_like(y_ref)


def _expert_ffn(x, g_ffn, w_gate, w_up, w_down, tile_e, slot_tok, n_used):
    n, d = x.shape
    tm = EXPERT_TILE
    n_tiles = tile_e.shape[0]
    de = w_gate.shape[2]
    grid_spec = pltpu.PrefetchScalarGridSpec(
        num_scalar_prefetch=3,
        grid=(n_tiles,),
        in_specs=[pl.BlockSpec(memory_space=pl.ANY),
                  pl.BlockSpec((1, d), lambda i, te, st, nu: (0, 0)),
                  pl.BlockSpec((1, d, de), lambda i, te, st, nu: (te[i], 0, 0)),
                  pl.BlockSpec((1, d, de), lambda i, te, st, nu: (te[i], 0, 0)),
                  pl.BlockSpec((1, de, d), lambda i, te, st, nu: (te[i], 0, 0))],
        out_specs=pl.BlockSpec((tm, d), lambda i, te, st, nu: (i, 0)),
        scratch_shapes=[pltpu.VMEM((tm, d), F32), pltpu.VMEM((d, de), BF16), pltpu.VMEM((d, de), BF16),
                        pltpu.VMEM((de, d), BF16), pltpu.SemaphoreType.DMA(())],
    )
    return pl.pallas_call(
        _ffn_kernel,
        grid_spec=grid_spec,
        out_shape=jax.ShapeDtypeStruct((n_tiles * tm, d), F32),
        compiler_params=_params("arbitrary"),
        name="expert_ffn",
    )(tile_e, slot_tok, n_used, x, g_ffn.reshape(1, d), w_gate, w_up, w_down)


def _combine_kernel(pos_ref, y_hbm, gates_ref, x_ref, *rest, final):
    if final:
        go_ref, o_ref, buf0, buf1, sem = rest
    else:
        o_ref, buf0, buf1, sem = rest
    i = pl.program_id(0)
    tc = buf0.shape[0]

    def start(r, _):
        a = TOP_K * (i * tc + r)
        _row_copy(y_hbm, pos_ref[a], buf0, r, sem).start()
        _row_copy(y_hbm, pos_ref[a + 1], buf1, r, sem).start()
        return 0

    lax.fori_loop(0, tc, start, 0)

    def wait(r, _):
        _row_copy(y_hbm, 0, buf0, r, sem).wait()
        _row_copy(y_hbm, 0, buf1, r, sem).wait()
        return 0

    lax.fori_loop(0, tc, wait, 0)
    gates = gates_ref[...]
    out = x_ref[...] + buf0[...] * gates[:, 0:1] + buf1[...] * gates[:, 1:2]
    if final:
        out = _rms(out, go_ref[...])
    o_ref[...] = out


def _combine(x, y_slots, gates, pos, g_out=None):
    n, d = x.shape
    tc = COMBINE_TILE
    final = g_out is not None
    in_specs = [pl.BlockSpec(memory_space=pl.ANY),
                pl.BlockSpec((tc, LANES), lambda i, p: (i, 0)),
                pl.BlockSpec((tc, d), lambda i, p: (i, 0))]
    args = [pos, y_slots, gates, x]
    if final:
        in_specs.append(pl.BlockSpec((1, d), lambda i, p: (0, 0)))
        args.append(g_out.reshape(1, d))
    grid_spec = pltpu.PrefetchScalarGridSpec(
        num_scalar_prefetch=1,
        grid=(n // tc,),
        in_specs=in_specs,
        out_specs=pl.BlockSpec((tc, d), lambda i, p: (i, 0)),
        scratch_shapes=[pltpu.VMEM((tc, d), F32), pltpu.VMEM((tc, d), F32), pltpu.SemaphoreType.DMA(())],
    )
    return pl.pallas_call(
        functools.partial(_combine_kernel, final=final),
        grid_spec=grid_spec,
        out_shape=jax.ShapeDtypeStruct((n, d), F32),
        compiler_params=_params("arbitrary"),
        name="moe_combine",
    )(*args)


def _dispatch_plan(ids):
    n = ids.shape[0]
    a = n * TOP_K
    tm = EXPERT_TILE
    e_flat = ids[:, :TOP_K].reshape(a)
    tok = jnp.arange(a, dtype=jnp.int32) // TOP_K
    order = jnp.argsort(e_flat, stable=True).astype(jnp.int32)
    e_s = e_flat[order]
    counts = jnp.zeros((N_EXPERTS,), jnp.int32).at[e_flat].add(1)
    padded = (counts + tm - 1) // tm * tm
    p_end = jnp.cumsum(padded)
    p_start = p_end - padded
    start = jnp.cumsum(counts) - counts
    dest = (p_start[e_s] + jnp.arange(a, dtype=jnp.int32) - start[e_s]).astype(jnp.int32)
    n_tiles = (a + N_EXPERTS * (tm - 1)) // tm
    slot_tok = jnp.zeros((n_tiles * tm,), jnp.int32).at[dest].set(tok[order])
    pos = jnp.zeros((a,), jnp.int32).at[order].set(dest)
    tile_start = jnp.arange(n_tiles, dtype=jnp.int32) * tm
    tile_e = jnp.minimum(jnp.searchsorted(p_end, tile_start, side="right"), N_EXPERTS - 1).astype(jnp.int32)
    n_used = (p_end[-1:] // tm).astype(jnp.int32)
    return tile_e, slot_tok, n_used, pos


def _hier_moe(x, g_ffn, w_r_pad, b_r_pad, w_gate, w_up, w_down, g_out=None):
    ids, gates = _router(x, g_ffn, w_r_pad, b_r_pad)
    tile_e, slot_tok, n_used, pos = _dispatch_plan(ids)
    y_slots = _expert_ffn(x, g_ffn, w_gate, w_up, w_down, tile_e, slot_tok, n_used)
    return _combine(x, y_slots, gates, pos, g_out)


def _rope_table(pos):
    inv = jnp.power(ROPE_THETA, -jnp.arange(ROPE_HALF, dtype=F32) * (2.0 / ROPE_DIM))
    ang = pos[:, None] * inv[None, :]
    return jnp.cos(ang), jnp.sin(ang)


def _pad_cols(w, n):
    return jnp.pad(w, ((0, 0), (0, n - w.shape[1])))


def _q_head_weights(w_uq):
    w = w_uq.reshape(Q_LORA, MLA_HEADS, QK_DIM)
    r1 = w[:, :, NOPE_DIM:NOPE_DIM + ROPE_HALF]
    r2 = w[:, :, NOPE_DIM + ROPE_HALF:]
    return jnp.concatenate([w, -r2, r1], axis=-1).reshape(Q_LORA, MLA_HEADS * QK_PAD).astype(BF16)


def kernel(x_prompt, x_sample, cache_mla_ckv, cache_mla_kpe, state_gla, norm_mix, norm_ffn, norm_out, mla_w_in, mla_g_q_lat, mla_g_kv_lat, mla_w_uq, mla_w_uk, mla_w_uv, mla_g_q, mla_g_k, mla_w_o, gla_w_in, gla_w_a, gla_b_a, gla_g_o, gla_w_o, moe_w_router, moe_b_router, moe_w_gate, moe_w_up, moe_w_down):
    batch, seq, d = x_prompt.shape
    nb, t_new, _ = x_sample.shape
    past = cache_mla_ckv.shape[2]
    n_p, n_s = batch * seq, nb * t_new
    n = n_p + n_s
    x = jnp.concatenate([x_prompt.reshape(n_p, d), x_sample.reshape(n_s, d)], axis=0)

    pos_rows = jnp.concatenate([jnp.tile(jnp.arange(seq, dtype=F32), batch),
                                jnp.tile(past + jnp.arange(t_new, dtype=F32), nb)])
    cos_r, sin_r = _rope_table(pos_rows)
    tab = jnp.concatenate([cos_r, cos_r, sin_r, sin_r], axis=1)
    w_in_pad = _pad_cols(mla_w_in[0], 9 * LANES).astype(BF16)
    c_q, c_kv, c_kv_b, k_pe = _mla_in(x, norm_mix[0], w_in_pad, mla_g_q_lat[0], mla_g_kv_lat[0])
    q = _mla_q(c_q, _q_head_weights(mla_w_uq[0]), mla_g_q[0], tab)

    w_uk, w_uv = mla_w_uk[0], mla_w_uv[0]
    w_kv_heads = jnp.concatenate([w_uk, w_uv], axis=-1).reshape(KV_LORA, -1).astype(BF16)
    k_p, v_p = _mla_kv(c_kv_b, k_pe, w_kv_heads, mla_g_k[0], tab, n_p)
    o_p = _flash_prompt(q, k_p, v_p, batch, seq)

    n_keys = past + t_new
    kp = (n_keys + LANES - 1) // LANES * LANES
    c_new = c_kv_b[n_p:].reshape(nb, t_new, KV_LORA)
    c_all = jnp.concatenate([cache_mla_ckv[0].astype(BF16), c_new,
                             jnp.zeros((nb, kp - n_keys, KV_LORA), BF16)], axis=1)
    kpe_new = k_pe[n_p:].reshape(nb, t_new, ROPE_DIM)
    kpe_all = jnp.concatenate([cache_mla_kpe[0], kpe_new,
                               jnp.zeros((nb, kp - n_keys, ROPE_DIM), F32)], axis=1)
    kpet_all = kpe_all.transpose(0, 2, 1)
    cos_k, sin_k = _rope_table(jnp.arange(kp, dtype=F32))
    cost = jnp.concatenate([cos_k, cos_k], axis=1).T
    sint = jnp.concatenate([sin_k, sin_k], axis=1).T
    w_ukt = w_uk.reshape(KV_LORA, -1).T.astype(BF16)
    o_lat = _sample_attn(q, c_all, kpet_all, w_ukt, mla_g_k[0], cost, sint,
                         row_block0=n_p // t_new, n_keys=n_keys, t_new=t_new)
    o_s = _latent_to_values(o_lat, w_uv.reshape(KV_LORA, -1).astype(BF16), t_new)

    o = jnp.concatenate([o_p, o_s], axis=0)
    x = _mm(o, mla_w_o[0].astype(BF16), res=x, out_dtype=F32, tm=MM_ROW_TILE, tn=512)

    w_r_pad = [_pad_cols(moe_w_router[i], LANES).astype(BF16) for i in range(2)]
    b_r_pad = [_pad_cols(moe_b_router[i].reshape(1, -1), LANES) for i in range(2)]
    x = _hier_moe(x, norm_ffn[0], w_r_pad[0], b_r_pad[0], moe_w_gate[0], moe_w_up[0], moe_w_down[0])

    hk, hv = GLA_HEADS * GLA_DK, GLA_HEADS * GLA_DV
    wg = gla_w_in[0]
    w_gla = jnp.concatenate([wg[:, :2 * hk + hv], wg[:, 2 * hk + hv + GATE_RANK:],
                             _pad_cols(wg[:, 2 * hk + hv:2 * hk + hv + GATE_RANK], LANES)],
                            axis=1).astype(BF16)
    z = _mm(x, w_gla, gain=norm_mix[1], out_dtype=BF16, tm=MM_ROW_TILE, tn=7 * LANES)
    w_a_pad = jnp.pad(gla_w_a[0], ((0, LANES - GATE_RANK), (0, 0))).astype(BF16)
    s0_p = jnp.zeros((batch, GLA_HEADS, GLA_DK, GLA_DV), F32)
    og_p, st_p = _gla_scan(z, w_a_pad, gla_b_a[0], gla_g_o[0], s0_p,
                           row0=0, n_streams=batch, t_len=seq, nsub=GLA_SUBCHUNKS)
    og_s, st_s = _gla_scan(z, w_a_pad, gla_b_a[0], gla_g_o[0], state_gla[0],
                           row0=n_p, n_streams=nb, t_len=t_new, nsub=t_new // CHUNK)
    og = jnp.concatenate([og_p, og_s], axis=0)
    x = _mm(og, gla_w_o[0].astype(BF16), res=x, out_dtype=F32, tm=MM_ROW_TILE, tn=512)
    y = _hier_moe(x, norm_ffn[1], w_r_pad[1], b_r_pad[1], moe_w_gate[1], moe_w_up[1], moe_w_down[1],
                  g_out=norm_out)

    return (y[:n_p].reshape(batch, seq, d),
            y[n_p:].reshape(nb, t_new, d),
            c_kv[:n_p].reshape(1, batch, seq, KV_LORA),
            k_pe[:n_p].reshape(1, batch, seq, ROPE_DIM),
            st_p[None],
            c_kv[n_p:].reshape(1, nb, t_new, KV_LORA),
            k_pe[n_p:].reshape(1, nb, t_new, ROPE_DIM),
            st_s[None])
```

```python
import functools

import jax
import jax.numpy as jnp
from jax import lax
from jax.experimental import pallas as pl
from jax.experimental.pallas import tpu as pltpu

F32 = jnp.float32
BF16 = jnp.bfloat16

LANES = 128
V7X_VMEM_BYTES = 64 * 1024 * 1024
VMEM_LIMIT = V7X_VMEM_BYTES * 3 // 4

EPS = 1e-6
CHUNK = 64
MLA_HEADS = 16
NOPE_DIM = 128
ROPE_DIM = 64
ROPE_HALF = ROPE_DIM // 2
QK_DIM = NOPE_DIM + ROPE_DIM
QK_PAD = 2 * LANES
V_DIM = 128
Q_LORA = 512
KV_LORA = 512
ROPE_THETA = 10000.0
ATTN_SCALE = QK_DIM ** -0.5
GLA_HEADS = 4
GLA_DK = 256
GLA_DV = 512
GATE_RANK = 16
GATE_TAU = 16.0
N_GROUPS = 8
EXPERTS_PER_GROUP = 8
N_EXPERTS = N_GROUPS * EXPERTS_PER_GROUP
TOP_K = 2

ROW_TILE = 512
MM_ROW_TILE = 1024
FLASH_TQ = 512
FLASH_TK = 512
SAMPLE_HEAD_GROUP = 4
GLA_SUBCHUNKS = 4
EXPERT_TILE = 128
COMBINE_TILE = 128

NT_DIMS = (((1,), (1,)), ((), ()))
TN_DIMS = (((0,), (0,)), ((), ()))


def _params(*sem):
    return pltpu.CompilerParams(dimension_semantics=sem, vmem_limit_bytes=VMEM_LIMIT)


def _rms(x, g):
    return x * lax.rsqrt(jnp.mean(x * x, axis=-1, keepdims=True) + EPS) * g


def _mm_kernel(*refs, has_norm, has_res):
    it = iter(refs)
    x_ref = next(it)
    g_ref = next(it) if has_norm else None
    w_ref = next(it)
    r_ref = next(it) if has_res else None
    o_ref = next(it)
    if has_norm:
        xn_ref = next(it)

        @pl.when(pl.program_id(1) == 0)
        def _():
            xn_ref[...] = _rms(x_ref[...], g_ref[...]).astype(BF16)

        xb = xn_ref[...]
    else:
        xb = x_ref[...]
    acc = jnp.dot(xb, w_ref[...], preferred_element_type=F32)
    if has_res:
        acc = acc + r_ref[...]
    o_ref[...] = acc.astype(o_ref.dtype)


def _mm(x, w, *, gain=None, res=None, out_dtype, tm, tn):
    m, k = x.shape
    n = w.shape[1]
    has_norm, has_res = gain is not None, res is not None
    in_specs = [pl.BlockSpec((tm, k), lambda i, j: (i, 0))]
    args = [x]
    if has_norm:
        in_specs.append(pl.BlockSpec((1, k), lambda i, j: (0, 0)))
        args.append(gain.reshape(1, k))
    in_specs.append(pl.BlockSpec((k, tn), lambda i, j: (0, j)))
    args.append(w)
    if has_res:
        in_specs.append(pl.BlockSpec((tm, tn), lambda i, j: (i, j)))
        args.append(res)
    return pl.pallas_call(
        functools.partial(_mm_kernel, has_norm=has_norm, has_res=has_res),
        grid=(m // tm, n // tn),
        in_specs=in_specs,
        out_specs=pl.BlockSpec((tm, tn), lambda i, j: (i, j)),
        out_shape=jax.ShapeDtypeStruct((m, n), out_dtype),
        scratch_shapes=[pltpu.VMEM((tm, k), BF16)] if has_norm else [],
        compiler_params=_params("parallel", "arbitrary"),
        name="mm",
    )(*args)


def _mla_in_kernel(x_ref, g_ref, w_ref, gq_ref, gkv_ref, cq_ref, ckv_ref, ckvb_ref, kpe_ref):
    xn = _rms(x_ref[...], g_ref[...]).astype(BF16)
    z = jnp.dot(xn, w_ref[...], preferred_element_type=F32)
    cq_ref[...] = _rms(z[:, :Q_LORA], gq_ref[...]).astype(BF16)
    ckv = _rms(z[:, Q_LORA:Q_LORA + KV_LORA], gkv_ref[...])
    ckv_ref[...] = ckv
    ckvb_ref[...] = ckv.astype(BF16)
    kpe_ref[...] = z[:, Q_LORA + KV_LORA:Q_LORA + KV_LORA + ROPE_DIM]


def _mla_in(x, g_mix, w_in_pad, g_q_lat, g_kv_lat):
    n, d = x.shape
    tm = ROW_TILE
    wn = w_in_pad.shape[1]
    row = lambda i: (i, 0)
    fixed = lambda i: (0, 0)
    return pl.pallas_call(
        _mla_in_kernel,
        grid=(n // tm,),
        in_specs=[pl.BlockSpec((tm, d), row), pl.BlockSpec((1, d), fixed),
                  pl.BlockSpec((d, wn), fixed), pl.BlockSpec((1, Q_LORA), fixed),
                  pl.BlockSpec((1, KV_LORA), fixed)],
        out_specs=[pl.BlockSpec((tm, Q_LORA), row), pl.BlockSpec((tm, KV_LORA), row),
                   pl.BlockSpec((tm, KV_LORA), row), pl.BlockSpec((tm, ROPE_DIM), row)],
        out_shape=[jax.ShapeDtypeStruct((n, Q_LORA), BF16), jax.ShapeDtypeStruct((n, KV_LORA), F32),
                   jax.ShapeDtypeStruct((n, KV_LORA), BF16), jax.ShapeDtypeStruct((n, ROPE_DIM), F32)],
        compiler_params=_params("parallel"),
        name="mla_in",
    )(x, g_mix.reshape(1, d), w_in_pad, g_q_lat.reshape(1, -1), g_kv_lat.reshape(1, -1))


def _mla_q_kernel(cq_ref, w_ref, gn_ref, g2_ref, tab_ref, q_ref):
    cq = cq_ref[...]
    gtab = g2_ref[...] * tab_ref[...]
    is_rope = lax.broadcasted_iota(jnp.int32, gtab.shape, 1) < ROPE_DIM
    for h in range(MLA_HEADS):
        cols = slice(h * QK_PAD, (h + 1) * QK_PAD)
        t = jnp.dot(cq, w_ref[:, cols], preferred_element_type=F32)
        t1 = t[:, :NOPE_DIM]
        t2 = t[:, NOPE_DIM:]
        ss = (jnp.sum(t1 * t1, axis=-1, keepdims=True)
              + jnp.sum(jnp.where(is_rope, t2 * t2, 0.0), axis=-1, keepdims=True))
        rs = lax.rsqrt(ss * (1.0 / QK_DIM) + EPS) * ATTN_SCALE
        u = t2 * gtab
        u = jnp.where(is_rope, u + pltpu.roll(u, ROPE_DIM, axis=1), 0.0)
        q_ref[:, cols] = jnp.concatenate([t1 * gn_ref[...] * rs, u * rs], axis=1).astype(BF16)


def _mla_q(cq, w_q_heads, g_q, tab):
    n = cq.shape[0]
    tm = ROW_TILE
    g1, g2 = g_q[NOPE_DIM:NOPE_DIM + ROPE_HALF], g_q[NOPE_DIM + ROPE_HALF:]
    g_rope = jnp.concatenate([g1, g2, g2, g1]).reshape(1, LANES)
    row = lambda i: (i, 0)
    fixed = lambda i: (0, 0)
    return pl.pallas_call(
        _mla_q_kernel,
        grid=(n // tm,),
        in_specs=[pl.BlockSpec((tm, Q_LORA), row),
                  pl.BlockSpec((Q_LORA, MLA_HEADS * QK_PAD), fixed),
                  pl.BlockSpec((1, NOPE_DIM), fixed),
                  pl.BlockSpec((1, LANES), fixed),
                  pl.BlockSpec((tm, LANES), row)],
        out_specs=pl.BlockSpec((tm, MLA_HEADS * QK_PAD), row),
        out_shape=jax.ShapeDtypeStruct((n, MLA_HEADS * QK_PAD), BF16),
        compiler_params=_params("parallel"),
        name="mla_q",
    )(cq, w_q_heads, g_q[:NOPE_DIM].reshape(1, NOPE_DIM), g_rope, tab)


def _rotate_half_rows(x):
    return jnp.concatenate([-x[ROPE_HALF:], x[:ROPE_HALF]], axis=0)


def _mla_kv_kernel(c_ref, kpe_ref, w_ref, gn_ref, gr_ref, tab_ref, k_ref, v_ref):
    c = c_ref[...]
    kpe = kpe_ref[...]
    sspe = jnp.sum(kpe * kpe, axis=-1, keepdims=True)
    kg = kpe * gr_ref[...]
    rot = jnp.concatenate([-kg[:, ROPE_HALF:], kg[:, :ROPE_HALF]], axis=1)
    tab = tab_ref[...]
    kr = kg * tab[:, :ROPE_DIM] + rot * tab[:, ROPE_DIM:]
    kr = jnp.concatenate([kr, jnp.zeros_like(kr)], axis=1)
    for h in range(MLA_HEADS):
        t = jnp.dot(c, w_ref[:, h * QK_PAD:(h + 1) * QK_PAD], preferred_element_type=F32)
        kn = t[:, :NOPE_DIM]
        rs = lax.rsqrt((jnp.sum(kn * kn, axis=-1, keepdims=True) + sspe) * (1.0 / QK_DIM) + EPS)
        k_ref[:, h * QK_PAD:(h + 1) * QK_PAD] = jnp.concatenate(
            [kn * gn_ref[...] * rs, kr * rs], axis=1).astype(BF16)
        v_ref[:, h * V_DIM:(h + 1) * V_DIM] = t[:, NOPE_DIM:].astype(BF16)


def _mla_kv(ckv_b, kpe, w_kv_heads, g_k, tab, n_rows):
    tm = ROW_TILE
    row = lambda i: (i, 0)
    fixed = lambda i: (0, 0)
    return pl.pallas_call(
        _mla_kv_kernel,
        grid=(n_rows // tm,),
        in_specs=[pl.BlockSpec((tm, KV_LORA), row),
                  pl.BlockSpec((tm, ROPE_DIM), row),
                  pl.BlockSpec((KV_LORA, MLA_HEADS * (NOPE_DIM + V_DIM)), fixed),
                  pl.BlockSpec((1, NOPE_DIM), fixed),
                  pl.BlockSpec((1, ROPE_DIM), fixed),
                  pl.BlockSpec((tm, LANES), row)],
        out_specs=[pl.BlockSpec((tm, MLA_HEADS * QK_PAD), row),
                   pl.BlockSpec((tm, MLA_HEADS * V_DIM), row)],
        out_shape=[jax.ShapeDtypeStruct((n_rows, MLA_HEADS * QK_PAD), BF16),
                   jax.ShapeDtypeStruct((n_rows, MLA_HEADS * V_DIM), BF16)],
        compiler_params=_params("parallel"),
        name="mla_kv",
    )(ckv_b, kpe, w_kv_heads, g_k[:NOPE_DIM].reshape(1, NOPE_DIM),
      g_k[NOPE_DIM:].reshape(1, ROPE_DIM), tab)


def _flash_kernel(q_ref, k_ref, v_ref, o_ref, *, tq, tk):
    qi = pl.program_id(2)
    q = q_ref[...]

    def step(j, carry, masked):
        m, l, acc = carry
        ks = pl.multiple_of(j * tk, tk)
        s = lax.dot_general(q, k_ref[pl.ds(ks, tk), :], NT_DIMS, preferred_element_type=F32)
        if masked:
            row = lax.broadcasted_iota(jnp.int32, s.shape, 0) // CHUNK
            col = lax.broadcasted_iota(jnp.int32, s.shape, 1) // CHUNK
            s = jnp.where(col <= row, s, -jnp.inf)
        m_new = jnp.maximum(m, jnp.max(s, axis=-1, keepdims=True))
        p = jnp.exp(s - m_new)
        alpha = jnp.exp(m - m_new)
        l = alpha * l + jnp.sum(p, axis=-1, keepdims=True)
        acc = alpha * acc + jnp.dot(p.astype(BF16), v_ref[pl.ds(ks, tk), :],
                                    preferred_element_type=F32)
        return m_new, l, acc

    init = (jnp.full((tq, 1), -jnp.inf, F32), jnp.zeros((tq, 1), F32), jnp.zeros((tq, V_DIM), F32))
    carry = lax.fori_loop(0, qi, lambda j, c: step(j, c, False), init)
    _, l, acc = step(qi, carry, True)
    o_ref[...] = (acc / l).astype(BF16)


def _flash_prompt(q, k, v, batch, seq):
    tq, tk = FLASH_TQ, FLASH_TK
    assert tq == tk
    nq = seq // tq
    return pl.pallas_call(
        functools.partial(_flash_kernel, tq=tq, tk=tk),
        grid=(batch, MLA_HEADS, nq),
        in_specs=[pl.BlockSpec((tq, QK_PAD), lambda b, h, i: (b * nq + i, h)),
                  pl.BlockSpec((seq, QK_PAD), lambda b, h, i: (b, h)),
                  pl.BlockSpec((seq, V_DIM), lambda b, h, i: (b, h))],
        out_specs=pl.BlockSpec((tq, V_DIM), lambda b, h, i: (b * nq + i, h)),
        out_shape=jax.ShapeDtypeStruct((batch * seq, MLA_HEADS * V_DIM), BF16),
        compiler_params=_params("parallel", "parallel", "arbitrary"),
        name="flash_prompt",
    )(q, k, v)


def _sample_attn_kernel(q_ref, c_ref, kpet_ref, wukt_ref, gkn_ref, gkrt_ref, cost_ref, sint_ref,
                        olat_ref, qabs_ref, qr_ref, *, n_keys, t_new):
    c = c_ref[0]
    kp = c.shape[0]
    kpet = kpet_ref[0]
    sspe = jnp.sum(kpet * kpet, axis=0, keepdims=True)
    kg = kpet * gkrt_ref[...]
    krt = kg * cost_ref[...] + _rotate_half_rows(kg) * sint_ref[...]
    krt = jnp.concatenate([krt, jnp.zeros_like(krt)], axis=0).astype(BF16)
    gkn = gkn_ref[...]
    for h in range(MLA_HEADS):
        qh = q_ref[:, h * QK_PAD:(h + 1) * QK_PAD]
        qn = (qh[:, :NOPE_DIM].astype(F32) * gkn).astype(BF16)
        qa = jnp.dot(qn, wukt_ref[h * NOPE_DIM:(h + 1) * NOPE_DIM, :], preferred_element_type=F32)
        qabs_ref[h * t_new:(h + 1) * t_new, :] = qa.astype(BF16)
        qr_ref[h * t_new:(h + 1) * t_new, :] = qh[:, NOPE_DIM:]
    valid = lax.broadcasted_iota(jnp.int32, (t_new, kp), 1) < n_keys
    hg = SAMPLE_HEAD_GROUP
    for g in range(MLA_HEADS // hg):
        knt = lax.dot_general(wukt_ref[g * hg * NOPE_DIM:(g + 1) * hg * NOPE_DIM, :], c, NT_DIMS,
                              preferred_element_type=F32)
        ss = jnp.sum((knt * knt).reshape(hg, NOPE_DIM, kp), axis=1) + sspe
        rst = lax.rsqrt(ss * (1.0 / QK_DIM) + EPS)
        rows = slice(g * hg * t_new, (g + 1) * hg * t_new)
        s = (lax.dot_general(qabs_ref[rows, :], c, NT_DIMS, preferred_element_type=F32)
             + jnp.dot(qr_ref[rows, :], krt, preferred_element_type=F32))
        ps = []
        for hh in range(hg):
            sh = s[hh * t_new:(hh + 1) * t_new] * rst[hh:hh + 1]
            sh = jnp.where(valid, sh, -jnp.inf)
            e = jnp.exp(sh - jnp.max(sh, axis=-1, keepdims=True))
            ps.append((e / jnp.sum(e, axis=-1, keepdims=True)).astype(BF16))
        p = jnp.concatenate(ps, axis=0)
        olat_ref[0, rows, :] = jnp.dot(p, c, preferred_element_type=F32).astype(BF16)


def _sample_attn(q, c_all, kpet_all, w_ukt, g_k, cost, sint, *, row_block0, n_keys, t_new):
    nb, kp, _ = c_all.shape
    fixed = lambda b: (0, 0)
    return pl.pallas_call(
        functools.partial(_sample_attn_kernel, n_keys=n_keys, t_new=t_new),
        grid=(nb,),
        in_specs=[pl.BlockSpec((t_new, MLA_HEADS * QK_PAD), lambda b: (row_block0 + b, 0)),
                  pl.BlockSpec((1, kp, KV_LORA), lambda b: (b, 0, 0)),
                  pl.BlockSpec((1, ROPE_DIM, kp), lambda b: (b, 0, 0)),
                  pl.BlockSpec((MLA_HEADS * NOPE_DIM, KV_LORA), fixed),
                  pl.BlockSpec((1, NOPE_DIM), fixed),
                  pl.BlockSpec((ROPE_DIM, 1), fixed),
                  pl.BlockSpec((ROPE_DIM, kp), fixed),
                  pl.BlockSpec((ROPE_DIM, kp), fixed)],
        out_specs=pl.BlockSpec((1, MLA_HEADS * t_new, KV_LORA), lambda b: (b, 0, 0)),
        out_shape=jax.ShapeDtypeStruct((nb, MLA_HEADS * t_new, KV_LORA), BF16),
        scratch_shapes=[pltpu.VMEM((MLA_HEADS * t_new, KV_LORA), BF16),
                        pltpu.VMEM((MLA_HEADS * t_new, LANES), BF16)],
        compiler_params=_params("parallel"),
        name="sample_attn",
    )(q, c_all, kpet_all, w_ukt, g_k[:NOPE_DIM].reshape(1, NOPE_DIM),
      g_k[NOPE_DIM:].reshape(ROPE_DIM, 1), cost, sint)


def _head_mm_kernel(x_ref, w_ref, o_ref):
    nb, t, r = x_ref.shape
    o_ref[...] = jnp.dot(x_ref[...].reshape(nb * t, r), w_ref[...],
                         preferred_element_type=F32).astype(o_ref.dtype)


def _latent_to_values(o_lat, w_uv2d, t_new):
    nb = o_lat.shape[0]
    return pl.pallas_call(
        _head_mm_kernel,
        grid=(MLA_HEADS,),
        in_specs=[pl.BlockSpec((nb, t_new, KV_LORA), lambda h: (0, h, 0)),
                  pl.BlockSpec((KV_LORA, V_DIM), lambda h: (0, h))],
        out_specs=pl.BlockSpec((nb * t_new, V_DIM), lambda h: (0, h)),
        out_shape=jax.ShapeDtypeStruct((nb * t_new, MLA_HEADS * V_DIM), BF16),
        compiler_params=_params("parallel"),
        name="latent_to_values",
    )(o_lat, w_uv2d)


def _gla_kernel(q_ref, k_ref, v_ref, gate_ref, a_ref, wa_ref, ba_ref, go_ref, s0_ref,
                o_ref, sout_ref, st_ref, *, nsub):
    c = pl.program_id(2)

    @pl.when(c == 0)
    def _():
        st_ref[...] = s0_ref[0, 0].T

    tril = (lax.broadcasted_iota(jnp.int32, (CHUNK, CHUNK), 0)
            >= lax.broadcasted_iota(jnp.int32, (CHUNK, CHUNK), 1))
    tril_b = jnp.where(tril, 1.0, 0.0).astype(BF16)
    st = st_ref[...]
    for j in range(nsub):
        sl = slice(j * CHUNK, (j + 1) * CHUNK)
        x = jnp.dot(a_ref[sl, :], wa_ref[...], preferred_element_type=F32) + ba_ref[...]
        la = (jnp.minimum(x, 0.0) - jnp.log(1.0 + jnp.exp(-jnp.abs(x)))) * (1.0 / GATE_TAU)
        la_hi = la.astype(BF16)
        la_lo = (la - la_hi.astype(F32)).astype(BF16)
        b = (jnp.dot(tril_b, la_hi, preferred_element_type=F32)
             + jnp.dot(tril_b, la_lo, preferred_element_type=F32))
        b_last = b[CHUNK - 1:CHUNK, :]
        q = q_ref[sl, :].astype(F32) * (GLA_DK ** -0.5)
        k = k_ref[sl, :].astype(F32)
        v = v_ref[sl, :]
        qs = (q * jnp.exp(b)).astype(BF16)
        ks = (k * jnp.exp(-b)).astype(BF16)
        att = lax.dot_general(qs, ks, NT_DIMS, preferred_element_type=F32)
        att = jnp.where(tril, att, 0.0).astype(BF16)
        o = (jnp.dot(att, v, preferred_element_type=F32)
             + lax.dot_general(qs, st.astype(BF16), NT_DIMS, preferred_element_type=F32))
        kd = (k * jnp.exp(b_last - b)).astype(BF16)
        st = st * jnp.exp(b_last) + lax.dot_general(v, kd, TN_DIMS, preferred_element_type=F32)
        gt = gate_ref[sl, :].astype(F32)
        o_ref[sl, :] = (_rms(o, go_ref[...]) * (gt / (1.0 + jnp.exp(-gt)))).astype(BF16)
    st_ref[...] = st

    @pl.when(c == pl.num_programs(2) - 1)
    def _():
        sout_ref[0, 0] = st.T


def _gla_scan(z, w_a_pad, b_a, g_o, s0, *, row0, n_streams, t_len, nsub):
    tc = nsub * CHUNK
    nc = t_len // tc
    rb0 = row0 // tc
    rows = lambda b, h, c: rb0 + b * nc + c
    hk, hv = GLA_HEADS * GLA_DK, GLA_HEADS * GLA_DV
    a_blk = (2 * hk + 2 * hv) // LANES
    n = z.shape[0]
    del n
    return pl.pallas_call(
        functools.partial(_gla_kernel, nsub=nsub),
        grid=(n_streams, GLA_HEADS, nc),
        in_specs=[pl.BlockSpec((tc, GLA_DK), lambda b, h, c: (rows(b, h, c), h)),
                  pl.BlockSpec((tc, GLA_DK), lambda b, h, c: (rows(b, h, c), hk // GLA_DK + h)),
                  pl.BlockSpec((tc, GLA_DV), lambda b, h, c: (rows(b, h, c), 2 * hk // GLA_DV + h)),
                  pl.BlockSpec((tc, GLA_DV), lambda b, h, c: (rows(b, h, c), (2 * hk + hv) // GLA_DV + h)),
                  pl.BlockSpec((tc, LANES), lambda b, h, c: (rows(b, h, c), a_blk)),
                  pl.BlockSpec((LANES, GLA_DK), lambda b, h, c: (0, h)),
                  pl.BlockSpec((1, GLA_DK), lambda b, h, c: (0, h)),
                  pl.BlockSpec((1, GLA_DV), lambda b, h, c: (0, 0)),
                  pl.BlockSpec((1, 1, GLA_DK, GLA_DV), lambda b, h, c: (b, h, 0, 0))],
        out_specs=[pl.BlockSpec((tc, GLA_DV), lambda b, h, c: (b * nc + c, h)),
                   pl.BlockSpec((1, 1, GLA_DK, GLA_DV), lambda b, h, c: (b, h, 0, 0))],
        out_shape=[jax.ShapeDtypeStruct((n_streams * t_len, hv), BF16),
                   jax.ShapeDtypeStruct((n_streams, GLA_HEADS, GLA_DK, GLA_DV), F32)],
        scratch_shapes=[pltpu.VMEM((GLA_DV, GLA_DK), F32)],
        compiler_params=_params("parallel", "parallel", "arbitrary"),
        name="gla_scan",
    )(z, z, z, z, z, w_a_pad, b_a.reshape(1, hk), g_o.reshape(1, GLA_DV), s0)


def _router_kernel(x_ref, g_ref, w_ref, b_ref, ids_ref, gates_ref):
    xn = _rms(x_ref[...], g_ref[...]).astype(BF16)
    logits = jnp.dot(xn, w_ref[...], preferred_element_type=F32) + b_ref[...]
    lane = lax.broadcasted_iota(jnp.int32, logits.shape, 1)
    neg = -jnp.inf

    def top(mask):
        vals = jnp.where(mask, logits, neg)
        m = jnp.max(vals, axis=-1, keepdims=True)
        idx = jnp.min(jnp.where(vals == m, lane, LANES), axis=-1, keepdims=True)
        return m, idx

    is_grp = lane < N_GROUPS
    m_g, grp = top(is_grp)
    p_grp = 1.0 / jnp.sum(jnp.where(is_grp, jnp.exp(logits - m_g), 0.0), axis=-1, keepdims=True)
    lo = N_GROUPS + grp * EXPERTS_PER_GROUP
    in_grp = (lane >= lo) & (lane < lo + EXPERTS_PER_GROUP)
    m1, i1 = top(in_grp)
    m2, i2 = top(in_grp & (lane != i1))
    e2 = jnp.exp(m2 - m1)
    g1 = p_grp / (1.0 + e2)
    g2 = p_grp * e2 / (1.0 + e2)
    ids_ref[...] = jnp.where(lane == 0, i1 - N_GROUPS, jnp.where(lane == 1, i2 - N_GROUPS, 0))
    gates_ref[...] = jnp.where(lane == 0, g1, jnp.where(lane == 1, g2, 0.0))


def _router(x, g_ffn, w_r_pad, b_r_pad):
    n, d = x.shape
    tm = ROW_TILE
    row = lambda i: (i, 0)
    fixed = lambda i: (0, 0)
    return pl.pallas_call(
        _router_kernel,
        grid=(n // tm,),
        in_specs=[pl.BlockSpec((tm, d), row), pl.BlockSpec((1, d), fixed),
                  pl.BlockSpec((d, LANES), fixed), pl.BlockSpec((1, LANES), fixed)],
        out_specs=[pl.BlockSpec((tm, LANES), row), pl.BlockSpec((tm, LANES), row)],
        out_shape=[jax.ShapeDtypeStruct((n, LANES), jnp.int32), jax.ShapeDtypeStruct((n, LANES), F32)],
        compiler_params=_params("parallel"),
        name="router",
    )(x, g_ffn.reshape(1, d), w_r_pad, b_r_pad)


def _row_copy(src_hbm, row, dst_vmem, r, sem):
    return pltpu.make_async_copy(src_hbm.at[pl.ds(row, 1)], dst_vmem.at[pl.ds(r, 1)], sem)


def _ffn_kernel(tile0_ref, slot_tok_ref, x_hbm, g_ref, wg_ref, wu_ref, wd_ref, y_hbm,
                xbuf, ybuf, wg_b, wu_b, wd_b, gsem, osem):
    e = pl.program_id(0)
    tm = xbuf.shape[1]
    t_lo, t_hi, n_used = tile0_ref[e], tile0_ref[e + 1], tile0_ref[N_EXPERTS]

    def gather_start(t):
        def body(r, _):
            _row_copy(x_hbm, slot_tok_ref[t * tm + r], xbuf.at[t % 2], r, gsem.at[t % 2]).start()
            return 0
        lax.fori_loop(0, tm, body, 0)

    def gather_wait(t):
        def body(r, _):
            _row_copy(x_hbm, 0, xbuf.at[t % 2], r, gsem.at[t % 2]).wait()
            return 0
        lax.fori_loop(0, tm, body, 0)

    def out_copy(t):
        return pltpu.make_async_copy(ybuf.at[t % 2], y_hbm.at[pl.ds(t * tm, tm)], osem.at[t % 2])

    @pl.when((e == 0) & (n_used > 0))
    def _():
        gather_start(0)

    @pl.when(t_hi > t_lo)
    def _():
        wg_b[...] = wg_ref[0, 0].astype(BF16)
        wu_b[...] = wu_ref[0, 0].astype(BF16)
        wd_b[...] = wd_ref[0, 0].astype(BF16)

        def tile(t, _):
            @pl.when(t + 1 < n_used)
            def _():
                gather_start(t + 1)

            gather_wait(t)

            @pl.when(t >= 2)
            def _():
                out_copy(t - 2).wait()

            xn = _rms(xbuf[t % 2], g_ref[...]).astype(BF16)
            hg = jnp.dot(xn, wg_b[...], preferred_element_type=F32)
            hu = jnp.dot(xn, wu_b[...], preferred_element_type=F32)
            hid = (hg / (1.0 + jnp.exp(-hg)) * hu).astype(BF16)
            ybuf[t % 2] = jnp.dot(hid, wd_b[...], preferred_element_type=F32)
            out_copy(t).start()
            return 0

        lax.fori_loop(t_lo, t_hi, tile, 0)

    @pl.when(e == pl.num_programs(0) - 1)
    def _():
        for back in (1, 2):
            @pl.when(n_used >= back)
            def _():
                out_copy(n_used - back).wait()

        n_tiles = y_hbm.shape[0] // tm
        ybuf[0] = jnp.zeros(ybuf.shape[1:], F32)

        def zero_copy(t):
            return pltpu.make_async_copy(ybuf.at[0], y_hbm.at[pl.ds(t * tm, tm)], osem.at[0])

        lax.fori_loop(n_used, n_tiles, lambda t, c: (zero_copy(t).start(), c)[1], 0)
        lax.fori_loop(n_used, n_tiles, lambda t, c: (zero_copy(t).wait(), c)[1], 0)


def _expert_ffn(x, g_ffn, w_gate, w_up, w_down, layer, tile0, slot_tok):
    n, d = x.shape
    tm = EXPERT_TILE
    de = w_gate.shape[3]
    fixed = lambda e, t0, st: (0, 0)
    w_map = lambda e, t0, st: (layer, e, 0, 0)
    grid_spec = pltpu.PrefetchScalarGridSpec(
        num_scalar_prefetch=2,
        grid=(N_EXPERTS,),
        in_specs=[pl.BlockSpec(memory_space=pl.ANY),
                  pl.BlockSpec((1, d), fixed),
                  pl.BlockSpec((1, 1, d, de), w_map),
                  pl.BlockSpec((1, 1, d, de), w_map),
                  pl.BlockSpec((1, 1, de, d), w_map)],
        out_specs=pl.BlockSpec(memory_space=pl.ANY),
        scratch_shapes=[pltpu.VMEM((2, tm, d), F32), pltpu.VMEM((2, tm, d), F32),
                        pltpu.VMEM((d, de), BF16), pltpu.VMEM((d, de), BF16), pltpu.VMEM((de, d), BF16),
                        pltpu.SemaphoreType.DMA((2,)), pltpu.SemaphoreType.DMA((2,))],
    )
    return pl.pallas_call(
        _ffn_kernel,
        grid_spec=grid_spec,
        out_shape=jax.ShapeDtypeStruct((slot_tok.shape[0], d), F32),
        compiler_params=_params("arbitrary"),
        name="expert_ffn",
    )(tile0, slot_tok, x, g_ffn.reshape(1, d), w_gate, w_up, w_down)


def _combine_kernel(pos_ref, y_hbm, gates_ref, x_ref, *rest, final):
    if final:
        go_ref, o_ref, buf0, buf1, sem = rest
    else:
        o_ref, buf0, buf1, sem = rest
    i = pl.program_id(0)
    tc = buf0.shape[1]

    def gather_start(t):
        def body(r, _):
            a = TOP_K * (t * tc + r)
            _row_copy(y_hbm, pos_ref[a], buf0.at[t % 2], r, sem.at[t % 2]).start()
            _row_copy(y_hbm, pos_ref[a + 1], buf1.at[t % 2], r, sem.at[t % 2]).start()
            return 0
        lax.fori_loop(0, tc, body, 0)

    @pl.when(i == 0)
    def _():
        gather_start(0)

    @pl.when(i + 1 < pl.num_programs(0))
    def _():
        gather_start(i + 1)

    def wait(r, _):
        _row_copy(y_hbm, 0, buf0.at[i % 2], r, sem.at[i % 2]).wait()
        _row_copy(y_hbm, 0, buf1.at[i % 2], r, sem.at[i % 2]).wait()
        return 0

    lax.fori_loop(0, tc, wait, 0)
    gates = gates_ref[...]
    out = x_ref[...] + buf0[i % 2] * gates[:, 0:1] + buf1[i % 2] * gates[:, 1:2]
    if final:
        out = _rms(out, go_ref[...])
    o_ref[...] = out


def _combine(x, y_slots, gates, pos, g_out=None):
    n, d = x.shape
    tc = COMBINE_TILE
    final = g_out is not None
    in_specs = [pl.BlockSpec(memory_space=pl.ANY),
                pl.BlockSpec((tc, LANES), lambda i, p: (i, 0)),
                pl.BlockSpec((tc, d), lambda i, p: (i, 0))]
    args = [pos, y_slots, gates, x]
    if final:
        in_specs.append(pl.BlockSpec((1, d), lambda i, p: (0, 0)))
        args.append(g_out.reshape(1, d))
    grid_spec = pltpu.PrefetchScalarGridSpec(
        num_scalar_prefetch=1,
        grid=(n // tc,),
        in_specs=in_specs,
        out_specs=pl.BlockSpec((tc, d), lambda i, p: (i, 0)),
        scratch_shapes=[pltpu.VMEM((2, tc, d), F32), pltpu.VMEM((2, tc, d), F32),
                        pltpu.SemaphoreType.DMA((2,))],
    )
    return pl.pallas_call(
        functools.partial(_combine_kernel, final=final),
        grid_spec=grid_spec,
        out_shape=jax.ShapeDtypeStruct((n, d), F32),
        compiler_params=_params("arbitrary"),
        name="moe_combine",
    )(*args)


def _dispatch_plan(ids):
    n = ids.shape[0]
    a = n * TOP_K
    tm = EXPERT_TILE
    i32 = jnp.int32
    e_flat = ids[:, :TOP_K].reshape(a)
    order = jnp.argsort(e_flat, stable=True).astype(i32)
    rank = jnp.argsort(order).astype(i32)
    experts = jnp.arange(N_EXPERTS, dtype=i32)
    counts = jnp.sum((e_flat[:, None] == experts[None, :]).astype(i32), axis=0)
    start = jnp.cumsum(counts) - counts
    tile_end = jnp.cumsum((counts + tm - 1) // tm)
    tile0 = jnp.concatenate([jnp.zeros((1,), i32), tile_end]).astype(i32)
    p_start = tile0[:N_EXPERTS] * tm
    pos = (p_start[e_flat] + rank - start[e_flat]).astype(i32)
    n_tiles = (a + N_EXPERTS * (tm - 1)) // tm
    tile_ids = jnp.arange(n_tiles, dtype=i32)
    tile_e = jnp.minimum(jnp.sum((tile_end[None, :] <= tile_ids[:, None]).astype(i32), axis=1),
                         N_EXPERTS - 1)
    e_slot = jnp.repeat(tile_e, tm)
    off = jnp.arange(n_tiles * tm, dtype=i32) - p_start[e_slot]
    valid = off < counts[e_slot]
    slot_tok = jnp.where(valid, order[jnp.where(valid, start[e_slot] + off, 0)] // TOP_K, 0).astype(i32)
    return tile0, slot_tok, pos


def _hier_moe(x, g_ffn, w_r_pad, b_r_pad, w_gate, w_up, w_down, layer, g_out=None):
    ids, gates = _router(x, g_ffn, w_r_pad, b_r_pad)
    tile0, slot_tok, pos = _dispatch_plan(ids)
    y_slots = _expert_ffn(x, g_ffn, w_gate, w_up, w_down, layer, tile0, slot_tok)
    return _combine(x, y_slots, gates, pos, g_out)


def _rope_table(pos):
    inv = jnp.power(ROPE_THETA, -jnp.arange(ROPE_HALF, dtype=F32) * (2.0 / ROPE_DIM))
    ang = pos[:, None] * inv[None, :]
    return jnp.cos(ang), jnp.sin(ang)


def _pad_cols(w, n):
    return jnp.pad(w, ((0, 0), (0, n - w.shape[1])))


def _q_head_weights(w_uq):
    w = w_uq.reshape(Q_LORA, MLA_HEADS, QK_DIM)
    r1 = w[:, :, NOPE_DIM:NOPE_DIM + ROPE_HALF]
    r2 = w[:, :, NOPE_DIM + ROPE_HALF:]
    return jnp.concatenate([w, -r2, r1], axis=-1).reshape(Q_LORA, MLA_HEADS * QK_PAD).astype(BF16)


def kernel(x_prompt, x_sample, cache_mla_ckv, cache_mla_kpe, state_gla, norm_mix, norm_ffn, norm_out, mla_w_in, mla_g_q_lat, mla_g_kv_lat, mla_w_uq, mla_w_uk, mla_w_uv, mla_g_q, mla_g_k, mla_w_o, gla_w_in, gla_w_a, gla_b_a, gla_g_o, gla_w_o, moe_w_router, moe_b_router, moe_w_gate, moe_w_up, moe_w_down):
    batch, seq, d = x_prompt.shape
    nb, t_new, _ = x_sample.shape
    past = cache_mla_ckv.shape[2]
    n_p, n_s = batch * seq, nb * t_new
    n = n_p + n_s
    x = jnp.concatenate([x_prompt.reshape(n_p, d), x_sample.reshape(n_s, d)], axis=0)

    pos_rows = jnp.concatenate([jnp.tile(jnp.arange(seq, dtype=F32), batch),
                                jnp.tile(past + jnp.arange(t_new, dtype=F32), nb)])
    cos_r, sin_r = _rope_table(pos_rows)
    tab = jnp.concatenate([cos_r, cos_r, sin_r, sin_r], axis=1)
    w_in_pad = _pad_cols(mla_w_in[0], 9 * LANES).astype(BF16)
    c_q, c_kv, c_kv_b, k_pe = _mla_in(x, norm_mix[0], w_in_pad, mla_g_q_lat[0], mla_g_kv_lat[0])
    q = _mla_q(c_q, _q_head_weights(mla_w_uq[0]), mla_g_q[0], tab)

    w_uk, w_uv = mla_w_uk[0], mla_w_uv[0]
    w_kv_heads = jnp.concatenate([w_uk, w_uv], axis=-1).reshape(KV_LORA, -1).astype(BF16)
    k_p, v_p = _mla_kv(c_kv_b, k_pe, w_kv_heads, mla_g_k[0], tab, n_p)
    o_p = _flash_prompt(q, k_p, v_p, batch, seq)

    n_keys = past + t_new
    kp = (n_keys + LANES - 1) // LANES * LANES
    c_new = c_kv_b[n_p:].reshape(nb, t_new, KV_LORA)
    c_all = jnp.concatenate([cache_mla_ckv[0].astype(BF16), c_new,
                             jnp.zeros((nb, kp - n_keys, KV_LORA), BF16)], axis=1)
    kpe_new = k_pe[n_p:].reshape(nb, t_new, ROPE_DIM)
    kpe_all = jnp.concatenate([cache_mla_kpe[0], kpe_new,
                               jnp.zeros((nb, kp - n_keys, ROPE_DIM), F32)], axis=1)
    kpet_all = kpe_all.transpose(0, 2, 1)
    cos_k, sin_k = _rope_table(jnp.arange(kp, dtype=F32))
    cost = jnp.concatenate([cos_k, cos_k], axis=1).T
    sint = jnp.concatenate([sin_k, sin_k], axis=1).T
    w_ukt = w_uk.reshape(KV_LORA, -1).T.astype(BF16)
    o_lat = _sample_attn(q, c_all, kpet_all, w_ukt, mla_g_k[0], cost, sint,
                         row_block0=n_p // t_new, n_keys=n_keys, t_new=t_new)
    o_s = _latent_to_values(o_lat, w_uv.reshape(KV_LORA, -1).astype(BF16), t_new)

    o = jnp.concatenate([o_p, o_s], axis=0)
    x = _mm(o, mla_w_o[0].astype(BF16), res=x, out_dtype=F32, tm=MM_ROW_TILE, tn=512)

    w_r_pad = [_pad_cols(moe_w_router[i], LANES).astype(BF16) for i in range(2)]
    b_r_pad = [_pad_cols(moe_b_router[i].reshape(1, -1), LANES) for i in range(2)]
    x = _hier_moe(x, norm_ffn[0], w_r_pad[0], b_r_pad[0], moe_w_gate, moe_w_up, moe_w_down, 0)

    hk, hv = GLA_HEADS * GLA_DK, GLA_HEADS * GLA_DV
    wg = gla_w_in[0]
    w_gla = jnp.concatenate([wg[:, :2 * hk + hv], wg[:, 2 * hk + hv + GATE_RANK:],
                             _pad_cols(wg[:, 2 * hk + hv:2 * hk + hv + GATE_RANK], LANES)],
                            axis=1).astype(BF16)
    z = _mm(x, w_gla, gain=norm_mix[1], out_dtype=BF16, tm=MM_ROW_TILE, tn=7 * LANES)
    w_a_pad = jnp.pad(gla_w_a[0], ((0, LANES - GATE_RANK), (0, 0))).astype(BF16)
    s0_p = jnp.zeros((batch, GLA_HEADS, GLA_DK, GLA_DV), F32)
    og_p, st_p = _gla_scan(z, w_a_pad, gla_b_a[0], gla_g_o[0], s0_p,
                           row0=0, n_streams=batch, t_len=seq, nsub=GLA_SUBCHUNKS)
    og_s, st_s = _gla_scan(z, w_a_pad, gla_b_a[0], gla_g_o[0], state_gla[0],
                           row0=n_p, n_streams=nb, t_len=t_new, nsub=t_new // CHUNK)
    og = jnp.concatenate([og_p, og_s], axis=0)
    x = _mm(og, gla_w_o[0].astype(BF16), res=x, out_dtype=F32, tm=MM_ROW_TILE, tn=512)
    y = _hier_moe(x, norm_ffn[1], w_r_pad[1], b_r_pad[1], moe_w_gate, moe_w_up, moe_w_down, 1,
                  g_out=norm_out)

    return (y[:n_p].reshape(batch, seq, d),
            y[n_p:].reshape(nb, t_new, d),
            c_kv[:n_p].reshape(1, batch, seq, KV_LORA),
            k_pe[:n_p].reshape(1, batch, seq, ROPE_DIM),
            st_p[None],
            c_kv[n_p:].reshape(1, nb, t_new, KV_LORA),
            k_pe[n_p:].reshape(1, nb, t_new, ROPE_DIM),
            st_s[None])
```

```python
import functools

import jax
import jax.numpy as jnp
from jax import lax
from jax.experimental import pallas as pl
from jax.experimental.pallas import tpu as pltpu

F32 = jnp.float32
BF16 = jnp.bfloat16

LANES = 128
V7X_VMEM_BYTES = 64 * 1024 * 1024
VMEM_LIMIT = V7X_VMEM_BYTES * 3 // 4

EPS = 1e-6
CHUNK = 64
MLA_HEADS = 16
NOPE_DIM = 128
ROPE_DIM = 64
ROPE_HALF = ROPE_DIM // 2
QK_DIM = NOPE_DIM + ROPE_DIM
QK_PAD = 2 * LANES
V_DIM = 128
Q_LORA = 512
KV_LORA = 512
ROPE_THETA = 10000.0
ATTN_SCALE = QK_DIM ** -0.5
GLA_HEADS = 4
GLA_DK = 256
GLA_DV = 512
GATE_RANK = 16
GATE_TAU = 16.0
N_GROUPS = 8
EXPERTS_PER_GROUP = 8
N_EXPERTS = N_GROUPS * EXPERTS_PER_GROUP
TOP_K = 2
TOP_K_SHIFT = 1
assert 1 << TOP_K_SHIFT == TOP_K

ROW_TILE = 512
MM_ROW_TILE = 1024
FLASH_TQ = 512
FLASH_TK = 512
FLASH_HEADS = 2
SAMPLE_HEAD_GROUP = 4
GLA_SUBCHUNKS = 4
EXPERT_TILE = 128
COMBINE_TILE = 128

NT_DIMS = (((1,), (1,)), ((), ()))
TN_DIMS = (((0,), (0,)), ((), ()))


def _params(*sem):
    return pltpu.CompilerParams(dimension_semantics=sem, vmem_limit_bytes=VMEM_LIMIT)


def _rms(x, g):
    return x * lax.rsqrt(jnp.mean(x * x, axis=-1, keepdims=True) + EPS) * g


def _mm_kernel(*refs, has_norm, has_res, split):
    it = iter(refs)
    x_ref = next(it)
    x2_ref = next(it) if split else None
    g_ref = next(it) if has_norm else None
    w_ref = next(it)
    r_ref = next(it) if has_res else None
    o_ref = next(it)
    if has_norm:
        xn_ref = next(it)

        @pl.when(pl.program_id(1) == 0)
        def _():
            xn_ref[...] = _rms(x_ref[...], g_ref[...]).astype(BF16)

        x_ref = xn_ref

    def emit(src_ref):
        acc = jnp.dot(src_ref[...], w_ref[...], preferred_element_type=F32)
        if has_res:
            acc = acc + r_ref[...]
        o_ref[...] = acc.astype(o_ref.dtype)

    if split:
        pl.when(pl.program_id(0) < split)(lambda: emit(x_ref))
        pl.when(pl.program_id(0) >= split)(lambda: emit(x2_ref))
    else:
        emit(x_ref)


def _mm(x, w, *, x2=None, gain=None, res=None, out_dtype, tm, tn):
    m, k = x.shape
    n = w.shape[1]
    has_norm, has_res = gain is not None, res is not None
    split = 0
    in_specs = [pl.BlockSpec((tm, k), lambda i, j: (i, 0))]
    args = [x]
    if x2 is not None:
        assert not has_norm
        split = m // tm
        m += x2.shape[0]
        in_specs = [pl.BlockSpec((tm, k), lambda i, j: (jnp.minimum(i, split - 1), 0)),
                    pl.BlockSpec((tm, k), lambda i, j: (jnp.maximum(i - split, 0), 0))]
        args.append(x2)
    if has_norm:
        in_specs.append(pl.BlockSpec((1, k), lambda i, j: (0, 0)))
        args.append(gain.reshape(1, k))
    in_specs.append(pl.BlockSpec((k, tn), lambda i, j: (0, j)))
    args.append(w)
    if has_res:
        in_specs.append(pl.BlockSpec((tm, tn), lambda i, j: (i, j)))
        args.append(res)
    return pl.pallas_call(
        functools.partial(_mm_kernel, has_norm=has_norm, has_res=has_res, split=split),
        grid=(m // tm, n // tn),
        in_specs=in_specs,
        out_specs=pl.BlockSpec((tm, tn), lambda i, j: (i, j)),
        out_shape=jax.ShapeDtypeStruct((m, n), out_dtype),
        scratch_shapes=[pltpu.VMEM((tm, k), BF16)] if has_norm else [],
        compiler_params=_params("parallel", "arbitrary"),
        name="mm",
    )(*args)


def _mla_in_kernel(x_ref, g_ref, w_ref, gq_ref, gkv_ref, cq_ref, ckv_ref, ckvb_ref, kpe_ref):
    xn = _rms(x_ref[...], g_ref[...]).astype(BF16)
    z = jnp.dot(xn, w_ref[...], preferred_element_type=F32)
    cq_ref[...] = _rms(z[:, :Q_LORA], gq_ref[...]).astype(BF16)
    ckv = _rms(z[:, Q_LORA:Q_LORA + KV_LORA], gkv_ref[...])
    ckv_ref[...] = ckv
    ckvb_ref[...] = ckv.astype(BF16)
    kpe_ref[...] = z[:, Q_LORA + KV_LORA:Q_LORA + KV_LORA + ROPE_DIM]


def _mla_in(x, g_mix, w_in_pad, g_q_lat, g_kv_lat):
    n, d = x.shape
    tm = ROW_TILE
    wn = w_in_pad.shape[1]
    row = lambda i: (i, 0)
    fixed = lambda i: (0, 0)
    return pl.pallas_call(
        _mla_in_kernel,
        grid=(n // tm,),
        in_specs=[pl.BlockSpec((tm, d), row), pl.BlockSpec((1, d), fixed),
                  pl.BlockSpec((d, wn), fixed), pl.BlockSpec((1, Q_LORA), fixed),
                  pl.BlockSpec((1, KV_LORA), fixed)],
        out_specs=[pl.BlockSpec((tm, Q_LORA), row), pl.BlockSpec((tm, KV_LORA), row),
                   pl.BlockSpec((tm, KV_LORA), row), pl.BlockSpec((tm, ROPE_DIM), row)],
        out_shape=[jax.ShapeDtypeStruct((n, Q_LORA), BF16), jax.ShapeDtypeStruct((n, KV_LORA), F32),
                   jax.ShapeDtypeStruct((n, KV_LORA), BF16), jax.ShapeDtypeStruct((n, ROPE_DIM), F32)],
        compiler_params=_params("parallel"),
        name="mla_in",
    )(x, g_mix.reshape(1, d), w_in_pad, g_q_lat.reshape(1, -1), g_kv_lat.reshape(1, -1))


def _mla_q_kernel(cq_ref, w_ref, gn_ref, g2_ref, tab_ref, q_ref):
    cq = cq_ref[...]
    gtab = g2_ref[...] * tab_ref[...]
    is_rope = lax.broadcasted_iota(jnp.int32, gtab.shape, 1) < ROPE_DIM
    for h in range(MLA_HEADS):
        cols = slice(h * QK_PAD, (h + 1) * QK_PAD)
        t = jnp.dot(cq, w_ref[:, cols], preferred_element_type=F32)
        t1 = t[:, :NOPE_DIM]
        t2 = t[:, NOPE_DIM:]
        ss = (jnp.sum(t1 * t1, axis=-1, keepdims=True)
              + jnp.sum(jnp.where(is_rope, t2 * t2, 0.0), axis=-1, keepdims=True))
        rs = lax.rsqrt(ss * (1.0 / QK_DIM) + EPS) * ATTN_SCALE
        u = t2 * gtab
        u = jnp.where(is_rope, u + pltpu.roll(u, ROPE_DIM, axis=1), 0.0)
        q_ref[:, cols] = jnp.concatenate([t1 * gn_ref[...] * rs, u * rs], axis=1).astype(BF16)


def _mla_q(cq, w_q_heads, g_q, tab):
    n = cq.shape[0]
    tm = ROW_TILE
    g1, g2 = g_q[NOPE_DIM:NOPE_DIM + ROPE_HALF], g_q[NOPE_DIM + ROPE_HALF:]
    g_rope = jnp.concatenate([g1, g2, g2, g1]).reshape(1, LANES)
    row = lambda i: (i, 0)
    fixed = lambda i: (0, 0)
    return pl.pallas_call(
        _mla_q_kernel,
        grid=(n // tm,),
        in_specs=[pl.BlockSpec((tm, Q_LORA), row),
                  pl.BlockSpec((Q_LORA, MLA_HEADS * QK_PAD), fixed),
                  pl.BlockSpec((1, NOPE_DIM), fixed),
                  pl.BlockSpec((1, LANES), fixed),
                  pl.BlockSpec((tm, LANES), row)],
        out_specs=pl.BlockSpec((tm, MLA_HEADS * QK_PAD), row),
        out_shape=jax.ShapeDtypeStruct((n, MLA_HEADS * QK_PAD), BF16),
        compiler_params=_params("parallel"),
        name="mla_q",
    )(cq, w_q_heads, g_q[:NOPE_DIM].reshape(1, NOPE_DIM), g_rope, tab)


def _rotate_half_rows(x):
    return jnp.concatenate([-x[ROPE_HALF:], x[:ROPE_HALF]], axis=0)


def _mla_kv_kernel(c_ref, kpe_ref, w_ref, gn_ref, gr_ref, tab_ref, k_ref, v_ref):
    c = c_ref[...]
    kpe = kpe_ref[...]
    sspe = jnp.sum(kpe * kpe, axis=-1, keepdims=True)
    kg = kpe * gr_ref[...]
    rot = jnp.concatenate([-kg[:, ROPE_HALF:], kg[:, :ROPE_HALF]], axis=1)
    tab = tab_ref[...]
    kr = kg * tab[:, :ROPE_DIM] + rot * tab[:, ROPE_DIM:]
    kr = jnp.concatenate([kr, jnp.zeros_like(kr)], axis=1)
    for h in range(MLA_HEADS):
        t = jnp.dot(c, w_ref[:, h * QK_PAD:(h + 1) * QK_PAD], preferred_element_type=F32)
        kn = t[:, :NOPE_DIM]
        rs = lax.rsqrt((jnp.sum(kn * kn, axis=-1, keepdims=True) + sspe) * (1.0 / QK_DIM) + EPS)
        k_ref[:, h * QK_PAD:(h + 1) * QK_PAD] = jnp.concatenate(
            [kn * gn_ref[...] * rs, kr * rs], axis=1).astype(BF16)
        v_ref[:, h * V_DIM:(h + 1) * V_DIM] = t[:, NOPE_DIM:].astype(BF16)


def _mla_kv(ckv_b, kpe, w_kv_heads, g_k, tab, n_rows):
    tm = ROW_TILE
    row = lambda i: (i, 0)
    fixed = lambda i: (0, 0)
    return pl.pallas_call(
        _mla_kv_kernel,
        grid=(n_rows // tm,),
        in_specs=[pl.BlockSpec((tm, KV_LORA), row),
                  pl.BlockSpec((tm, ROPE_DIM), row),
                  pl.BlockSpec((KV_LORA, MLA_HEADS * (NOPE_DIM + V_DIM)), fixed),
                  pl.BlockSpec((1, NOPE_DIM), fixed),
                  pl.BlockSpec((1, ROPE_DIM), fixed),
                  pl.BlockSpec((tm, LANES), row)],
        out_specs=[pl.BlockSpec((tm, MLA_HEADS * QK_PAD), row),
                   pl.BlockSpec((tm, MLA_HEADS * V_DIM), row)],
        out_shape=[jax.ShapeDtypeStruct((n_rows, MLA_HEADS * QK_PAD), BF16),
                   jax.ShapeDtypeStruct((n_rows, MLA_HEADS * V_DIM), BF16)],
        compiler_params=_params("parallel"),
        name="mla_kv",
    )(ckv_b, kpe, w_kv_heads, g_k[:NOPE_DIM].reshape(1, NOPE_DIM),
      g_k[NOPE_DIM:].reshape(1, ROPE_DIM), tab)


def _flash_kernel(q_ref, k_ref, v_ref, o_ref, *, tq, tk):
    qi = pl.program_id(2)

    def step(j, carries, masked):
        ks = pl.multiple_of(j * tk, tk)
        out = []
        for hh, (m, l, acc) in enumerate(carries):
            qk = slice(hh * QK_PAD, (hh + 1) * QK_PAD)
            vv = slice(hh * V_DIM, (hh + 1) * V_DIM)
            s = lax.dot_general(q_ref[:, qk], k_ref[pl.ds(ks, tk), qk], NT_DIMS,
                                preferred_element_type=F32)
            if masked:
                row = lax.broadcasted_iota(jnp.int32, s.shape, 0) // CHUNK
                col = lax.broadcasted_iota(jnp.int32, s.shape, 1) // CHUNK
                s = jnp.where(col <= row, s, -jnp.inf)
            m_new = jnp.maximum(m, jnp.max(s, axis=-1, keepdims=True))
            p = jnp.exp(s - m_new)
            alpha = jnp.exp(m - m_new)
            l = alpha * l + jnp.sum(p, axis=-1, keepdims=True)
            acc = alpha * acc + jnp.dot(p.astype(BF16), v_ref[pl.ds(ks, tk), vv],
                                        preferred_element_type=F32)
            out.append((m_new, l, acc))
        return tuple(out)

    init = tuple((jnp.full((tq, 1), -jnp.inf, F32), jnp.zeros((tq, 1), F32),
                  jnp.zeros((tq, V_DIM), F32)) for _ in range(FLASH_HEADS))
    carries = lax.fori_loop(0, qi, lambda j, c: step(j, c, False), init)
    carries = step(qi, carries, True)
    for hh, (_, l, acc) in enumerate(carries):
        o_ref[:, hh * V_DIM:(hh + 1) * V_DIM] = (acc / l).astype(BF16)


def _flash_prompt(q, k, v, batch, seq):
    tq, tk, hs = FLASH_TQ, FLASH_TK, FLASH_HEADS
    assert tq == tk
    nq = seq // tq
    return pl.pallas_call(
        functools.partial(_flash_kernel, tq=tq, tk=tk),
        grid=(batch, MLA_HEADS // hs, nq),
        in_specs=[pl.BlockSpec((tq, hs * QK_PAD), lambda b, h, i: (b * nq + i, h)),
                  pl.BlockSpec((seq, hs * QK_PAD), lambda b, h, i: (b, h)),
                  pl.BlockSpec((seq, hs * V_DIM), lambda b, h, i: (b, h))],
        out_specs=pl.BlockSpec((tq, hs * V_DIM), lambda b, h, i: (b * nq + i, h)),
        out_shape=jax.ShapeDtypeStruct((batch * seq, MLA_HEADS * V_DIM), BF16),
        compiler_params=_params("parallel", "parallel", "arbitrary"),
        name="flash_prompt",
    )(q, k, v)


def _sample_attn_kernel(q_ref, c_ref, kpet_ref, wukt_ref, gkn_ref, gkrt_ref, cost_ref, sint_ref,
                        olat_ref, qabs_ref, qr_ref, *, n_keys, t_new):
    c = c_ref[0]
    kp = c.shape[0]
    kpet = kpet_ref[0]
    sspe = jnp.sum(kpet * kpet, axis=0, keepdims=True)
    kg = kpet * gkrt_ref[...]
    krt = kg * cost_ref[...] + _rotate_half_rows(kg) * sint_ref[...]
    krt = jnp.concatenate([krt, jnp.zeros_like(krt)], axis=0).astype(BF16)
    gkn = gkn_ref[...]
    for h in range(MLA_HEADS):
        qh = q_ref[:, h * QK_PAD:(h + 1) * QK_PAD]
        qn = (qh[:, :NOPE_DIM].astype(F32) * gkn).astype(BF16)
        qa = jnp.dot(qn, wukt_ref[h * NOPE_DIM:(h + 1) * NOPE_DIM, :], preferred_element_type=F32)
        qabs_ref[h * t_new:(h + 1) * t_new, :] = qa.astype(BF16)
        qr_ref[h * t_new:(h + 1) * t_new, :] = qh[:, NOPE_DIM:]
    valid = lax.broadcasted_iota(jnp.int32, (t_new, kp), 1) < n_keys
    hg = SAMPLE_HEAD_GROUP
    for g in range(MLA_HEADS // hg):
        knt = lax.dot_general(wukt_ref[g * hg * NOPE_DIM:(g + 1) * hg * NOPE_DIM, :], c, NT_DIMS,
                              preferred_element_type=F32)
        ss = jnp.sum((knt * knt).reshape(hg, NOPE_DIM, kp), axis=1) + sspe
        rst = lax.rsqrt(ss * (1.0 / QK_DIM) + EPS)
        rows = slice(g * hg * t_new, (g + 1) * hg * t_new)
        s = (lax.dot_general(qabs_ref[rows, :], c, NT_DIMS, preferred_element_type=F32)
             + jnp.dot(qr_ref[rows, :], krt, preferred_element_type=F32))
        ps = []
        for hh in range(hg):
            sh = s[hh * t_new:(hh + 1) * t_new] * rst[hh:hh + 1]
            sh = jnp.where(valid, sh, -jnp.inf)
            e = jnp.exp(sh - jnp.max(sh, axis=-1, keepdims=True))
            ps.append((e / jnp.sum(e, axis=-1, keepdims=True)).astype(BF16))
        p = jnp.concatenate(ps, axis=0)
        olat_ref[0, rows, :] = jnp.dot(p, c, preferred_element_type=F32).astype(BF16)


def _sample_attn(q, c_all, kpet_all, w_ukt, g_k, cost, sint, *, row_block0, n_keys, t_new):
    nb, kp, _ = c_all.shape
    fixed = lambda b: (0, 0)
    return pl.pallas_call(
        functools.partial(_sample_attn_kernel, n_keys=n_keys, t_new=t_new),
        grid=(nb,),
        in_specs=[pl.BlockSpec((t_new, MLA_HEADS * QK_PAD), lambda b: (row_block0 + b, 0)),
                  pl.BlockSpec((1, kp, KV_LORA), lambda b: (b, 0, 0)),
                  pl.BlockSpec((1, ROPE_DIM, kp), lambda b: (b, 0, 0)),
                  pl.BlockSpec((MLA_HEADS * NOPE_DIM, KV_LORA), fixed),
                  pl.BlockSpec((1, NOPE_DIM), fixed),
                  pl.BlockSpec((ROPE_DIM, 1), fixed),
                  pl.BlockSpec((ROPE_DIM, kp), fixed),
                  pl.BlockSpec((ROPE_DIM, kp), fixed)],
        out_specs=pl.BlockSpec((1, MLA_HEADS * t_new, KV_LORA), lambda b: (b, 0, 0)),
        out_shape=jax.ShapeDtypeStruct((nb, MLA_HEADS * t_new, KV_LORA), BF16),
        scratch_shapes=[pltpu.VMEM((MLA_HEADS * t_new, KV_LORA), BF16),
                        pltpu.VMEM((MLA_HEADS * t_new, LANES), BF16)],
        compiler_params=_params("parallel"),
        name="sample_attn",
    )(q, c_all, kpet_all, w_ukt, g_k[:NOPE_DIM].reshape(1, NOPE_DIM),
      g_k[NOPE_DIM:].reshape(ROPE_DIM, 1), cost, sint)


def _head_mm_kernel(x_ref, w_ref, o_ref):
    nb, t, r = x_ref.shape
    o_ref[...] = jnp.dot(x_ref[...].reshape(nb * t, r), w_ref[...],
                         preferred_element_type=F32).astype(o_ref.dtype)


def _latent_to_values(o_lat, w_uv2d, t_new):
    nb = o_lat.shape[0]
    return pl.pallas_call(
        _head_mm_kernel,
        grid=(MLA_HEADS,),
        in_specs=[pl.BlockSpec((nb, t_new, KV_LORA), lambda h: (0, h, 0)),
                  pl.BlockSpec((KV_LORA, V_DIM), lambda h: (0, h))],
        out_specs=pl.BlockSpec((nb * t_new, V_DIM), lambda h: (0, h)),
        out_shape=jax.ShapeDtypeStruct((nb * t_new, MLA_HEADS * V_DIM), BF16),
        compiler_params=_params("parallel"),
        name="latent_to_values",
    )(o_lat, w_uv2d)


def _gla_kernel(q_ref, k_ref, v_ref, gate_ref, a_ref, wa_ref, ba_ref, go_ref, s0_ref,
                o_ref, sout_ref, st_ref, *, nsub):
    c = pl.program_id(2)

    @pl.when(c == 0)
    def _():
        st_ref[...] = s0_ref[0, 0].T

    tril = (lax.broadcasted_iota(jnp.int32, (CHUNK, CHUNK), 0)
            >= lax.broadcasted_iota(jnp.int32, (CHUNK, CHUNK), 1))
    tril_b = jnp.where(tril, 1.0, 0.0).astype(BF16)
    st = st_ref[...]
    for j in range(nsub):
        sl = slice(j * CHUNK, (j + 1) * CHUNK)
        x = jnp.dot(a_ref[sl, :], wa_ref[...], preferred_element_type=F32) + ba_ref[...]
        la = (jnp.minimum(x, 0.0) - jnp.log(1.0 + jnp.exp(-jnp.abs(x)))) * (1.0 / GATE_TAU)
        la_hi = la.astype(BF16)
        la_lo = (la - la_hi.astype(F32)).astype(BF16)
        b = (jnp.dot(tril_b, la_hi, preferred_element_type=F32)
             + jnp.dot(tril_b, la_lo, preferred_element_type=F32))
        b_last = b[CHUNK - 1:CHUNK, :]
        q = q_ref[sl, :].astype(F32) * (GLA_DK ** -0.5)
        k = k_ref[sl, :].astype(F32)
        v = v_ref[sl, :]
        qs = (q * jnp.exp(b)).astype(BF16)
        ks = (k * jnp.exp(-b)).astype(BF16)
        att = lax.dot_general(qs, ks, NT_DIMS, preferred_element_type=F32)
        att = jnp.where(tril, att, 0.0).astype(BF16)
        o = (jnp.dot(att, v, preferred_element_type=F32)
             + lax.dot_general(qs, st.astype(BF16), NT_DIMS, preferred_element_type=F32))
        kd = (k * jnp.exp(b_last - b)).astype(BF16)
        st = st * jnp.exp(b_last) + lax.dot_general(v, kd, TN_DIMS, preferred_element_type=F32)
        gt = gate_ref[sl, :].astype(F32)
        o_ref[sl, :] = (_rms(o, go_ref[...]) * (gt / (1.0 + jnp.exp(-gt)))).astype(BF16)
    st_ref[...] = st

    @pl.when(c == pl.num_programs(2) - 1)
    def _():
        sout_ref[0, 0] = st.T


def _gla_scan(z, w_a_pad, b_a, g_o, s0, *, row0, n_streams, t_len, nsub):
    tc = nsub * CHUNK
    nc = t_len // tc
    rb0 = row0 // tc
    rows = lambda b, h, c: rb0 + b * nc + c
    hk, hv = GLA_HEADS * GLA_DK, GLA_HEADS * GLA_DV
    a_blk = (2 * hk + 2 * hv) // LANES
    n = z.shape[0]
    del n
    return pl.pallas_call(
        functools.partial(_gla_kernel, nsub=nsub),
        grid=(n_streams, GLA_HEADS, nc),
        in_specs=[pl.BlockSpec((tc, GLA_DK), lambda b, h, c: (rows(b, h, c), h)),
                  pl.BlockSpec((tc, GLA_DK), lambda b, h, c: (rows(b, h, c), hk // GLA_DK + h)),
                  pl.BlockSpec((tc, GLA_DV), lambda b, h, c: (rows(b, h, c), 2 * hk // GLA_DV + h)),
                  pl.BlockSpec((tc, GLA_DV), lambda b, h, c: (rows(b, h, c), (2 * hk + hv) // GLA_DV + h)),
                  pl.BlockSpec((tc, LANES), lambda b, h, c: (rows(b, h, c), a_blk)),
                  pl.BlockSpec((LANES, GLA_DK), lambda b, h, c: (0, h)),
                  pl.BlockSpec((1, GLA_DK), lambda b, h, c: (0, h)),
                  pl.BlockSpec((1, GLA_DV), lambda b, h, c: (0, 0)),
                  pl.BlockSpec((1, 1, GLA_DK, GLA_DV), lambda b, h, c: (b, h, 0, 0))],
        out_specs=[pl.BlockSpec((tc, GLA_DV), lambda b, h, c: (b * nc + c, h)),
                   pl.BlockSpec((1, 1, GLA_DK, GLA_DV), lambda b, h, c: (b, h, 0, 0))],
        out_shape=[jax.ShapeDtypeStruct((n_streams * t_len, hv), BF16),
                   jax.ShapeDtypeStruct((n_streams, GLA_HEADS, GLA_DK, GLA_DV), F32)],
        scratch_shapes=[pltpu.VMEM((GLA_DV, GLA_DK), F32)],
        compiler_params=_params("parallel", "parallel", "arbitrary"),
        name="gla_scan",
    )(z, z, z, z, z, w_a_pad, b_a.reshape(1, hk), g_o.reshape(1, GLA_DV), s0)


def _router_kernel(x_ref, g_ref, w_ref, b_ref, ids_ref, gates_ref):
    xn = _rms(x_ref[...], g_ref[...]).astype(BF16)
    logits = jnp.dot(xn, w_ref[...], preferred_element_type=F32) + b_ref[...]
    lane = lax.broadcasted_iota(jnp.int32, logits.shape, 1)
    neg = -jnp.inf

    def top(mask):
        vals = jnp.where(mask, logits, neg)
        m = jnp.max(vals, axis=-1, keepdims=True)
        idx = jnp.min(jnp.where(vals == m, lane, LANES), axis=-1, keepdims=True)
        return m, idx

    is_grp = lane < N_GROUPS
    m_g, grp = top(is_grp)
    p_grp = 1.0 / jnp.sum(jnp.where(is_grp, jnp.exp(logits - m_g), 0.0), axis=-1, keepdims=True)
    lo = N_GROUPS + grp * EXPERTS_PER_GROUP
    in_grp = (lane >= lo) & (lane < lo + EXPERTS_PER_GROUP)
    m1, i1 = top(in_grp)
    m2, i2 = top(in_grp & (lane != i1))
    e2 = jnp.exp(m2 - m1)
    g1 = p_grp / (1.0 + e2)
    g2 = p_grp * e2 / (1.0 + e2)
    ids_ref[...] = jnp.where(lane == 0, i1 - N_GROUPS, jnp.where(lane == 1, i2 - N_GROUPS, 0))
    gates_ref[...] = jnp.where(lane == 0, g1, jnp.where(lane == 1, g2, 0.0))


def _router(x, g_ffn, w_r_pad, b_r_pad):
    n, d = x.shape
    tm = ROW_TILE
    row = lambda i: (i, 0)
    fixed = lambda i: (0, 0)
    return pl.pallas_call(
        _router_kernel,
        grid=(n // tm,),
        in_specs=[pl.BlockSpec((tm, d), row), pl.BlockSpec((1, d), fixed),
                  pl.BlockSpec((d, LANES), fixed), pl.BlockSpec((1, LANES), fixed)],
        out_specs=[pl.BlockSpec((tm, LANES), row), pl.BlockSpec((tm, LANES), row)],
        out_shape=[jax.ShapeDtypeStruct((n, LANES), jnp.int32), jax.ShapeDtypeStruct((n, LANES), F32)],
        compiler_params=_params("parallel"),
        name="router",
    )(x, g_ffn.reshape(1, d), w_r_pad, b_r_pad)


def _row_copy(src_hbm, row, dst_vmem, r, sem):
    return pltpu.make_async_copy(src_hbm.at[pl.ds(row, 1)], dst_vmem.at[pl.ds(r, 1)], sem)


def _ffn_kernel(tile0_ref, tile_e_ref, start_ref, order_ref, x_hbm, g_ref, wg_ref, wu_ref, wd_ref,
                y_hbm, pos_ref, xbuf, ybuf, wg_b, wu_b, wd_b, gsem, osem):
    e = pl.program_id(0)
    tm = xbuf.shape[1]
    t_lo, t_hi, n_used = tile0_ref[e], tile0_ref[e + 1], tile0_ref[N_EXPERTS]

    def gather_start(t, slot, unrolled):
        ee = tile_e_ref[t]
        base = start_ref[ee]
        last = start_ref[ee + 1] - base - 1
        off0 = (t - tile0_ref[ee]) * tm

        def one(r):
            a = order_ref[base + jnp.minimum(off0 + r, last)]
            tok = lax.shift_right_logical(a, TOP_K_SHIFT)
            _row_copy(x_hbm, tok, xbuf.at[slot], r, gsem.at[slot]).start()
            pos_ref[a] = t * tm + r

        if unrolled:
            for r in range(tm):
                one(r)
        else:
            lax.fori_loop(0, tm, lambda r, c: (one(r), c)[1], 0)

    def gather_wait(slot):
        pltpu.make_async_copy(x_hbm.at[pl.ds(0, tm)], xbuf.at[slot], gsem.at[slot]).wait()

    def out_copy(t):
        return pltpu.make_async_copy(ybuf.at[t & 1], y_hbm.at[pl.ds(t * tm, tm)], osem.at[t & 1])

    @pl.when((e == 0) & (n_used > 0))
    def _():
        gather_start(0, 0, False)

    @pl.when(t_hi > t_lo)
    def _():
        wg_b[...] = wg_ref[0, 0].astype(BF16)
        wu_b[...] = wu_ref[0, 0].astype(BF16)
        wd_b[...] = wd_ref[0, 0].astype(BF16)

        def tile(t, _):
            gather_wait(t & 1)

            @pl.when(t >= 2)
            def _():
                out_copy(t - 2).wait()

            xn = _rms(xbuf[t & 1], g_ref[...]).astype(BF16)
            gather_start(jnp.minimum(t + 1, n_used - 1), (t + 1) & 1, True)
            hg = jnp.dot(xn, wg_b[...], preferred_element_type=F32)
            hu = jnp.dot(xn, wu_b[...], preferred_element_type=F32)
            hid = (hg / (1.0 + jnp.exp(-hg)) * hu).astype(BF16)
            ybuf[t & 1] = jnp.dot(hid, wd_b[...], preferred_element_type=F32)
            out_copy(t).start()
            return 0

        lax.fori_loop(t_lo, t_hi, tile, 0)

    @pl.when(e == pl.num_programs(0) - 1)
    def _():
        @pl.when(n_used >= 1)
        def _():
            gather_wait(n_used & 1)

        for back in (1, 2):
            @pl.when(n_used >= back)
            def _():
                out_copy(n_used - back).wait()

        n_tiles = y_hbm.shape[0] // tm
        ybuf[0] = jnp.zeros(ybuf.shape[1:], F32)

        def zero_copy(t):
            return pltpu.make_async_copy(ybuf.at[0], y_hbm.at[pl.ds(t * tm, tm)], osem.at[0])

        lax.fori_loop(n_used, n_tiles, lambda t, c: (zero_copy(t).start(), c)[1], 0)
        lax.fori_loop(n_used, n_tiles, lambda t, c: (zero_copy(t).wait(), c)[1], 0)


def _expert_ffn(x, g_ffn, w_gate, w_up, w_down, layer, tile0, tile_e, start, order):
    n, d = x.shape
    tm = EXPERT_TILE
    de = w_gate.shape[3]
    fixed = lambda e, *_: (0, 0)
    w_map = lambda e, *_: (layer, e, 0, 0)
    grid_spec = pltpu.PrefetchScalarGridSpec(
        num_scalar_prefetch=4,
        grid=(N_EXPERTS,),
        in_specs=[pl.BlockSpec(memory_space=pl.ANY),
                  pl.BlockSpec((1, d), fixed),
                  pl.BlockSpec((1, 1, d, de), w_map),
                  pl.BlockSpec((1, 1, d, de), w_map),
                  pl.BlockSpec((1, 1, de, d), w_map)],
        out_specs=[pl.BlockSpec(memory_space=pl.ANY), pl.BlockSpec(memory_space=pltpu.SMEM)],
        scratch_shapes=[pltpu.VMEM((2, tm, d), F32), pltpu.VMEM((2, tm, d), F32),
                        pltpu.VMEM((d, de), BF16), pltpu.VMEM((d, de), BF16), pltpu.VMEM((de, d), BF16),
                        pltpu.SemaphoreType.DMA((2,)), pltpu.SemaphoreType.DMA((2,))],
    )
    return pl.pallas_call(
        _ffn_kernel,
        grid_spec=grid_spec,
        out_shape=[jax.ShapeDtypeStruct((tile_e.shape[0] * tm, d), F32),
                   jax.ShapeDtypeStruct(order.shape, jnp.int32)],
        compiler_params=_params("arbitrary"),
        name="expert_ffn",
    )(tile0, tile_e, start, order, x, g_ffn.reshape(1, d), w_gate, w_up, w_down)


def _combine_kernel(pos_ref, y_hbm, gates_ref, x_ref, *rest, final):
    if final:
        go_ref, o_ref, buf0, buf1, sem = rest
    else:
        o_ref, buf0, buf1, sem = rest
    i = pl.program_id(0)
    tc = buf0.shape[1]

    last = pl.num_programs(0) - 1

    def gather_start(t, slot, unrolled):
        def one(r):
            a = TOP_K * (t * tc + r)
            _row_copy(y_hbm, pos_ref[a], buf0.at[slot], r, sem.at[slot]).start()
            _row_copy(y_hbm, pos_ref[a + 1], buf1.at[slot], r, sem.at[slot]).start()

        if unrolled:
            for r in range(tc):
                one(r)
        else:
            lax.fori_loop(0, tc, lambda r, c: (one(r), c)[1], 0)

    def gather_wait(slot):
        for buf in (buf0, buf1):
            pltpu.make_async_copy(y_hbm.at[pl.ds(0, tc)], buf.at[slot], sem.at[slot]).wait()

    @pl.when(i == 0)
    def _():
        gather_start(0, 0, False)

    gather_wait(i & 1)
    gates = gates_ref[...]
    out = x_ref[...] + buf0[i & 1] * gates[:, 0:1] + buf1[i & 1] * gates[:, 1:2]
    gather_start(jnp.minimum(i + 1, last), (i + 1) & 1, True)
    if final:
        out = _rms(out, go_ref[...])
    o_ref[...] = out

    @pl.when(i == last)
    def _():
        gather_wait((i + 1) & 1)


def _combine(x, y_slots, gates, pos, g_out=None):
    n, d = x.shape
    tc = COMBINE_TILE
    final = g_out is not None
    in_specs = [pl.BlockSpec(memory_space=pl.ANY),
                pl.BlockSpec((tc, LANES), lambda i, p: (i, 0)),
                pl.BlockSpec((tc, d), lambda i, p: (i, 0))]
    args = [pos, y_slots, gates, x]
    if final:
        in_specs.append(pl.BlockSpec((1, d), lambda i, p: (0, 0)))
        args.append(g_out.reshape(1, d))
    grid_spec = pltpu.PrefetchScalarGridSpec(
        num_scalar_prefetch=1,
        grid=(n // tc,),
        in_specs=in_specs,
        out_specs=pl.BlockSpec((tc, d), lambda i, p: (i, 0)),
        scratch_shapes=[pltpu.VMEM((2, tc, d), F32), pltpu.VMEM((2, tc, d), F32),
                        pltpu.SemaphoreType.DMA((2,))],
    )
    return pl.pallas_call(
        functools.partial(_combine_kernel, final=final),
        grid_spec=grid_spec,
        out_shape=jax.ShapeDtypeStruct((n, d), F32),
        compiler_params=_params("arbitrary"),
        name="moe_combine",
    )(*args)


def _dispatch_plan(ids):
    n = ids.shape[0]
    a = n * TOP_K
    tm = EXPERT_TILE
    i32 = jnp.int32
    e_flat = ids[:, :TOP_K].reshape(a)
    order = jnp.argsort(e_flat, stable=True).astype(i32)
    experts = jnp.arange(N_EXPERTS, dtype=i32)
    counts = jnp.sum((e_flat[:, None] == experts[None, :]).astype(i32), axis=0)
    zero = jnp.zeros((1,), i32)
    start = jnp.concatenate([zero, jnp.cumsum(counts)]).astype(i32)
    tile_end = jnp.cumsum((counts + tm - 1) // tm)
    tile0 = jnp.concatenate([zero, tile_end]).astype(i32)
    n_tiles = (a + N_EXPERTS * (tm - 1)) // tm
    tile_ids = jnp.arange(n_tiles, dtype=i32)
    tile_e = jnp.minimum(jnp.sum((tile_end[None, :] <= tile_ids[:, None]).astype(i32), axis=1),
                         N_EXPERTS - 1).astype(i32)
    return tile0, tile_e, start, order


def _hier_moe(x, g_ffn, w_r_pad, b_r_pad, w_gate, w_up, w_down, layer, g_out=None):
    ids, gates = _router(x, g_ffn, w_r_pad, b_r_pad)
    tile0, tile_e, start, order = _dispatch_plan(ids)
    y_slots, pos = _expert_ffn(x, g_ffn, w_gate, w_up, w_down, layer, tile0, tile_e, start, order)
    return _combine(x, y_slots, gates, pos, g_out)


def _rope_table(pos):
    inv = jnp.power(ROPE_THETA, -jnp.arange(ROPE_HALF, dtype=F32) * (2.0 / ROPE_DIM))
    ang = pos[:, None] * inv[None, :]
    return jnp.cos(ang), jnp.sin(ang)


def _pad_cols(w, n):
    return jnp.pad(w, ((0, 0), (0, n - w.shape[1])))


def _q_head_weights(w_uq):
    w = w_uq.reshape(Q_LORA, MLA_HEADS, QK_DIM)
    r1 = w[:, :, NOPE_DIM:NOPE_DIM + ROPE_HALF]
    r2 = w[:, :, NOPE_DIM + ROPE_HALF:]
    return jnp.concatenate([w, -r2, r1], axis=-1).reshape(Q_LORA, MLA_HEADS * QK_PAD).astype(BF16)


def kernel(x_prompt, x_sample, cache_mla_ckv, cache_mla_kpe, state_gla, norm_mix, norm_ffn, norm_out, mla_w_in, mla_g_q_lat, mla_g_kv_lat, mla_w_uq, mla_w_uk, mla_w_uv, mla_g_q, mla_g_k, mla_w_o, gla_w_in, gla_w_a, gla_b_a, gla_g_o, gla_w_o, moe_w_router, moe_b_router, moe_w_gate, moe_w_up, moe_w_down):
    batch, seq, d = x_prompt.shape
    nb, t_new, _ = x_sample.shape
    past = cache_mla_ckv.shape[2]
    n_p, n_s = batch * seq, nb * t_new
    n = n_p + n_s
    x = jnp.concatenate([x_prompt.reshape(n_p, d), x_sample.reshape(n_s, d)], axis=0)

    pos_rows = jnp.concatenate([jnp.tile(jnp.arange(seq, dtype=F32), batch),
                                jnp.tile(past + jnp.arange(t_new, dtype=F32), nb)])
    cos_r, sin_r = _rope_table(pos_rows)
    tab = jnp.concatenate([cos_r, cos_r, sin_r, sin_r], axis=1)
    w_in_pad = _pad_cols(mla_w_in[0], 9 * LANES).astype(BF16)
    c_q, c_kv, c_kv_b, k_pe = _mla_in(x, norm_mix[0], w_in_pad, mla_g_q_lat[0], mla_g_kv_lat[0])
    q = _mla_q(c_q, _q_head_weights(mla_w_uq[0]), mla_g_q[0], tab)

    w_uk, w_uv = mla_w_uk[0], mla_w_uv[0]
    w_kv_heads = jnp.concatenate([w_uk, w_uv], axis=-1).reshape(KV_LORA, -1).astype(BF16)
    k_p, v_p = _mla_kv(c_kv_b, k_pe, w_kv_heads, mla_g_k[0], tab, n_p)
    o_p = _flash_prompt(q, k_p, v_p, batch, seq)

    n_keys = past + t_new
    kp = (n_keys + LANES - 1) // LANES * LANES
    c_new = c_kv_b[n_p:].reshape(nb, t_new, KV_LORA)
    c_all = jnp.concatenate([cache_mla_ckv[0].astype(BF16), c_new,
                             jnp.zeros((nb, kp - n_keys, KV_LORA), BF16)], axis=1)
    kpe_new = k_pe[n_p:].reshape(nb, t_new, ROPE_DIM)
    kpe_all = jnp.concatenate([cache_mla_kpe[0], kpe_new,
                               jnp.zeros((nb, kp - n_keys, ROPE_DIM), F32)], axis=1)
    kpet_all = kpe_all.transpose(0, 2, 1)
    cos_k, sin_k = _rope_table(jnp.arange(kp, dtype=F32))
    cost = jnp.concatenate([cos_k, cos_k], axis=1).T
    sint = jnp.concatenate([sin_k, sin_k], axis=1).T
    w_ukt = w_uk.reshape(KV_LORA, -1).T.astype(BF16)
    o_lat = _sample_attn(q, c_all, kpet_all, w_ukt, mla_g_k[0], cost, sint,
                         row_block0=n_p // t_new, n_keys=n_keys, t_new=t_new)
    o_s = _latent_to_values(o_lat, w_uv.reshape(KV_LORA, -1).astype(BF16), t_new)

    x = _mm(o_p, mla_w_o[0].astype(BF16), x2=o_s, res=x, out_dtype=F32, tm=MM_ROW_TILE, tn=512)

    w_r_pad = [_pad_cols(moe_w_router[i], LANES).astype(BF16) for i in range(2)]
    b_r_pad = [_pad_cols(moe_b_router[i].reshape(1, -1), LANES) for i in range(2)]
    x = _hier_moe(x, norm_ffn[0], w_r_pad[0], b_r_pad[0], moe_w_gate, moe_w_up, moe_w_down, 0)

    hk, hv = GLA_HEADS * GLA_DK, GLA_HEADS * GLA_DV
    wg = gla_w_in[0]
    w_gla = jnp.concatenate([wg[:, :2 * hk + hv], wg[:, 2 * hk + hv + GATE_RANK:],
                             _pad_cols(wg[:, 2 * hk + hv:2 * hk + hv + GATE_RANK], LANES)],
                            axis=1).astype(BF16)
    z = _mm(x, w_gla, gain=norm_mix[1], out_dtype=BF16, tm=MM_ROW_TILE, tn=7 * LANES)
    w_a_pad = jnp.pad(gla_w_a[0], ((0, LANES - GATE_RANK), (0, 0))).astype(BF16)
    s0_p = jnp.zeros((batch, GLA_HEADS, GLA_DK, GLA_DV), F32)
    og_p, st_p = _gla_scan(z, w_a_pad, gla_b_a[0], gla_g_o[0], s0_p,
                           row0=0, n_streams=batch, t_len=seq, nsub=GLA_SUBCHUNKS)
    og_s, st_s = _gla_scan(z, w_a_pad, gla_b_a[0], gla_g_o[0], state_gla[0],
                           row0=n_p, n_streams=nb, t_len=t_new, nsub=t_new // CHUNK)
    x = _mm(og_p, gla_w_o[0].astype(BF16), x2=og_s, res=x, out_dtype=F32, tm=MM_ROW_TILE, tn=512)
    y = _hier_moe(x, norm_ffn[1], w_r_pad[1], b_r_pad[1], moe_w_gate, moe_w_up, moe_w_down, 1,
                  g_out=norm_out)

    return (y[:n_p].reshape(batch, seq, d),
            y[n_p:].reshape(nb, t_new, d),
            c_kv[:n_p].reshape(1, batch, seq, KV_LORA),
            k_pe[:n_p].reshape(1, batch, seq, ROPE_DIM),
            st_p[None],
            c_kv[n_p:].reshape(1, nb, t_new, KV_LORA),
            k_pe[n_p:].reshape(1, nb, t_new, ROPE_DIM),
            st_s[None])
```

```python
import functools

import jax
import jax.numpy as jnp
from jax import lax
from jax.experimental import pallas as pl
from jax.experimental.pallas import tpu as pltpu

F32 = jnp.float32
BF16 = jnp.bfloat16

LANES = 128
V7X_VMEM_BYTES = 64 * 1024 * 1024
VMEM_LIMIT = V7X_VMEM_BYTES * 3 // 4

EPS = 1e-6
CHUNK = 64
MLA_HEADS = 16
NOPE_DIM = 128
ROPE_DIM = 64
ROPE_HALF = ROPE_DIM // 2
QK_DIM = NOPE_DIM + ROPE_DIM
QK_PAD = 2 * LANES
V_DIM = 128
Q_LORA = 512
KV_LORA = 512
ROPE_THETA = 10000.0
ATTN_SCALE = QK_DIM ** -0.5
GLA_HEADS = 4
GLA_DK = 256
GLA_DV = 512
GATE_RANK = 16
GATE_TAU = 16.0
N_GROUPS = 8
EXPERTS_PER_GROUP = 8
N_EXPERTS = N_GROUPS * EXPERTS_PER_GROUP
TOP_K = 2
TOP_K_SHIFT = 1
assert 1 << TOP_K_SHIFT == TOP_K

ROW_TILE = 512
MM_ROW_TILE = 1024
FLASH_TQ = 512
FLASH_TK = 512
FLASH_HEADS = 2
SAMPLE_HEAD_GROUP = 4
GLA_SUBCHUNKS = 4
EXPERT_TILE = 128
COMBINE_TILE = 128

NT_DIMS = (((1,), (1,)), ((), ()))
TN_DIMS = (((0,), (0,)), ((), ()))


def _params(*sem):
    return pltpu.CompilerParams(dimension_semantics=sem, vmem_limit_bytes=VMEM_LIMIT)


def _rms(x, g):
    return x * lax.rsqrt(jnp.mean(x * x, axis=-1, keepdims=True) + EPS) * g


def _mm_kernel(*refs, has_norm, has_res, split):
    it = iter(refs)
    x_ref = next(it)
    x2_ref = next(it) if split else None
    g_ref = next(it) if has_norm else None
    w_ref = next(it)
    r_ref = next(it) if has_res else None
    o_ref = next(it)
    if has_norm:
        xn_ref = next(it)

        @pl.when(pl.program_id(1) == 0)
        def _():
            xn_ref[...] = _rms(x_ref[...], g_ref[...]).astype(BF16)

        x_ref = xn_ref

    def emit(src_ref):
        acc = jnp.dot(src_ref[...], w_ref[...], preferred_element_type=F32)
        if has_res:
            acc = acc + r_ref[...]
        o_ref[...] = acc.astype(o_ref.dtype)

    if split:
        pl.when(pl.program_id(0) < split)(lambda: emit(x_ref))
        pl.when(pl.program_id(0) >= split)(lambda: emit(x2_ref))
    else:
        emit(x_ref)


def _mm(x, w, *, x2=None, gain=None, res=None, out_dtype, tm, tn):
    m, k = x.shape
    n = w.shape[1]
    has_norm, has_res = gain is not None, res is not None
    split = 0
    in_specs = [pl.BlockSpec((tm, k), lambda i, j: (i, 0))]
    args = [x]
    if x2 is not None:
        assert not has_norm
        split = m // tm
        m += x2.shape[0]
        in_specs = [pl.BlockSpec((tm, k), lambda i, j: (jnp.minimum(i, split - 1), 0)),
                    pl.BlockSpec((tm, k), lambda i, j: (jnp.maximum(i - split, 0), 0))]
        args.append(x2)
    if has_norm:
        in_specs.append(pl.BlockSpec((1, k), lambda i, j: (0, 0)))
        args.append(gain.reshape(1, k))
    in_specs.append(pl.BlockSpec((k, tn), lambda i, j: (0, j)))
    args.append(w)
    if has_res:
        in_specs.append(pl.BlockSpec((tm, tn), lambda i, j: (i, j)))
        args.append(res)
    return pl.pallas_call(
        functools.partial(_mm_kernel, has_norm=has_norm, has_res=has_res, split=split),
        grid=(m // tm, n // tn),
        in_specs=in_specs,
        out_specs=pl.BlockSpec((tm, tn), lambda i, j: (i, j)),
        out_shape=jax.ShapeDtypeStruct((m, n), out_dtype),
        scratch_shapes=[pltpu.VMEM((tm, k), BF16)] if has_norm else [],
        compiler_params=_params("parallel", "arbitrary"),
        name="mm",
    )(*args)


def _mla_in_kernel(x_ref, g_ref, w_ref, gq_ref, gkv_ref, cq_ref, ckv_ref, ckvb_ref, kpe_ref):
    xn = _rms(x_ref[...], g_ref[...]).astype(BF16)
    z = jnp.dot(xn, w_ref[...], preferred_element_type=F32)
    cq_ref[...] = _rms(z[:, :Q_LORA], gq_ref[...]).astype(BF16)
    ckv = _rms(z[:, Q_LORA:Q_LORA + KV_LORA], gkv_ref[...])
    ckv_ref[...] = ckv
    ckvb_ref[...] = ckv.astype(BF16)
    kpe_ref[...] = z[:, Q_LORA + KV_LORA:Q_LORA + KV_LORA + ROPE_DIM]


def _mla_in(x, g_mix, w_in_pad, g_q_lat, g_kv_lat):
    n, d = x.shape
    tm = ROW_TILE
    wn = w_in_pad.shape[1]
    row = lambda i: (i, 0)
    fixed = lambda i: (0, 0)
    return pl.pallas_call(
        _mla_in_kernel,
        grid=(n // tm,),
        in_specs=[pl.BlockSpec((tm, d), row), pl.BlockSpec((1, d), fixed),
                  pl.BlockSpec((d, wn), fixed), pl.BlockSpec((1, Q_LORA), fixed),
                  pl.BlockSpec((1, KV_LORA), fixed)],
        out_specs=[pl.BlockSpec((tm, Q_LORA), row), pl.BlockSpec((tm, KV_LORA), row),
                   pl.BlockSpec((tm, KV_LORA), row), pl.BlockSpec((tm, ROPE_DIM), row)],
        out_shape=[jax.ShapeDtypeStruct((n, Q_LORA), BF16), jax.ShapeDtypeStruct((n, KV_LORA), F32),
                   jax.ShapeDtypeStruct((n, KV_LORA), BF16), jax.ShapeDtypeStruct((n, ROPE_DIM), F32)],
        compiler_params=_params("parallel"),
        name="mla_in",
    )(x, g_mix.reshape(1, d), w_in_pad, g_q_lat.reshape(1, -1), g_kv_lat.reshape(1, -1))


def _mla_q_kernel(cq_ref, w_ref, gn_ref, g2_ref, tab_ref, q_ref):
    cq = cq_ref[...]
    gtab = g2_ref[...] * tab_ref[...]
    is_rope = lax.broadcasted_iota(jnp.int32, gtab.shape, 1) < ROPE_DIM
    for h in range(MLA_HEADS):
        cols = slice(h * QK_PAD, (h + 1) * QK_PAD)
        t = jnp.dot(cq, w_ref[:, cols], preferred_element_type=F32)
        t1 = t[:, :NOPE_DIM]
        t2 = t[:, NOPE_DIM:]
        ss = (jnp.sum(t1 * t1, axis=-1, keepdims=True)
              + jnp.sum(jnp.where(is_rope, t2 * t2, 0.0), axis=-1, keepdims=True))
        rs = lax.rsqrt(ss * (1.0 / QK_DIM) + EPS) * ATTN_SCALE
        u = t2 * gtab
        u = jnp.where(is_rope, u + pltpu.roll(u, ROPE_DIM, axis=1), 0.0)
        q_ref[:, cols] = jnp.concatenate([t1 * gn_ref[...] * rs, u * rs], axis=1).astype(BF16)


def _mla_q(cq, w_q_heads, g_q, tab):
    n = cq.shape[0]
    tm = ROW_TILE
    g1, g2 = g_q[NOPE_DIM:NOPE_DIM + ROPE_HALF], g_q[NOPE_DIM + ROPE_HALF:]
    g_rope = jnp.concatenate([g1, g2, g2, g1]).reshape(1, LANES)
    row = lambda i: (i, 0)
    fixed = lambda i: (0, 0)
    return pl.pallas_call(
        _mla_q_kernel,
        grid=(n // tm,),
        in_specs=[pl.BlockSpec((tm, Q_LORA), row),
                  pl.BlockSpec((Q_LORA, MLA_HEADS * QK_PAD), fixed),
                  pl.BlockSpec((1, NOPE_DIM), fixed),
                  pl.BlockSpec((1, LANES), fixed),
                  pl.BlockSpec((tm, LANES), row)],
        out_specs=pl.BlockSpec((tm, MLA_HEADS * QK_PAD), row),
        out_shape=jax.ShapeDtypeStruct((n, MLA_HEADS * QK_PAD), BF16),
        compiler_params=_params("parallel"),
        name="mla_q",
    )(cq, w_q_heads, g_q[:NOPE_DIM].reshape(1, NOPE_DIM), g_rope, tab)


def _rotate_half_rows(x):
    return jnp.concatenate([-x[ROPE_HALF:], x[:ROPE_HALF]], axis=0)


def _mla_kv_kernel(c_ref, kpe_ref, w_ref, gn_ref, gr_ref, tab_ref, k_ref, v_ref):
    c = c_ref[...]
    kpe = kpe_ref[...]
    sspe = jnp.sum(kpe * kpe, axis=-1, keepdims=True)
    kg = kpe * gr_ref[...]
    rot = jnp.concatenate([-kg[:, ROPE_HALF:], kg[:, :ROPE_HALF]], axis=1)
    tab = tab_ref[...]
    kr = kg * tab[:, :ROPE_DIM] + rot * tab[:, ROPE_DIM:]
    kr = jnp.concatenate([kr, jnp.zeros_like(kr)], axis=1)
    for h in range(MLA_HEADS):
        t = jnp.dot(c, w_ref[:, h * QK_PAD:(h + 1) * QK_PAD], preferred_element_type=F32)
        kn = t[:, :NOPE_DIM]
        rs = lax.rsqrt((jnp.sum(kn * kn, axis=-1, keepdims=True) + sspe) * (1.0 / QK_DIM) + EPS)
        k_ref[:, h * QK_PAD:(h + 1) * QK_PAD] = jnp.concatenate(
            [kn * gn_ref[...] * rs, kr * rs], axis=1).astype(BF16)
        v_ref[:, h * V_DIM:(h + 1) * V_DIM] = t[:, NOPE_DIM:].astype(BF16)


def _mla_kv(ckv_b, kpe, w_kv_heads, g_k, tab, n_rows):
    tm = ROW_TILE
    row = lambda i: (i, 0)
    fixed = lambda i: (0, 0)
    return pl.pallas_call(
        _mla_kv_kernel,
        grid=(n_rows // tm,),
        in_specs=[pl.BlockSpec((tm, KV_LORA), row),
                  pl.BlockSpec((tm, ROPE_DIM), row),
                  pl.BlockSpec((KV_LORA, MLA_HEADS * (NOPE_DIM + V_DIM)), fixed),
                  pl.BlockSpec((1, NOPE_DIM), fixed),
                  pl.BlockSpec((1, ROPE_DIM), fixed),
                  pl.BlockSpec((tm, LANES), row)],
        out_specs=[pl.BlockSpec((tm, MLA_HEADS * QK_PAD), row),
                   pl.BlockSpec((tm, MLA_HEADS * V_DIM), row)],
        out_shape=[jax.ShapeDtypeStruct((n_rows, MLA_HEADS * QK_PAD), BF16),
                   jax.ShapeDtypeStruct((n_rows, MLA_HEADS * V_DIM), BF16)],
        compiler_params=_params("parallel"),
        name="mla_kv",
    )(ckv_b, kpe, w_kv_heads, g_k[:NOPE_DIM].reshape(1, NOPE_DIM),
      g_k[NOPE_DIM:].reshape(1, ROPE_DIM), tab)


def _flash_kernel(q_ref, k_ref, v_ref, o_ref, *, tq, tk):
    qi = pl.program_id(2)

    def step(j, carries, masked):
        ks = pl.multiple_of(j * tk, tk)
        out = []
        for hh, (m, l, acc) in enumerate(carries):
            qk = slice(hh * QK_PAD, (hh + 1) * QK_PAD)
            vv = slice(hh * V_DIM, (hh + 1) * V_DIM)
            s = lax.dot_general(q_ref[:, qk], k_ref[pl.ds(ks, tk), qk], NT_DIMS,
                                preferred_element_type=F32)
            if masked:
                row = lax.broadcasted_iota(jnp.int32, s.shape, 0) // CHUNK
                col = lax.broadcasted_iota(jnp.int32, s.shape, 1) // CHUNK
                s = jnp.where(col <= row, s, -jnp.inf)
            m_new = jnp.maximum(m, jnp.max(s, axis=-1, keepdims=True))
            p = jnp.exp(s - m_new)
            alpha = jnp.exp(m - m_new)
            l = alpha * l + jnp.sum(p, axis=-1, keepdims=True)
            acc = alpha * acc + jnp.dot(p.astype(BF16), v_ref[pl.ds(ks, tk), vv],
                                        preferred_element_type=F32)
            out.append((m_new, l, acc))
        return tuple(out)

    init = tuple((jnp.full((tq, 1), -jnp.inf, F32), jnp.zeros((tq, 1), F32),
                  jnp.zeros((tq, V_DIM), F32)) for _ in range(FLASH_HEADS))
    carries = lax.fori_loop(0, qi, lambda j, c: step(j, c, False), init)
    carries = step(qi, carries, True)
    for hh, (_, l, acc) in enumerate(carries):
        o_ref[:, hh * V_DIM:(hh + 1) * V_DIM] = (acc / l).astype(BF16)


def _flash_prompt(q, k, v, batch, seq):
    tq, tk, hs = FLASH_TQ, FLASH_TK, FLASH_HEADS
    assert tq == tk
    nq = seq // tq
    return pl.pallas_call(
        functools.partial(_flash_kernel, tq=tq, tk=tk),
        grid=(batch, MLA_HEADS // hs, nq),
        in_specs=[pl.BlockSpec((tq, hs * QK_PAD), lambda b, h, i: (b * nq + i, h)),
                  pl.BlockSpec((seq, hs * QK_PAD), lambda b, h, i: (b, h)),
                  pl.BlockSpec((seq, hs * V_DIM), lambda b, h, i: (b, h))],
        out_specs=pl.BlockSpec((tq, hs * V_DIM), lambda b, h, i: (b * nq + i, h)),
        out_shape=jax.ShapeDtypeStruct((batch * seq, MLA_HEADS * V_DIM), BF16),
        compiler_params=_params("parallel", "parallel", "arbitrary"),
        name="flash_prompt",
    )(q, k, v)


def _sample_attn_kernel(q_ref, cache_ref, cnew_ref, kpet_ref, wukt_ref, gkn_ref, gkrt_ref, cost_ref,
                        sint_ref, olat_ref, c_sc, qabs_ref, qr_ref, *, n_keys, t_new):
    kp = c_sc.shape[0]
    past = n_keys - t_new
    c_sc[:past, :] = cache_ref[0, 0].astype(BF16)
    c_sc[past:n_keys, :] = cnew_ref[...]
    c_sc[n_keys:, :] = jnp.zeros((kp - n_keys, KV_LORA), BF16)
    c = c_sc[...]
    kpet = kpet_ref[0]
    sspe = jnp.sum(kpet * kpet, axis=0, keepdims=True)
    kg = kpet * gkrt_ref[...]
    krt = kg * cost_ref[...] + _rotate_half_rows(kg) * sint_ref[...]
    krt = jnp.concatenate([krt, jnp.zeros_like(krt)], axis=0).astype(BF16)
    gkn = gkn_ref[...]
    for h in range(MLA_HEADS):
        qh = q_ref[:, h * QK_PAD:(h + 1) * QK_PAD]
        qn = (qh[:, :NOPE_DIM].astype(F32) * gkn).astype(BF16)
        qa = jnp.dot(qn, wukt_ref[h * NOPE_DIM:(h + 1) * NOPE_DIM, :], preferred_element_type=F32)
        qabs_ref[h * t_new:(h + 1) * t_new, :] = qa.astype(BF16)
        qr_ref[h * t_new:(h + 1) * t_new, :] = qh[:, NOPE_DIM:]
    valid = lax.broadcasted_iota(jnp.int32, (t_new, kp), 1) < n_keys
    hg = SAMPLE_HEAD_GROUP
    for g in range(MLA_HEADS // hg):
        knt = lax.dot_general(wukt_ref[g * hg * NOPE_DIM:(g + 1) * hg * NOPE_DIM, :], c, NT_DIMS,
                              preferred_element_type=F32)
        ss = jnp.sum((knt * knt).reshape(hg, NOPE_DIM, kp), axis=1) + sspe
        rst = lax.rsqrt(ss * (1.0 / QK_DIM) + EPS)
        rows = slice(g * hg * t_new, (g + 1) * hg * t_new)
        s = (lax.dot_general(qabs_ref[rows, :], c, NT_DIMS, preferred_element_type=F32)
             + jnp.dot(qr_ref[rows, :], krt, preferred_element_type=F32))
        ps = []
        for hh in range(hg):
            sh = s[hh * t_new:(hh + 1) * t_new] * rst[hh:hh + 1]
            sh = jnp.where(valid, sh, -jnp.inf)
            e = jnp.exp(sh - jnp.max(sh, axis=-1, keepdims=True))
            ps.append((e / jnp.sum(e, axis=-1, keepdims=True)).astype(BF16))
        p = jnp.concatenate(ps, axis=0)
        olat_ref[0, rows, :] = jnp.dot(p, c, preferred_element_type=F32).astype(BF16)


def _sample_attn(q, cache_ckv, c_rows, kpet_all, w_ukt, g_k, cost, sint, *, row_block0, n_keys, t_new):
    nb, past = cache_ckv.shape[1], cache_ckv.shape[2]
    kp = kpet_all.shape[2]
    assert past + t_new == n_keys
    fixed = lambda b: (0, 0)
    return pl.pallas_call(
        functools.partial(_sample_attn_kernel, n_keys=n_keys, t_new=t_new),
        grid=(nb,),
        in_specs=[pl.BlockSpec((t_new, MLA_HEADS * QK_PAD), lambda b: (row_block0 + b, 0)),
                  pl.BlockSpec((1, 1, past, KV_LORA), lambda b: (0, b, 0, 0)),
                  pl.BlockSpec((t_new, KV_LORA), lambda b: (row_block0 + b, 0)),
                  pl.BlockSpec((1, ROPE_DIM, kp), lambda b: (b, 0, 0)),
                  pl.BlockSpec((MLA_HEADS * NOPE_DIM, KV_LORA), fixed),
                  pl.BlockSpec((1, NOPE_DIM), fixed),
                  pl.BlockSpec((ROPE_DIM, 1), fixed),
                  pl.BlockSpec((ROPE_DIM, kp), fixed),
                  pl.BlockSpec((ROPE_DIM, kp), fixed)],
        out_specs=pl.BlockSpec((1, MLA_HEADS * t_new, KV_LORA), lambda b: (b, 0, 0)),
        out_shape=jax.ShapeDtypeStruct((nb, MLA_HEADS * t_new, KV_LORA), BF16),
        scratch_shapes=[pltpu.VMEM((kp, KV_LORA), BF16),
                        pltpu.VMEM((MLA_HEADS * t_new, KV_LORA), BF16),
                        pltpu.VMEM((MLA_HEADS * t_new, LANES), BF16)],
        compiler_params=_params("parallel"),
        name="sample_attn",
    )(q, cache_ckv, c_rows, kpet_all, w_ukt, g_k[:NOPE_DIM].reshape(1, NOPE_DIM),
      g_k[NOPE_DIM:].reshape(ROPE_DIM, 1), cost, sint)


def _head_mm_kernel(x_ref, w_ref, o_ref):
    nb, t, r = x_ref.shape
    o_ref[...] = jnp.dot(x_ref[...].reshape(nb * t, r), w_ref[...],
                         preferred_element_type=F32).astype(o_ref.dtype)


def _latent_to_values(o_lat, w_uv2d, t_new):
    nb = o_lat.shape[0]
    return pl.pallas_call(
        _head_mm_kernel,
        grid=(MLA_HEADS,),
        in_specs=[pl.BlockSpec((nb, t_new, KV_LORA), lambda h: (0, h, 0)),
                  pl.BlockSpec((KV_LORA, V_DIM), lambda h: (0, h))],
        out_specs=pl.BlockSpec((nb * t_new, V_DIM), lambda h: (0, h)),
        out_shape=jax.ShapeDtypeStruct((nb * t_new, MLA_HEADS * V_DIM), BF16),
        compiler_params=_params("parallel"),
        name="latent_to_values",
    )(o_lat, w_uv2d)


def _gla_kernel(q_ref, k_ref, v_ref, gate_ref, a_ref, wa_ref, ba_ref, go_ref, s0_ref,
                o_ref, sout_ref, st_ref, *, nsub):
    c = pl.program_id(1)

    @pl.when(c == 0)
    def _():
        for h in range(GLA_HEADS):
            st_ref[h] = s0_ref[0, h].T

    tril = (lax.broadcasted_iota(jnp.int32, (CHUNK, CHUNK), 0)
            >= lax.broadcasted_iota(jnp.int32, (CHUNK, CHUNK), 1))
    tril_b = jnp.where(tril, 1.0, 0.0).astype(BF16)
    for j in range(nsub):
        sl = slice(j * CHUNK, (j + 1) * CHUNK)
        x = jnp.dot(a_ref[sl, :], wa_ref[...], preferred_element_type=F32) + ba_ref[...]
        la = (jnp.minimum(x, 0.0) - jnp.log(1.0 + jnp.exp(-jnp.abs(x)))) * (1.0 / GATE_TAU)
        la_hi = la.astype(BF16)
        la_lo = (la - la_hi.astype(F32)).astype(BF16)
        b_all = (jnp.dot(tril_b, la_hi, preferred_element_type=F32)
                 + jnp.dot(tril_b, la_lo, preferred_element_type=F32))
        for h in range(GLA_HEADS):
            kc = slice(h * GLA_DK, (h + 1) * GLA_DK)
            vc = slice(h * GLA_DV, (h + 1) * GLA_DV)
            b = b_all[:, kc]
            b_last = b[CHUNK - 1:CHUNK, :]
            q = q_ref[sl, kc].astype(F32) * (GLA_DK ** -0.5)
            k = k_ref[sl, kc].astype(F32)
            v = v_ref[sl, vc]
            qs = (q * jnp.exp(b)).astype(BF16)
            ks = (k * jnp.exp(-b)).astype(BF16)
            att = lax.dot_general(qs, ks, NT_DIMS, preferred_element_type=F32)
            att = jnp.where(tril, att, 0.0).astype(BF16)
            st = st_ref[h]
            o = (jnp.dot(att, v, preferred_element_type=F32)
                 + lax.dot_general(qs, st.astype(BF16), NT_DIMS, preferred_element_type=F32))
            kd = (k * jnp.exp(b_last - b)).astype(BF16)
            st_ref[h] = st * jnp.exp(b_last) + lax.dot_general(v, kd, TN_DIMS,
                                                               preferred_element_type=F32)
            gt = gate_ref[sl, vc].astype(F32)
            o_ref[sl, vc] = (_rms(o, go_ref[...]) * (gt / (1.0 + jnp.exp(-gt)))).astype(BF16)

    @pl.when(c == pl.num_programs(1) - 1)
    def _():
        for h in range(GLA_HEADS):
            sout_ref[0, h] = st_ref[h].T


def _gla_scan(z, w_a_pad, b_a, g_o, s0, *, row0, n_streams, t_len, nsub):
    tc = nsub * CHUNK
    nc = t_len // tc
    rb0 = row0 // tc
    rows = lambda b, c: rb0 + b * nc + c
    hk, hv = GLA_HEADS * GLA_DK, GLA_HEADS * GLA_DV
    a_blk = (2 * hk + 2 * hv) // LANES
    fixed = lambda b, c: (0, 0)
    state = lambda b, c: (b, 0, 0, 0)
    return pl.pallas_call(
        functools.partial(_gla_kernel, nsub=nsub),
        grid=(n_streams, nc),
        in_specs=[pl.BlockSpec((tc, hk), lambda b, c: (rows(b, c), 0)),
                  pl.BlockSpec((tc, hk), lambda b, c: (rows(b, c), 1)),
                  pl.BlockSpec((tc, hv), lambda b, c: (rows(b, c), 2 * hk // hv)),
                  pl.BlockSpec((tc, hv), lambda b, c: (rows(b, c), 2 * hk // hv + 1)),
                  pl.BlockSpec((tc, LANES), lambda b, c: (rows(b, c), a_blk)),
                  pl.BlockSpec((LANES, hk), fixed),
                  pl.BlockSpec((1, hk), fixed),
                  pl.BlockSpec((1, GLA_DV), fixed),
                  pl.BlockSpec((1, GLA_HEADS, GLA_DK, GLA_DV), state)],
        out_specs=[pl.BlockSpec((tc, hv), lambda b, c: (b * nc + c, 0)),
                   pl.BlockSpec((1, GLA_HEADS, GLA_DK, GLA_DV), state)],
        out_shape=[jax.ShapeDtypeStruct((n_streams * t_len, hv), BF16),
                   jax.ShapeDtypeStruct((n_streams, GLA_HEADS, GLA_DK, GLA_DV), F32)],
        scratch_shapes=[pltpu.VMEM((GLA_HEADS, GLA_DV, GLA_DK), F32)],
        compiler_params=_params("parallel", "arbitrary"),
        name="gla_scan",
    )(z, z, z, z, z, w_a_pad, b_a.reshape(1, hk), g_o.reshape(1, GLA_DV), s0)


def _router_kernel(x_ref, g_ref, w_ref, b_ref, ids_ref, gates_ref):
    xn = _rms(x_ref[...], g_ref[...]).astype(BF16)
    logits = jnp.dot(xn, w_ref[...], preferred_element_type=F32) + b_ref[...]
    lane = lax.broadcasted_iota(jnp.int32, logits.shape, 1)
    neg = -jnp.inf

    def top(mask):
        vals = jnp.where(mask, logits, neg)
        m = jnp.max(vals, axis=-1, keepdims=True)
        idx = jnp.min(jnp.where(vals == m, lane, LANES), axis=-1, keepdims=True)
        return m, idx

    is_grp = lane < N_GROUPS
    m_g, grp = top(is_grp)
    p_grp = 1.0 / jnp.sum(jnp.where(is_grp, jnp.exp(logits - m_g), 0.0), axis=-1, keepdims=True)
    lo = N_GROUPS + grp * EXPERTS_PER_GROUP
    in_grp = (lane >= lo) & (lane < lo + EXPERTS_PER_GROUP)
    m1, i1 = top(in_grp)
    m2, i2 = top(in_grp & (lane != i1))
    e2 = jnp.exp(m2 - m1)
    g1 = p_grp / (1.0 + e2)
    g2 = p_grp * e2 / (1.0 + e2)
    ids_ref[...] = jnp.where(lane == 0, i1 - N_GROUPS, jnp.where(lane == 1, i2 - N_GROUPS, 0))
    gates_ref[...] = jnp.where(lane == 0, g1, jnp.where(lane == 1, g2, 0.0))


def _router(x, g_ffn, w_r_pad, b_r_pad):
    n, d = x.shape
    tm = ROW_TILE
    row = lambda i: (i, 0)
    fixed = lambda i: (0, 0)
    return pl.pallas_call(
        _router_kernel,
        grid=(n // tm,),
        in_specs=[pl.BlockSpec((tm, d), row), pl.BlockSpec((1, d), fixed),
                  pl.BlockSpec((d, LANES), fixed), pl.BlockSpec((1, LANES), fixed)],
        out_specs=[pl.BlockSpec((tm, LANES), row), pl.BlockSpec((tm, LANES), row)],
        out_shape=[jax.ShapeDtypeStruct((n, LANES), jnp.int32), jax.ShapeDtypeStruct((n, LANES), F32)],
        compiler_params=_params("parallel"),
        name="router",
    )(x, g_ffn.reshape(1, d), w_r_pad, b_r_pad)


def _row_copy(src_hbm, row, dst_vmem, r, sem):
    return pltpu.make_async_copy(src_hbm.at[pl.ds(row, 1)], dst_vmem.at[pl.ds(r, 1)], sem)


def _ffn_kernel(tile0_ref, tile_e_ref, start_ref, order_ref, x_hbm, g_ref, wg_ref, wu_ref, wd_ref,
                y_hbm, pos_ref, xbuf, ybuf, wg_b, wu_b, wd_b, gsem, osem):
    e = pl.program_id(0)
    tm = xbuf.shape[1]
    t_lo, t_hi, n_used = tile0_ref[e], tile0_ref[e + 1], tile0_ref[N_EXPERTS]

    def gather_start(t, slot, unrolled):
        ee = tile_e_ref[t]
        base = start_ref[ee]
        last = start_ref[ee + 1] - base - 1
        off0 = (t - tile0_ref[ee]) * tm

        def one(r):
            a = order_ref[base + jnp.minimum(off0 + r, last)]
            tok = lax.shift_right_logical(a, TOP_K_SHIFT)
            _row_copy(x_hbm, tok, xbuf.at[slot], r, gsem.at[slot]).start(priority=1)
            pos_ref[a] = t * tm + r

        if unrolled:
            for r in range(tm):
                one(r)
        else:
            lax.fori_loop(0, tm, lambda r, c: (one(r), c)[1], 0)

    def gather_wait(slot):
        pltpu.make_async_copy(x_hbm.at[pl.ds(0, tm)], xbuf.at[slot], gsem.at[slot]).wait()

    def out_copy(t):
        return pltpu.make_async_copy(ybuf.at[t & 1], y_hbm.at[pl.ds(t * tm, tm)], osem.at[t & 1])

    @pl.when((e == 0) & (n_used > 0))
    def _():
        gather_start(0, 0, False)

    @pl.when(t_hi > t_lo)
    def _():
        wg_b[...] = wg_ref[0, 0].astype(BF16)
        wu_b[...] = wu_ref[0, 0].astype(BF16)
        wd_b[...] = wd_ref[0, 0].astype(BF16)

        def tile(t, _):
            gather_wait(t & 1)

            @pl.when(t >= 2)
            def _():
                out_copy(t - 2).wait()

            xn = _rms(xbuf[t & 1], g_ref[...]).astype(BF16)
            gather_start(jnp.minimum(t + 1, n_used - 1), (t + 1) & 1, True)
            hg = jnp.dot(xn, wg_b[...], preferred_element_type=F32)
            hu = jnp.dot(xn, wu_b[...], preferred_element_type=F32)
            hid = (hg / (1.0 + jnp.exp(-hg)) * hu).astype(BF16)
            ybuf[t & 1] = jnp.dot(hid, wd_b[...], preferred_element_type=F32)
            out_copy(t).start()
            return 0

        lax.fori_loop(t_lo, t_hi, tile, 0)

    @pl.when(e == pl.num_programs(0) - 1)
    def _():
        @pl.when(n_used >= 1)
        def _():
            gather_wait(n_used & 1)

        for back in (1, 2):
            @pl.when(n_used >= back)
            def _():
                out_copy(n_used - back).wait()

        n_tiles = y_hbm.shape[0] // tm
        ybuf[0] = jnp.zeros(ybuf.shape[1:], F32)

        def zero_copy(t):
            return pltpu.make_async_copy(ybuf.at[0], y_hbm.at[pl.ds(t * tm, tm)], osem.at[0])

        lax.fori_loop(n_used, n_tiles, lambda t, c: (zero_copy(t).start(), c)[1], 0)
        lax.fori_loop(n_used, n_tiles, lambda t, c: (zero_copy(t).wait(), c)[1], 0)


def _expert_ffn(x, g_ffn, w_gate, w_up, w_down, layer, tile0, tile_e, start, order):
    n, d = x.shape
    tm = EXPERT_TILE
    de = w_gate.shape[3]
    fixed = lambda e, *_: (0, 0)
    w_map = lambda e, *_: (layer, e, 0, 0)
    grid_spec = pltpu.PrefetchScalarGridSpec(
        num_scalar_prefetch=4,
        grid=(N_EXPERTS,),
        in_specs=[pl.BlockSpec(memory_space=pl.ANY),
                  pl.BlockSpec((1, d), fixed),
                  pl.BlockSpec((1, 1, d, de), w_map),
                  pl.BlockSpec((1, 1, d, de), w_map),
                  pl.BlockSpec((1, 1, de, d), w_map)],
        out_specs=[pl.BlockSpec(memory_space=pl.ANY), pl.BlockSpec(memory_space=pltpu.SMEM)],
        scratch_shapes=[pltpu.VMEM((2, tm, d), F32), pltpu.VMEM((2, tm, d), F32),
                        pltpu.VMEM((d, de), BF16), pltpu.VMEM((d, de), BF16), pltpu.VMEM((de, d), BF16),
                        pltpu.SemaphoreType.DMA((2,)), pltpu.SemaphoreType.DMA((2,))],
    )
    return pl.pallas_call(
        _ffn_kernel,
        grid_spec=grid_spec,
        out_shape=[jax.ShapeDtypeStruct((tile_e.shape[0] * tm, d), F32),
                   jax.ShapeDtypeStruct(order.shape, jnp.int32)],
        compiler_params=_params("arbitrary"),
        name="expert_ffn",
    )(tile0, tile_e, start, order, x, g_ffn.reshape(1, d), w_gate, w_up, w_down)


def _combine_kernel(pos_ref, y_hbm, gates_ref, x_ref, *rest, split):
    if split:
        go_ref, o_ref, o2_ref, buf0, buf1, sem = rest
    else:
        o_ref, buf0, buf1, sem = rest
    i = pl.program_id(0)
    tc = buf0.shape[1]

    last = pl.num_programs(0) - 1

    def gather_start(t, slot, unrolled):
        def one(r):
            a = TOP_K * (t * tc + r)
            _row_copy(y_hbm, pos_ref[a], buf0.at[slot], r, sem.at[slot]).start(priority=0)
            _row_copy(y_hbm, pos_ref[a + 1], buf1.at[slot], r, sem.at[slot]).start(priority=1)

        if unrolled:
            for r in range(tc):
                one(r)
        else:
            lax.fori_loop(0, tc, lambda r, c: (one(r), c)[1], 0)

    def gather_wait(slot):
        for buf in (buf0, buf1):
            pltpu.make_async_copy(y_hbm.at[pl.ds(0, tc)], buf.at[slot], sem.at[slot]).wait()

    @pl.when(i == 0)
    def _():
        gather_start(0, 0, False)

    gather_wait(i & 1)
    gates = gates_ref[...]
    out = x_ref[...] + buf0[i & 1] * gates[:, 0:1] + buf1[i & 1] * gates[:, 1:2]
    gather_start(jnp.minimum(i + 1, last), (i + 1) & 1, True)
    if split:
        out = _rms(out, go_ref[...])

        @pl.when(i < split)
        def _():
            o_ref[...] = out

        @pl.when(i >= split)
        def _():
            o2_ref[...] = out
    else:
        o_ref[...] = out

    @pl.when(i == last)
    def _():
        gather_wait((i + 1) & 1)


def _combine(x, y_slots, gates, pos, g_out=None, n_first=None):
    n, d = x.shape
    tc = COMBINE_TILE
    split = 0 if g_out is None else n_first // tc
    in_specs = [pl.BlockSpec(memory_space=pl.ANY),
                pl.BlockSpec((tc, LANES), lambda i, p: (i, 0)),
                pl.BlockSpec((tc, d), lambda i, p: (i, 0))]
    args = [pos, y_slots, gates, x]
    out_specs = pl.BlockSpec((tc, d), lambda i, p: (i, 0))
    out_shape = jax.ShapeDtypeStruct((n, d), F32)
    if split:
        in_specs.append(pl.BlockSpec((1, d), lambda i, p: (0, 0)))
        args.append(g_out.reshape(1, d))
        out_specs = [pl.BlockSpec((tc, d), lambda i, p: (jnp.minimum(i, split - 1), 0)),
                     pl.BlockSpec((tc, d), lambda i, p: (jnp.maximum(i - split, 0), 0))]
        out_shape = [jax.ShapeDtypeStruct((n_first, d), F32), jax.ShapeDtypeStruct((n - n_first, d), F32)]
    grid_spec = pltpu.PrefetchScalarGridSpec(
        num_scalar_prefetch=1,
        grid=(n // tc,),
        in_specs=in_specs,
        out_specs=out_specs,
        scratch_shapes=[pltpu.VMEM((2, tc, d), F32), pltpu.VMEM((2, tc, d), F32),
                        pltpu.SemaphoreType.DMA((2,))],
    )
    return pl.pallas_call(
        functools.partial(_combine_kernel, split=split),
        grid_spec=grid_spec,
        out_shape=out_shape,
        compiler_params=_params("arbitrary"),
        name="moe_combine",
    )(*args)


def _dispatch_plan(ids):
    n = ids.shape[0]
    a = n * TOP_K
    tm = EXPERT_TILE
    i32 = jnp.int32
    e_flat = ids[:, :TOP_K].reshape(a)
    order = jnp.argsort(e_flat, stable=True).astype(i32)
    experts = jnp.arange(N_EXPERTS, dtype=i32)
    counts = jnp.sum((e_flat[:, None] == experts[None, :]).astype(i32), axis=0)
    zero = jnp.zeros((1,), i32)
    start = jnp.concatenate([zero, jnp.cumsum(counts)]).astype(i32)
    tile_end = jnp.cumsum((counts + tm - 1) // tm)
    tile0 = jnp.concatenate([zero, tile_end]).astype(i32)
    n_tiles = (a + N_EXPERTS * (tm - 1)) // tm
    tile_ids = jnp.arange(n_tiles, dtype=i32)
    tile_e = jnp.minimum(jnp.sum((tile_end[None, :] <= tile_ids[:, None]).astype(i32), axis=1),
                         N_EXPERTS - 1).astype(i32)
    return tile0, tile_e, start, order


def _hier_moe(x, g_ffn, w_r_pad, b_r_pad, w_gate, w_up, w_down, layer, g_out=None, n_first=None):
    ids, gates = _router(x, g_ffn, w_r_pad, b_r_pad)
    tile0, tile_e, start, order = _dispatch_plan(ids)
    y_slots, pos = _expert_ffn(x, g_ffn, w_gate, w_up, w_down, layer, tile0, tile_e, start, order)
    return _combine(x, y_slots, gates, pos, g_out, n_first)


def _rope_table(pos):
    inv = jnp.power(ROPE_THETA, -jnp.arange(ROPE_HALF, dtype=F32) * (2.0 / ROPE_DIM))
    ang = pos[:, None] * inv[None, :]
    return jnp.cos(ang), jnp.sin(ang)


def _pad_cols(w, n):
    return jnp.pad(w, ((0, 0), (0, n - w.shape[1])))


def _q_head_weights(w_uq):
    w = w_uq.reshape(Q_LORA, MLA_HEADS, QK_DIM)
    r1 = w[:, :, NOPE_DIM:NOPE_DIM + ROPE_HALF]
    r2 = w[:, :, NOPE_DIM + ROPE_HALF:]
    return jnp.concatenate([w, -r2, r1], axis=-1).reshape(Q_LORA, MLA_HEADS * QK_PAD).astype(BF16)


def kernel(x_prompt, x_sample, cache_mla_ckv, cache_mla_kpe, state_gla, norm_mix, norm_ffn, norm_out, mla_w_in, mla_g_q_lat, mla_g_kv_lat, mla_w_uq, mla_w_uk, mla_w_uv, mla_g_q, mla_g_k, mla_w_o, gla_w_in, gla_w_a, gla_b_a, gla_g_o, gla_w_o, moe_w_router, moe_b_router, moe_w_gate, moe_w_up, moe_w_down):
    batch, seq, d = x_prompt.shape
    nb, t_new, _ = x_sample.shape
    past = cache_mla_ckv.shape[2]
    n_p, n_s = batch * seq, nb * t_new
    n = n_p + n_s
    x = jnp.concatenate([x_prompt.reshape(n_p, d), x_sample.reshape(n_s, d)], axis=0)

    pos_rows = jnp.concatenate([jnp.tile(jnp.arange(seq, dtype=F32), batch),
                                jnp.tile(past + jnp.arange(t_new, dtype=F32), nb)])
    cos_r, sin_r = _rope_table(pos_rows)
    tab = jnp.concatenate([cos_r, cos_r, sin_r, sin_r], axis=1)
    w_in_pad = _pad_cols(mla_w_in[0], 9 * LANES).astype(BF16)
    c_q, c_kv, c_kv_b, k_pe = _mla_in(x, norm_mix[0], w_in_pad, mla_g_q_lat[0], mla_g_kv_lat[0])
    q = _mla_q(c_q, _q_head_weights(mla_w_uq[0]), mla_g_q[0], tab)

    w_uk, w_uv = mla_w_uk[0], mla_w_uv[0]
    w_kv_heads = jnp.concatenate([w_uk, w_uv], axis=-1).reshape(KV_LORA, -1).astype(BF16)
    k_p, v_p = _mla_kv(c_kv_b, k_pe, w_kv_heads, mla_g_k[0], tab, n_p)
    o_p = _flash_prompt(q, k_p, v_p, batch, seq)

    n_keys = past + t_new
    kp = (n_keys + LANES - 1) // LANES * LANES
    kpe_new = k_pe[n_p:].reshape(nb, t_new, ROPE_DIM)
    kpe_all = jnp.concatenate([cache_mla_kpe[0], kpe_new,
                               jnp.zeros((nb, kp - n_keys, ROPE_DIM), F32)], axis=1)
    kpet_all = kpe_all.transpose(0, 2, 1)
    cos_k, sin_k = _rope_table(jnp.arange(kp, dtype=F32))
    cost = jnp.concatenate([cos_k, cos_k], axis=1).T
    sint = jnp.concatenate([sin_k, sin_k], axis=1).T
    w_ukt = w_uk.reshape(KV_LORA, -1).T.astype(BF16)
    o_lat = _sample_attn(q, cache_mla_ckv, c_kv_b, kpet_all, w_ukt, mla_g_k[0], cost, sint,
                         row_block0=n_p // t_new, n_keys=n_keys, t_new=t_new)
    o_s = _latent_to_values(o_lat, w_uv.reshape(KV_LORA, -1).astype(BF16), t_new)

    x = _mm(o_p, mla_w_o[0].astype(BF16), x2=o_s, res=x, out_dtype=F32, tm=MM_ROW_TILE, tn=512)

    w_r_pad = [_pad_cols(moe_w_router[i], LANES).astype(BF16) for i in range(2)]
    b_r_pad = [_pad_cols(moe_b_router[i].reshape(1, -1), LANES) for i in range(2)]
    x = _hier_moe(x, norm_ffn[0], w_r_pad[0], b_r_pad[0], moe_w_gate, moe_w_up, moe_w_down, 0)

    hk, hv = GLA_HEADS * GLA_DK, GLA_HEADS * GLA_DV
    wg = gla_w_in[0]
    w_gla = jnp.concatenate([wg[:, :2 * hk + hv], wg[:, 2 * hk + hv + GATE_RANK:],
                             _pad_cols(wg[:, 2 * hk + hv:2 * hk + hv + GATE_RANK], LANES)],
                            axis=1).astype(BF16)
    z = _mm(x, w_gla, gain=norm_mix[1], out_dtype=BF16, tm=MM_ROW_TILE, tn=7 * LANES)
    w_a_pad = jnp.pad(gla_w_a[0], ((0, LANES - GATE_RANK), (0, 0))).astype(BF16)
    s0_p = jnp.zeros((batch, GLA_HEADS, GLA_DK, GLA_DV), F32)
    og_p, st_p = _gla_scan(z, w_a_pad, gla_b_a[0], gla_g_o[0], s0_p,
                           row0=0, n_streams=batch, t_len=seq, nsub=GLA_SUBCHUNKS)
    og_s, st_s = _gla_scan(z, w_a_pad, gla_b_a[0], gla_g_o[0], state_gla[0],
                           row0=n_p, n_streams=nb, t_len=t_new, nsub=t_new // CHUNK)
    x = _mm(og_p, gla_w_o[0].astype(BF16), x2=og_s, res=x, out_dtype=F32, tm=MM_ROW_TILE, tn=512)
    y_p, y_s = _hier_moe(x, norm_ffn[1], w_r_pad[1], b_r_pad[1], moe_w_gate, moe_w_up, moe_w_down, 1,
                         g_out=norm_out, n_first=n_p)

    return (y_p.reshape(batch, seq, d),
            y_s.reshape(nb, t_new, d),
            c_kv[:n_p].reshape(1, batch, seq, KV_LORA),
            k_pe[:n_p].reshape(1, batch, seq, ROPE_DIM),
            st_p[None],
            c_kv[n_p:].reshape(1, nb, t_new, KV_LORA),
            k_pe[n_p:].reshape(1, nb, t_new, ROPE_DIM),
            st_s[None])
```

```python
import functools

import jax
import jax.numpy as jnp
from jax import lax
from jax.experimental import pallas as pl
from jax.experimental.pallas import tpu as pltpu

F32 = jnp.float32
BF16 = jnp.bfloat16

LANES = 128
V7X_VMEM_BYTES = 64 * 1024 * 1024
VMEM_LIMIT = V7X_VMEM_BYTES * 3 // 4

EPS = 1e-6
CHUNK = 64
MLA_HEADS = 16
NOPE_DIM = 128
ROPE_DIM = 64
ROPE_HALF = ROPE_DIM // 2
QK_DIM = NOPE_DIM + ROPE_DIM
QK_PAD = 2 * LANES
V_DIM = 128
Q_LORA = 512
KV_LORA = 512
ROPE_THETA = 10000.0
ATTN_SCALE = QK_DIM ** -0.5
GLA_HEADS = 4
GLA_DK = 256
GLA_DV = 512
GATE_RANK = 16
GATE_TAU = 16.0
N_GROUPS = 8
EXPERTS_PER_GROUP = 8
N_EXPERTS = N_GROUPS * EXPERTS_PER_GROUP
TOP_K = 2
TOP_K_SHIFT = 1
assert 1 << TOP_K_SHIFT == TOP_K

ROW_TILE = 512
MM_ROW_TILE = 1024
FLASH_TQ = 512
FLASH_TK = 512
FLASH_HEADS = 2
SAMPLE_HEAD_GROUP = 4
GLA_SUBCHUNKS = 4
EXPERT_TILE = 128
COMBINE_TILE = 128
GATHER_BUFS = 3

NT_DIMS = (((1,), (1,)), ((), ()))
TN_DIMS = (((0,), (0,)), ((), ()))


def _params(*sem):
    return pltpu.CompilerParams(dimension_semantics=sem, vmem_limit_bytes=VMEM_LIMIT)


def _rms(x, g):
    return x * lax.rsqrt(jnp.mean(x * x, axis=-1, keepdims=True) + EPS) * g


def _mm_kernel(*refs, has_norm, has_res, split):
    it = iter(refs)
    x_ref = next(it)
    x2_ref = next(it) if split else None
    g_ref = next(it) if has_norm else None
    w_ref = next(it)
    r_ref = next(it) if has_res else None
    o_ref = next(it)
    if has_norm:
        xn_ref = next(it)

        @pl.when(pl.program_id(1) == 0)
        def _():
            xn_ref[...] = _rms(x_ref[...], g_ref[...]).astype(BF16)

        x_ref = xn_ref

    def emit(src_ref):
        acc = jnp.dot(src_ref[...], w_ref[...], preferred_element_type=F32)
        if has_res:
            acc = acc + r_ref[...]
        o_ref[...] = acc.astype(o_ref.dtype)

    if split:
        pl.when(pl.program_id(0) < split)(lambda: emit(x_ref))
        pl.when(pl.program_id(0) >= split)(lambda: emit(x2_ref))
    else:
        emit(x_ref)


def _mm(x, w, *, x2=None, gain=None, res=None, out_dtype, tm, tn):
    m, k = x.shape
    n = w.shape[1]
    has_norm, has_res = gain is not None, res is not None
    split = 0
    in_specs = [pl.BlockSpec((tm, k), lambda i, j: (i, 0))]
    args = [x]
    if x2 is not None:
        assert not has_norm
        split = m // tm
        m += x2.shape[0]
        in_specs = [pl.BlockSpec((tm, k), lambda i, j: (jnp.minimum(i, split - 1), 0)),
                    pl.BlockSpec((tm, k), lambda i, j: (jnp.maximum(i - split, 0), 0))]
        args.append(x2)
    if has_norm:
        in_specs.append(pl.BlockSpec((1, k), lambda i, j: (0, 0)))
        args.append(gain.reshape(1, k))
    in_specs.append(pl.BlockSpec((k, tn), lambda i, j: (0, j)))
    args.append(w)
    if has_res:
        in_specs.append(pl.BlockSpec((tm, tn), lambda i, j: (i, j)))
        args.append(res)
    return pl.pallas_call(
        functools.partial(_mm_kernel, has_norm=has_norm, has_res=has_res, split=split),
        grid=(m // tm, n // tn),
        in_specs=in_specs,
        out_specs=pl.BlockSpec((tm, tn), lambda i, j: (i, j)),
        out_shape=jax.ShapeDtypeStruct((m, n), out_dtype),
        scratch_shapes=[pltpu.VMEM((tm, k), BF16)] if has_norm else [],
        compiler_params=_params("parallel", "arbitrary"),
        name="mm",
    )(*args)


def _mla_in_kernel(x_ref, g_ref, w_ref, gq_ref, gkv_ref, cq_ref, ckv_ref, ckvb_ref, kpe_ref):
    xn = _rms(x_ref[...], g_ref[...]).astype(BF16)
    z = jnp.dot(xn, w_ref[...], preferred_element_type=F32)
    cq_ref[...] = _rms(z[:, :Q_LORA], gq_ref[...]).astype(BF16)
    ckv = _rms(z[:, Q_LORA:Q_LORA + KV_LORA], gkv_ref[...])
    ckv_ref[...] = ckv
    ckvb_ref[...] = ckv.astype(BF16)
    kpe_ref[...] = z[:, Q_LORA + KV_LORA:Q_LORA + KV_LORA + ROPE_DIM]


def _mla_in(x, g_mix, w_in_pad, g_q_lat, g_kv_lat):
    n, d = x.shape
    tm = ROW_TILE
    wn = w_in_pad.shape[1]
    row = lambda i: (i, 0)
    fixed = lambda i: (0, 0)
    return pl.pallas_call(
        _mla_in_kernel,
        grid=(n // tm,),
        in_specs=[pl.BlockSpec((tm, d), row), pl.BlockSpec((1, d), fixed),
                  pl.BlockSpec((d, wn), fixed), pl.BlockSpec((1, Q_LORA), fixed),
                  pl.BlockSpec((1, KV_LORA), fixed)],
        out_specs=[pl.BlockSpec((tm, Q_LORA), row), pl.BlockSpec((tm, KV_LORA), row),
                   pl.BlockSpec((tm, KV_LORA), row), pl.BlockSpec((tm, ROPE_DIM), row)],
        out_shape=[jax.ShapeDtypeStruct((n, Q_LORA), BF16), jax.ShapeDtypeStruct((n, KV_LORA), F32),
                   jax.ShapeDtypeStruct((n, KV_LORA), BF16), jax.ShapeDtypeStruct((n, ROPE_DIM), F32)],
        compiler_params=_params("parallel"),
        name="mla_in",
    )(x, g_mix.reshape(1, d), w_in_pad, g_q_lat.reshape(1, -1), g_kv_lat.reshape(1, -1))


def _mla_q_kernel(cq_ref, w_ref, gn_ref, g2_ref, tab_ref, q_ref):
    cq = cq_ref[...]
    gtab = g2_ref[...] * tab_ref[...]
    is_rope = lax.broadcasted_iota(jnp.int32, gtab.shape, 1) < ROPE_DIM
    for h in range(MLA_HEADS):
        cols = slice(h * QK_PAD, (h + 1) * QK_PAD)
        t = jnp.dot(cq, w_ref[:, cols], preferred_element_type=F32)
        t1 = t[:, :NOPE_DIM]
        t2 = t[:, NOPE_DIM:]
        ss = (jnp.sum(t1 * t1, axis=-1, keepdims=True)
              + jnp.sum(jnp.where(is_rope, t2 * t2, 0.0), axis=-1, keepdims=True))
        rs = lax.rsqrt(ss * (1.0 / QK_DIM) + EPS) * ATTN_SCALE
        u = t2 * gtab
        u = jnp.where(is_rope, u + pltpu.roll(u, ROPE_DIM, axis=1), 0.0)
        q_ref[:, cols] = jnp.concatenate([t1 * gn_ref[...] * rs, u * rs], axis=1).astype(BF16)


def _mla_q(cq, w_q_heads, g_q, tab):
    n = cq.shape[0]
    tm = ROW_TILE
    g1, g2 = g_q[NOPE_DIM:NOPE_DIM + ROPE_HALF], g_q[NOPE_DIM + ROPE_HALF:]
    g_rope = jnp.concatenate([g1, g2, g2, g1]).reshape(1, LANES)
    row = lambda i: (i, 0)
    fixed = lambda i: (0, 0)
    return pl.pallas_call(
        _mla_q_kernel,
        grid=(n // tm,),
        in_specs=[pl.BlockSpec((tm, Q_LORA), row),
                  pl.BlockSpec((Q_LORA, MLA_HEADS * QK_PAD), fixed),
                  pl.BlockSpec((1, NOPE_DIM), fixed),
                  pl.BlockSpec((1, LANES), fixed),
                  pl.BlockSpec((tm, LANES), row)],
        out_specs=pl.BlockSpec((tm, MLA_HEADS * QK_PAD), row),
        out_shape=jax.ShapeDtypeStruct((n, MLA_HEADS * QK_PAD), BF16),
        compiler_params=_params("parallel"),
        name="mla_q",
    )(cq, w_q_heads, g_q[:NOPE_DIM].reshape(1, NOPE_DIM), g_rope, tab)


def _rotate_half_rows(x):
    return jnp.concatenate([-x[ROPE_HALF:], x[:ROPE_HALF]], axis=0)


def _mla_kv_kernel(c_ref, kpe_ref, w_ref, gn_ref, gr_ref, tab_ref, k_ref, v_ref):
    c = c_ref[...]
    kpe = kpe_ref[...]
    sspe = jnp.sum(kpe * kpe, axis=-1, keepdims=True)
    kg = kpe * gr_ref[...]
    rot = jnp.concatenate([-kg[:, ROPE_HALF:], kg[:, :ROPE_HALF]], axis=1)
    tab = tab_ref[...]
    kr = kg * tab[:, :ROPE_DIM] + rot * tab[:, ROPE_DIM:]
    kr = jnp.concatenate([kr, jnp.zeros_like(kr)], axis=1)
    for h in range(MLA_HEADS):
        t = jnp.dot(c, w_ref[:, h * QK_PAD:(h + 1) * QK_PAD], preferred_element_type=F32)
        kn = t[:, :NOPE_DIM]
        rs = lax.rsqrt((jnp.sum(kn * kn, axis=-1, keepdims=True) + sspe) * (1.0 / QK_DIM) + EPS)
        k_ref[:, h * QK_PAD:(h + 1) * QK_PAD] = jnp.concatenate(
            [kn * gn_ref[...] * rs, kr * rs], axis=1).astype(BF16)
        v_ref[:, h * V_DIM:(h + 1) * V_DIM] = t[:, NOPE_DIM:].astype(BF16)


def _mla_kv(ckv_b, kpe, w_kv_heads, g_k, tab, n_rows):
    tm = ROW_TILE
    row = lambda i: (i, 0)
    fixed = lambda i: (0, 0)
    return pl.pallas_call(
        _mla_kv_kernel,
        grid=(n_rows // tm,),
        in_specs=[pl.BlockSpec((tm, KV_LORA), row),
                  pl.BlockSpec((tm, ROPE_DIM), row),
                  pl.BlockSpec((KV_LORA, MLA_HEADS * (NOPE_DIM + V_DIM)), fixed),
                  pl.BlockSpec((1, NOPE_DIM), fixed),
                  pl.BlockSpec((1, ROPE_DIM), fixed),
                  pl.BlockSpec((tm, LANES), row)],
        out_specs=[pl.BlockSpec((tm, MLA_HEADS * QK_PAD), row),
                   pl.BlockSpec((tm, MLA_HEADS * V_DIM), row)],
        out_shape=[jax.ShapeDtypeStruct((n_rows, MLA_HEADS * QK_PAD), BF16),
                   jax.ShapeDtypeStruct((n_rows, MLA_HEADS * V_DIM), BF16)],
        compiler_params=_params("parallel"),
        name="mla_kv",
    )(ckv_b, kpe, w_kv_heads, g_k[:NOPE_DIM].reshape(1, NOPE_DIM),
      g_k[NOPE_DIM:].reshape(1, ROPE_DIM), tab)


def _flash_kernel(q_ref, k_ref, v_ref, o_ref, *, tq, tk):
    qi = pl.program_id(2)

    def step(j, carries, masked):
        ks = pl.multiple_of(j * tk, tk)
        out = []
        for hh, (m, l, acc) in enumerate(carries):
            qk = slice(hh * QK_PAD, (hh + 1) * QK_PAD)
            vv = slice(hh * V_DIM, (hh + 1) * V_DIM)
            s = lax.dot_general(q_ref[:, qk], k_ref[pl.ds(ks, tk), qk], NT_DIMS,
                                preferred_element_type=F32)
            if masked:
                row = lax.broadcasted_iota(jnp.int32, s.shape, 0) // CHUNK
                col = lax.broadcasted_iota(jnp.int32, s.shape, 1) // CHUNK
                s = jnp.where(col <= row, s, -jnp.inf)
            m_new = jnp.maximum(m, jnp.max(s, axis=-1, keepdims=True))
            p = jnp.exp(s - m_new)
            alpha = jnp.exp(m - m_new)
            l = alpha * l + jnp.sum(p, axis=-1, keepdims=True)
            acc = alpha * acc + jnp.dot(p.astype(BF16), v_ref[pl.ds(ks, tk), vv],
                                        preferred_element_type=F32)
            out.append((m_new, l, acc))
        return tuple(out)

    init = tuple((jnp.full((tq, 1), -jnp.inf, F32), jnp.zeros((tq, 1), F32),
                  jnp.zeros((tq, V_DIM), F32)) for _ in range(FLASH_HEADS))
    carries = lax.fori_loop(0, qi, lambda j, c: step(j, c, False), init)
    carries = step(qi, carries, True)
    for hh, (_, l, acc) in enumerate(carries):
        o_ref[:, hh * V_DIM:(hh + 1) * V_DIM] = (acc / l).astype(BF16)


def _flash_prompt(q, k, v, batch, seq):
    tq, tk, hs = FLASH_TQ, FLASH_TK, FLASH_HEADS
    assert tq == tk
    nq = seq // tq
    return pl.pallas_call(
        functools.partial(_flash_kernel, tq=tq, tk=tk),
        grid=(batch, MLA_HEADS // hs, nq),
        in_specs=[pl.BlockSpec((tq, hs * QK_PAD), lambda b, h, i: (b * nq + i, h)),
                  pl.BlockSpec((seq, hs * QK_PAD), lambda b, h, i: (b, h)),
                  pl.BlockSpec((seq, hs * V_DIM), lambda b, h, i: (b, h))],
        out_specs=pl.BlockSpec((tq, hs * V_DIM), lambda b, h, i: (b * nq + i, h)),
        out_shape=jax.ShapeDtypeStruct((batch * seq, MLA_HEADS * V_DIM), BF16),
        compiler_params=_params("parallel", "parallel", "arbitrary"),
        name="flash_prompt",
    )(q, k, v)


def _sample_attn_kernel(q_ref, cache_ref, cnew_ref, kpet_ref, wukt_ref, gkn_ref, gkrt_ref, cost_ref,
                        sint_ref, olat_ref, c_sc, qabs_ref, qr_ref, *, n_keys, t_new):
    kp = c_sc.shape[0]
    past = n_keys - t_new
    c_sc[:past, :] = cache_ref[0, 0].astype(BF16)
    c_sc[past:n_keys, :] = cnew_ref[...]
    c_sc[n_keys:, :] = jnp.zeros((kp - n_keys, KV_LORA), BF16)
    c = c_sc[...]
    kpet = kpet_ref[0]
    sspe = jnp.sum(kpet * kpet, axis=0, keepdims=True)
    kg = kpet * gkrt_ref[...]
    krt = kg * cost_ref[...] + _rotate_half_rows(kg) * sint_ref[...]
    krt = jnp.concatenate([krt, jnp.zeros_like(krt)], axis=0).astype(BF16)
    gkn = gkn_ref[...]
    for h in range(MLA_HEADS):
        qh = q_ref[:, h * QK_PAD:(h + 1) * QK_PAD]
        qn = (qh[:, :NOPE_DIM].astype(F32) * gkn).astype(BF16)
        qa = jnp.dot(qn, wukt_ref[h * NOPE_DIM:(h + 1) * NOPE_DIM, :], preferred_element_type=F32)
        qabs_ref[h * t_new:(h + 1) * t_new, :] = qa.astype(BF16)
        qr_ref[h * t_new:(h + 1) * t_new, :] = qh[:, NOPE_DIM:]
    valid = lax.broadcasted_iota(jnp.int32, (t_new, kp), 1) < n_keys
    hg = SAMPLE_HEAD_GROUP
    for g in range(MLA_HEADS // hg):
        knt = lax.dot_general(wukt_ref[g * hg * NOPE_DIM:(g + 1) * hg * NOPE_DIM, :], c, NT_DIMS,
                              preferred_element_type=F32)
        ss = jnp.sum((knt * knt).reshape(hg, NOPE_DIM, kp), axis=1) + sspe
        rst = lax.rsqrt(ss * (1.0 / QK_DIM) + EPS)
        rows = slice(g * hg * t_new, (g + 1) * hg * t_new)
        s = (lax.dot_general(qabs_ref[rows, :], c, NT_DIMS, preferred_element_type=F32)
             + jnp.dot(qr_ref[rows, :], krt, preferred_element_type=F32))
        ps = []
        for hh in range(hg):
            sh = s[hh * t_new:(hh + 1) * t_new] * rst[hh:hh + 1]
            sh = jnp.where(valid, sh, -jnp.inf)
            e = jnp.exp(sh - jnp.max(sh, axis=-1, keepdims=True))
            ps.append((e / jnp.sum(e, axis=-1, keepdims=True)).astype(BF16))
        p = jnp.concatenate(ps, axis=0)
        olat_ref[0, rows, :] = jnp.dot(p, c, preferred_element_type=F32).astype(BF16)


def _sample_attn(q, cache_ckv, c_rows, kpet_all, w_ukt, g_k, cost, sint, *, row_block0, n_keys, t_new):
    nb, past = cache_ckv.shape[1], cache_ckv.shape[2]
    kp = kpet_all.shape[2]
    assert past + t_new == n_keys
    fixed = lambda b: (0, 0)
    return pl.pallas_call(
        functools.partial(_sample_attn_kernel, n_keys=n_keys, t_new=t_new),
        grid=(nb,),
        in_specs=[pl.BlockSpec((t_new, MLA_HEADS * QK_PAD), lambda b: (row_block0 + b, 0)),
                  pl.BlockSpec((1, 1, past, KV_LORA), lambda b: (0, b, 0, 0)),
                  pl.BlockSpec((t_new, KV_LORA), lambda b: (row_block0 + b, 0)),
                  pl.BlockSpec((1, ROPE_DIM, kp), lambda b: (b, 0, 0)),
                  pl.BlockSpec((MLA_HEADS * NOPE_DIM, KV_LORA), fixed),
                  pl.BlockSpec((1, NOPE_DIM), fixed),
                  pl.BlockSpec((ROPE_DIM, 1), fixed),
                  pl.BlockSpec((ROPE_DIM, kp), fixed),
                  pl.BlockSpec((ROPE_DIM, kp), fixed)],
        out_specs=pl.BlockSpec((1, MLA_HEADS * t_new, KV_LORA), lambda b: (b, 0, 0)),
        out_shape=jax.ShapeDtypeStruct((nb, MLA_HEADS * t_new, KV_LORA), BF16),
        scratch_shapes=[pltpu.VMEM((kp, KV_LORA), BF16),
                        pltpu.VMEM((MLA_HEADS * t_new, KV_LORA), BF16),
                        pltpu.VMEM((MLA_HEADS * t_new, LANES), BF16)],
        compiler_params=_params("parallel"),
        name="sample_attn",
    )(q, cache_ckv, c_rows, kpet_all, w_ukt, g_k[:NOPE_DIM].reshape(1, NOPE_DIM),
      g_k[NOPE_DIM:].reshape(ROPE_DIM, 1), cost, sint)


def _head_mm_kernel(x_ref, w_ref, o_ref):
    nb, t, r = x_ref.shape
    o_ref[...] = jnp.dot(x_ref[...].reshape(nb * t, r), w_ref[...],
                         preferred_element_type=F32).astype(o_ref.dtype)


def _latent_to_values(o_lat, w_uv2d, t_new):
    nb = o_lat.shape[0]
    return pl.pallas_call(
        _head_mm_kernel,
        grid=(MLA_HEADS,),
        in_specs=[pl.BlockSpec((nb, t_new, KV_LORA), lambda h: (0, h, 0)),
                  pl.BlockSpec((KV_LORA, V_DIM), lambda h: (0, h))],
        out_specs=pl.BlockSpec((nb * t_new, V_DIM), lambda h: (0, h)),
        out_shape=jax.ShapeDtypeStruct((nb * t_new, MLA_HEADS * V_DIM), BF16),
        compiler_params=_params("parallel"),
        name="latent_to_values",
    )(o_lat, w_uv2d)


def _gla_kernel(q_ref, k_ref, v_ref, gate_ref, a_ref, wa_ref, ba_ref, go_ref, s0_ref,
                o_ref, sout_ref, st_ref, *, nsub):
    c = pl.program_id(1)

    @pl.when(c == 0)
    def _():
        for h in range(GLA_HEADS):
            st_ref[h] = s0_ref[0, h].T

    tril = (lax.broadcasted_iota(jnp.int32, (CHUNK, CHUNK), 0)
            >= lax.broadcasted_iota(jnp.int32, (CHUNK, CHUNK), 1))
    tril_b = jnp.where(tril, 1.0, 0.0).astype(BF16)
    for j in range(nsub):
        sl = slice(j * CHUNK, (j + 1) * CHUNK)
        x = jnp.dot(a_ref[sl, :], wa_ref[...], preferred_element_type=F32) + ba_ref[...]
        la = (jnp.minimum(x, 0.0) - jnp.log(1.0 + jnp.exp(-jnp.abs(x)))) * (1.0 / GATE_TAU)
        la_hi = la.astype(BF16)
        la_lo = (la - la_hi.astype(F32)).astype(BF16)
        b_all = (jnp.dot(tril_b, la_hi, preferred_element_type=F32)
                 + jnp.dot(tril_b, la_lo, preferred_element_type=F32))
        for h in range(GLA_HEADS):
            kc = slice(h * GLA_DK, (h + 1) * GLA_DK)
            vc = slice(h * GLA_DV, (h + 1) * GLA_DV)
            b = b_all[:, kc]
            b_last = b[CHUNK - 1:CHUNK, :]
            q = q_ref[sl, kc].astype(F32) * (GLA_DK ** -0.5)
            k = k_ref[sl, kc].astype(F32)
            v = v_ref[sl, vc]
            qs = (q * jnp.exp(b)).astype(BF16)
            ks = (k * jnp.exp(-b)).astype(BF16)
            att = lax.dot_general(qs, ks, NT_DIMS, preferred_element_type=F32)
            att = jnp.where(tril, att, 0.0).astype(BF16)
            st = st_ref[h]
            o = (jnp.dot(att, v, preferred_element_type=F32)
                 + lax.dot_general(qs, st.astype(BF16), NT_DIMS, preferred_element_type=F32))
            kd = (k * jnp.exp(b_last - b)).astype(BF16)
            st_ref[h] = st * jnp.exp(b_last) + lax.dot_general(v, kd, TN_DIMS,
                                                               preferred_element_type=F32)
            gt = gate_ref[sl, vc].astype(F32)
            o_ref[sl, vc] = (_rms(o, go_ref[...]) * (gt / (1.0 + jnp.exp(-gt)))).astype(BF16)

    @pl.when(c == pl.num_programs(1) - 1)
    def _():
        for h in range(GLA_HEADS):
            sout_ref[0, h] = st_ref[h].T


def _gla_scan(z, w_a_pad, b_a, g_o, s0, *, row0, n_streams, t_len, nsub):
    tc = nsub * CHUNK
    nc = t_len // tc
    rb0 = row0 // tc
    rows = lambda b, c: rb0 + b * nc + c
    hk, hv = GLA_HEADS * GLA_DK, GLA_HEADS * GLA_DV
    a_blk = (2 * hk + 2 * hv) // LANES
    fixed = lambda b, c: (0, 0)
    state = lambda b, c: (b, 0, 0, 0)
    return pl.pallas_call(
        functools.partial(_gla_kernel, nsub=nsub),
        grid=(n_streams, nc),
        in_specs=[pl.BlockSpec((tc, hk), lambda b, c: (rows(b, c), 0)),
                  pl.BlockSpec((tc, hk), lambda b, c: (rows(b, c), 1)),
                  pl.BlockSpec((tc, hv), lambda b, c: (rows(b, c), 2 * hk // hv)),
                  pl.BlockSpec((tc, hv), lambda b, c: (rows(b, c), 2 * hk // hv + 1)),
                  pl.BlockSpec((tc, LANES), lambda b, c: (rows(b, c), a_blk)),
                  pl.BlockSpec((LANES, hk), fixed),
                  pl.BlockSpec((1, hk), fixed),
                  pl.BlockSpec((1, GLA_DV), fixed),
                  pl.BlockSpec((1, GLA_HEADS, GLA_DK, GLA_DV), state)],
        out_specs=[pl.BlockSpec((tc, hv), lambda b, c: (b * nc + c, 0)),
                   pl.BlockSpec((1, GLA_HEADS, GLA_DK, GLA_DV), state)],
        out_shape=[jax.ShapeDtypeStruct((n_streams * t_len, hv), BF16),
                   jax.ShapeDtypeStruct((n_streams, GLA_HEADS, GLA_DK, GLA_DV), F32)],
        scratch_shapes=[pltpu.VMEM((GLA_HEADS, GLA_DV, GLA_DK), F32)],
        compiler_params=_params("parallel", "arbitrary"),
        name="gla_scan",
    )(z, z, z, z, z, w_a_pad, b_a.reshape(1, hk), g_o.reshape(1, GLA_DV), s0)


def _router_kernel(x_ref, g_ref, w_ref, b_ref, ids_ref, gates_ref):
    xn = _rms(x_ref[...], g_ref[...]).astype(BF16)
    logits = jnp.dot(xn, w_ref[...], preferred_element_type=F32) + b_ref[...]
    lane = lax.broadcasted_iota(jnp.int32, logits.shape, 1)
    neg = -jnp.inf

    def top(mask):
        vals = jnp.where(mask, logits, neg)
        m = jnp.max(vals, axis=-1, keepdims=True)
        idx = jnp.min(jnp.where(vals == m, lane, LANES), axis=-1, keepdims=True)
        return m, idx

    is_grp = lane < N_GROUPS
    m_g, grp = top(is_grp)
    p_grp = 1.0 / jnp.sum(jnp.where(is_grp, jnp.exp(logits - m_g), 0.0), axis=-1, keepdims=True)
    lo = N_GROUPS + grp * EXPERTS_PER_GROUP
    in_grp = (lane >= lo) & (lane < lo + EXPERTS_PER_GROUP)
    m1, i1 = top(in_grp)
    m2, i2 = top(in_grp & (lane != i1))
    e2 = jnp.exp(m2 - m1)
    g1 = p_grp / (1.0 + e2)
    g2 = p_grp * e2 / (1.0 + e2)
    ids_ref[...] = jnp.where(lane == 0, i1 - N_GROUPS, jnp.where(lane == 1, i2 - N_GROUPS, 0))
    gates_ref[...] = jnp.where(lane == 0, g1, jnp.where(lane == 1, g2, 0.0))


def _router(x, g_ffn, w_r_pad, b_r_pad):
    n, d = x.shape
    tm = ROW_TILE
    row = lambda i: (i, 0)
    fixed = lambda i: (0, 0)
    return pl.pallas_call(
        _router_kernel,
        grid=(n // tm,),
        in_specs=[pl.BlockSpec((tm, d), row), pl.BlockSpec((1, d), fixed),
                  pl.BlockSpec((d, LANES), fixed), pl.BlockSpec((1, LANES), fixed)],
        out_specs=[pl.BlockSpec((tm, LANES), row), pl.BlockSpec((tm, LANES), row)],
        out_shape=[jax.ShapeDtypeStruct((n, LANES), jnp.int32), jax.ShapeDtypeStruct((n, LANES), F32)],
        compiler_params=_params("parallel"),
        name="router",
    )(x, g_ffn.reshape(1, d), w_r_pad, b_r_pad)


def _row_copy(src_hbm, row, dst_vmem, r, sem):
    return pltpu.make_async_copy(src_hbm.at[pl.ds(row, 1)], dst_vmem.at[pl.ds(r, 1)], sem)


def _ffn_kernel(tile0_ref, tile_e_ref, start_ref, order_ref, x_hbm, g_ref, wg_ref, wu_ref, wd_ref,
                y_hbm, pos_ref, xbuf, ybuf, wg_b, wu_b, wd_b, gsem, osem):
    e = pl.program_id(0)
    tm = ybuf.shape[1]
    t_lo, t_hi, n_used = tile0_ref[e], tile0_ref[e + 1], tile0_ref[N_EXPERTS]
    nbuf = xbuf.shape[0]
    ahead = nbuf - 1

    def gather_start(t, slot, unrolled):
        ee = tile_e_ref[t]
        base = start_ref[ee]
        last = start_ref[ee + 1] - base - 1
        off0 = (t - tile0_ref[ee]) * tm

        def one(r):
            a = order_ref[base + jnp.minimum(off0 + r, last)]
            tok = lax.shift_right_logical(a, TOP_K_SHIFT)
            _row_copy(x_hbm, tok, xbuf.at[slot], r, gsem.at[slot]).start(
                priority=r % 2 if isinstance(r, int) else 1)
            pos_ref[a] = t * tm + r

        if unrolled:
            for r in range(tm):
                one(r)
        else:
            lax.fori_loop(0, tm, lambda r, c: (one(r), c)[1], 0)

    def gather_wait(slot):
        pltpu.make_async_copy(x_hbm.at[pl.ds(0, tm)], xbuf.at[slot], gsem.at[slot]).wait()

    def out_copy(t):
        return pltpu.make_async_copy(ybuf.at[t & 1], y_hbm.at[pl.ds(t * tm, tm)], osem.at[t & 1])

    @pl.when((e == 0) & (n_used > 0))
    def _():
        for k in range(ahead):
            gather_start(jnp.minimum(k, n_used - 1), k, False)

    @pl.when(t_hi > t_lo)
    def _():
        wg_b[...] = wg_ref[0, 0].astype(BF16)
        wu_b[...] = wu_ref[0, 0].astype(BF16)
        wd_b[...] = wd_ref[0, 0].astype(BF16)

        def tile(t, _):
            slot = lax.rem(t, nbuf)
            gather_wait(slot)

            @pl.when(t >= 2)
            def _():
                out_copy(t - 2).wait()

            xn = _rms(xbuf[slot], g_ref[...]).astype(BF16)
            gather_start(jnp.minimum(t + ahead, n_used - 1), lax.rem(t + ahead, nbuf), True)
            hg = jnp.dot(xn, wg_b[...], preferred_element_type=F32)
            hu = jnp.dot(xn, wu_b[...], preferred_element_type=F32)
            hid = (hg / (1.0 + jnp.exp(-hg)) * hu).astype(BF16)
            ybuf[t & 1] = jnp.dot(hid, wd_b[...], preferred_element_type=F32)
            out_copy(t).start()
            return 0

        lax.fori_loop(t_lo, t_hi, tile, 0)

    @pl.when(e == pl.num_programs(0) - 1)
    def _():
        @pl.when(n_used >= 1)
        def _():
            for k in range(ahead):
                gather_wait(lax.rem(n_used + k, nbuf))

        for back in (1, 2):
            @pl.when(n_used >= back)
            def _():
                out_copy(n_used - back).wait()

        n_tiles = y_hbm.shape[0] // tm
        ybuf[0] = jnp.zeros(ybuf.shape[1:], F32)

        def zero_copy(t):
            return pltpu.make_async_copy(ybuf.at[0], y_hbm.at[pl.ds(t * tm, tm)], osem.at[0])

        lax.fori_loop(n_used, n_tiles, lambda t, c: (zero_copy(t).start(), c)[1], 0)
        lax.fori_loop(n_used, n_tiles, lambda t, c: (zero_copy(t).wait(), c)[1], 0)


def _expert_ffn(x, g_ffn, w_gate, w_up, w_down, layer, tile0, tile_e, start, order):
    n, d = x.shape
    tm = EXPERT_TILE
    de = w_gate.shape[3]
    fixed = lambda e, *_: (0, 0)
    w_map = lambda e, *_: (layer, e, 0, 0)
    grid_spec = pltpu.PrefetchScalarGridSpec(
        num_scalar_prefetch=4,
        grid=(N_EXPERTS,),
        in_specs=[pl.BlockSpec(memory_space=pl.ANY),
                  pl.BlockSpec((1, d), fixed),
                  pl.BlockSpec((1, 1, d, de), w_map),
                  pl.BlockSpec((1, 1, d, de), w_map),
                  pl.BlockSpec((1, 1, de, d), w_map)],
        out_specs=[pl.BlockSpec(memory_space=pl.ANY), pl.BlockSpec(memory_space=pltpu.SMEM)],
        scratch_shapes=[pltpu.VMEM((GATHER_BUFS, tm, d), F32), pltpu.VMEM((2, tm, d), F32),
                        pltpu.VMEM((d, de), BF16), pltpu.VMEM((d, de), BF16), pltpu.VMEM((de, d), BF16),
                        pltpu.SemaphoreType.DMA((GATHER_BUFS,)), pltpu.SemaphoreType.DMA((2,))],
    )
    return pl.pallas_call(
        _ffn_kernel,
        grid_spec=grid_spec,
        out_shape=[jax.ShapeDtypeStruct((tile_e.shape[0] * tm, d), F32),
                   jax.ShapeDtypeStruct(order.shape, jnp.int32)],
        compiler_params=_params("arbitrary"),
        name="expert_ffn",
    )(tile0, tile_e, start, order, x, g_ffn.reshape(1, d), w_gate, w_up, w_down)


def _combine_kernel(pos_ref, y_hbm, gates_ref, x_ref, *rest, split):
    if split:
        go_ref, o_ref, o2_ref, buf0, buf1, sem = rest
    else:
        o_ref, buf0, buf1, sem = rest
    i = pl.program_id(0)
    tc = buf0.shape[1]

    last = pl.num_programs(0) - 1

    def gather_start(t, slot, unrolled):
        def one(r):
            a = TOP_K * (t * tc + r)
            _row_copy(y_hbm, pos_ref[a], buf0.at[slot], r, sem.at[slot]).start(priority=0)
            _row_copy(y_hbm, pos_ref[a + 1], buf1.at[slot], r, sem.at[slot]).start(priority=1)

        if unrolled:
            for r in range(tc):
                one(r)
        else:
            lax.fori_loop(0, tc, lambda r, c: (one(r), c)[1], 0)

    def gather_wait(slot):
        for buf in (buf0, buf1):
            pltpu.make_async_copy(y_hbm.at[pl.ds(0, tc)], buf.at[slot], sem.at[slot]).wait()

    nbuf = buf0.shape[0]
    ahead = nbuf - 1

    @pl.when(i == 0)
    def _():
        for k in range(ahead):
            gather_start(jnp.minimum(k, last), k, False)

    slot = lax.rem(i, nbuf)
    gather_wait(slot)
    gates = gates_ref[...]
    out = x_ref[...] + buf0[slot] * gates[:, 0:1] + buf1[slot] * gates[:, 1:2]
    gather_start(jnp.minimum(i + ahead, last), lax.rem(i + ahead, nbuf), True)
    if split:
        out = _rms(out, go_ref[...])

        @pl.when(i < split)
        def _():
            o_ref[...] = out

        @pl.when(i >= split)
        def _():
            o2_ref[...] = out
    else:
        o_ref[...] = out

    @pl.when(i == last)
    def _():
        for k in range(1, nbuf):
            gather_wait(lax.rem(i + k, nbuf))


def _combine(x, y_slots, gates, pos, g_out=None, n_first=None):
    n, d = x.shape
    tc = COMBINE_TILE
    split = 0 if g_out is None else n_first // tc
    in_specs = [pl.BlockSpec(memory_space=pl.ANY),
                pl.BlockSpec((tc, LANES), lambda i, p: (i, 0)),
                pl.BlockSpec((tc, d), lambda i, p: (i, 0))]
    args = [pos, y_slots, gates, x]
    out_specs = pl.BlockSpec((tc, d), lambda i, p: (i, 0))
    out_shape = jax.ShapeDtypeStruct((n, d), F32)
    if split:
        in_specs.append(pl.BlockSpec((1, d), lambda i, p: (0, 0)))
        args.append(g_out.reshape(1, d))
        out_specs = [pl.BlockSpec((tc, d), lambda i, p: (jnp.minimum(i, split - 1), 0)),
                     pl.BlockSpec((tc, d), lambda i, p: (jnp.maximum(i - split, 0), 0))]
        out_shape = [jax.ShapeDtypeStruct((n_first, d), F32), jax.ShapeDtypeStruct((n - n_first, d), F32)]
    grid_spec = pltpu.PrefetchScalarGridSpec(
        num_scalar_prefetch=1,
        grid=(n // tc,),
        in_specs=in_specs,
        out_specs=out_specs,
        scratch_shapes=[pltpu.VMEM((GATHER_BUFS, tc, d), F32), pltpu.VMEM((GATHER_BUFS, tc, d), F32),
                        pltpu.SemaphoreType.DMA((GATHER_BUFS,))],
    )
    return pl.pallas_call(
        functools.partial(_combine_kernel, split=split),
        grid_spec=grid_spec,
        out_shape=out_shape,
        compiler_params=_params("arbitrary"),
        name="moe_combine",
    )(*args)


def _dispatch_plan(ids):
    n = ids.shape[0]
    a = n * TOP_K
    tm = EXPERT_TILE
    i32 = jnp.int32
    e_flat = ids[:, :TOP_K].reshape(a)
    order = jnp.argsort(e_flat, stable=True).astype(i32)
    experts = jnp.arange(N_EXPERTS, dtype=i32)
    counts = jnp.sum((e_flat[:, None] == experts[None, :]).astype(i32), axis=0)
    zero = jnp.zeros((1,), i32)
    start = jnp.concatenate([zero, jnp.cumsum(counts)]).astype(i32)
    tile_end = jnp.cumsum((counts + tm - 1) // tm)
    tile0 = jnp.concatenate([zero, tile_end]).astype(i32)
    n_tiles = (a + N_EXPERTS * (tm - 1)) // tm
    tile_ids = jnp.arange(n_tiles, dtype=i32)
    tile_e = jnp.minimum(jnp.sum((tile_end[None, :] <= tile_ids[:, None]).astype(i32), axis=1),
                         N_EXPERTS - 1).astype(i32)
    return tile0, tile_e, start, order


def _hier_moe(x, g_ffn, w_r_pad, b_r_pad, w_gate, w_up, w_down, layer, g_out=None, n_first=None):
    ids, gates = _router(x, g_ffn, w_r_pad, b_r_pad)
    tile0, tile_e, start, order = _dispatch_plan(ids)
    y_slots, pos = _expert_ffn(x, g_ffn, w_gate, w_up, w_down, layer, tile0, tile_e, start, order)
    return _combine(x, y_slots, gates, pos, g_out, n_first)


def _rope_table(pos):
    inv = jnp.power(ROPE_THETA, -jnp.arange(ROPE_HALF, dtype=F32) * (2.0 / ROPE_DIM))
    ang = pos[:, None] * inv[None, :]
    return jnp.cos(ang), jnp.sin(ang)


def _pad_cols(w, n):
    return jnp.pad(w, ((0, 0), (0, n - w.shape[1])))


def _q_head_weights(w_uq):
    w = w_uq.reshape(Q_LORA, MLA_HEADS, QK_DIM)
    r1 = w[:, :, NOPE_DIM:NOPE_DIM + ROPE_HALF]
    r2 = w[:, :, NOPE_DIM + ROPE_HALF:]
    return jnp.concatenate([w, -r2, r1], axis=-1).reshape(Q_LORA, MLA_HEADS * QK_PAD).astype(BF16)


def kernel(x_prompt, x_sample, cache_mla_ckv, cache_mla_kpe, state_gla, norm_mix, norm_ffn, norm_out, mla_w_in, mla_g_q_lat, mla_g_kv_lat, mla_w_uq, mla_w_uk, mla_w_uv, mla_g_q, mla_g_k, mla_w_o, gla_w_in, gla_w_a, gla_b_a, gla_g_o, gla_w_o, moe_w_router, moe_b_router, moe_w_gate, moe_w_up, moe_w_down):
    batch, seq, d = x_prompt.shape
    nb, t_new, _ = x_sample.shape
    past = cache_mla_ckv.shape[2]
    n_p, n_s = batch * seq, nb * t_new
    n = n_p + n_s
    x = jnp.concatenate([x_prompt.reshape(n_p, d), x_sample.reshape(n_s, d)], axis=0)

    pos_rows = jnp.concatenate([jnp.tile(jnp.arange(seq, dtype=F32), batch),
                                jnp.tile(past + jnp.arange(t_new, dtype=F32), nb)])
    cos_r, sin_r = _rope_table(pos_rows)
    tab = jnp.concatenate([cos_r, cos_r, sin_r, sin_r], axis=1)
    w_in_pad = _pad_cols(mla_w_in[0], 9 * LANES).astype(BF16)
    c_q, c_kv, c_kv_b, k_pe = _mla_in(x, norm_mix[0], w_in_pad, mla_g_q_lat[0], mla_g_kv_lat[0])
    q = _mla_q(c_q, _q_head_weights(mla_w_uq[0]), mla_g_q[0], tab)

    w_uk, w_uv = mla_w_uk[0], mla_w_uv[0]
    w_kv_heads = jnp.concatenate([w_uk, w_uv], axis=-1).reshape(KV_LORA, -1).astype(BF16)
    k_p, v_p = _mla_kv(c_kv_b, k_pe, w_kv_heads, mla_g_k[0], tab, n_p)
    o_p = _flash_prompt(q, k_p, v_p, batch, seq)

    n_keys = past + t_new
    kp = (n_keys + LANES - 1) // LANES * LANES
    kpe_new = k_pe[n_p:].reshape(nb, t_new, ROPE_DIM)
    kpe_all = jnp.concatenate([cache_mla_kpe[0], kpe_new,
                               jnp.zeros((nb, kp - n_keys, ROPE_DIM), F32)], axis=1)
    kpet_all = kpe_all.transpose(0, 2, 1)
    cos_k, sin_k = _rope_table(jnp.arange(kp, dtype=F32))
    cost = jnp.concatenate([cos_k, cos_k], axis=1).T
    sint = jnp.concatenate([sin_k, sin_k], axis=1).T
    w_ukt = w_uk.reshape(KV_LORA, -1).T.astype(BF16)
    o_lat = _sample_attn(q, cache_mla_ckv, c_kv_b, kpet_all, w_ukt, mla_g_k[0], cost, sint,
                         row_block0=n_p // t_new, n_keys=n_keys, t_new=t_new)
    o_s = _latent_to_values(o_lat, w_uv.reshape(KV_LORA, -1).astype(BF16), t_new)

    x = _mm(o_p, mla_w_o[0].astype(BF16), x2=o_s, res=x, out_dtype=F32, tm=MM_ROW_TILE, tn=512)

    w_r_pad = [_pad_cols(moe_w_router[i], LANES).astype(BF16) for i in range(2)]
    b_r_pad = [_pad_cols(moe_b_router[i].reshape(1, -1), LANES) for i in range(2)]
    x = _hier_moe(x, norm_ffn[0], w_r_pad[0], b_r_pad[0], moe_w_gate, moe_w_up, moe_w_down, 0)

    hk, hv = GLA_HEADS * GLA_DK, GLA_HEADS * GLA_DV
    wg = gla_w_in[0]
    w_gla = jnp.concatenate([wg[:, :2 * hk + hv], wg[:, 2 * hk + hv + GATE_RANK:],
                             _pad_cols(wg[:, 2 * hk + hv:2 * hk + hv + GATE_RANK], LANES)],
                            axis=1).astype(BF16)
    z = _mm(x, w_gla, gain=norm_mix[1], out_dtype=BF16, tm=MM_ROW_TILE, tn=7 * LANES)
    w_a_pad = jnp.pad(gla_w_a[0], ((0, LANES - GATE_RANK), (0, 0))).astype(BF16)
    s0_p = jnp.zeros((batch, GLA_HEADS, GLA_DK, GLA_DV), F32)
    og_p, st_p = _gla_scan(z, w_a_pad, gla_b_a[0], gla_g_o[0], s0_p,
                           row0=0, n_streams=batch, t_len=seq, nsub=GLA_SUBCHUNKS)
    og_s, st_s = _gla_scan(z, w_a_pad, gla_b_a[0], gla_g_o[0], state_gla[0],
                           row0=n_p, n_streams=nb, t_len=t_new, nsub=t_new // CHUNK)
    x = _mm(og_p, gla_w_o[0].astype(BF16), x2=og_s, res=x, out_dtype=F32, tm=MM_ROW_TILE, tn=512)
    y_p, y_s = _hier_moe(x, norm_ffn[1], w_r_pad[1], b_r_pad[1], moe_w_gate, moe_w_up, moe_w_down, 1,
                         g_out=norm_out, n_first=n_p)

    return (y_p.reshape(batch, seq, d),
            y_s.reshape(nb, t_new, d),
            c_kv[:n_p].reshape(1, batch, seq, KV_LORA),
            k_pe[:n_p].reshape(1, batch, seq, ROPE_DIM),
            st_p[None],
            c_kv[n_p:].reshape(1, nb, t_new, KV_LORA),
            k_pe[n_p:].reshape(1, nb, t_new, ROPE_DIM),
            st_s[None])
```

```python
import functools

import jax
import jax.numpy as jnp
from jax import lax
from jax.experimental import pallas as pl
from jax.experimental.pallas import tpu as pltpu

F32 = jnp.float32
BF16 = jnp.bfloat16

LANES = 128
V7X_VMEM_BYTES = 64 * 1024 * 1024
VMEM_LIMIT = V7X_VMEM_BYTES * 3 // 4

EPS = 1e-6
CHUNK = 64
MLA_HEADS = 16
NOPE_DIM = 128
ROPE_DIM = 64
ROPE_HALF = ROPE_DIM // 2
QK_DIM = NOPE_DIM + ROPE_DIM
QK_PAD = 2 * LANES
V_DIM = 128
Q_LORA = 512
KV_LORA = 512
ROPE_THETA = 10000.0
ATTN_SCALE = QK_DIM ** -0.5
GLA_HEADS = 4
GLA_DK = 256
GLA_DV = 512
GATE_RANK = 16
GATE_TAU = 16.0
N_GROUPS = 8
EXPERTS_PER_GROUP = 8
N_EXPERTS = N_GROUPS * EXPERTS_PER_GROUP
TOP_K = 2
TOP_K_SHIFT = 1
assert 1 << TOP_K_SHIFT == TOP_K

ROW_TILE = 512
MM_ROW_TILE = 1024
FLASH_TQ = 512
FLASH_TK = 512
FLASH_HEADS = 2
SAMPLE_HEAD_GROUP = 4
GLA_SUBCHUNKS = 4
EXPERT_TILE = 128
COMBINE_TILE = 128
GATHER_BUFS = 8
COMBINE_BUFS = 3
OUT_BUFS = 4

NT_DIMS = (((1,), (1,)), ((), ()))
TN_DIMS = (((0,), (0,)), ((), ()))


def _params(*sem):
    return pltpu.CompilerParams(dimension_semantics=sem, vmem_limit_bytes=VMEM_LIMIT)


def _rms(x, g):
    return x * lax.rsqrt(jnp.mean(x * x, axis=-1, keepdims=True) + EPS) * g


def _mm_kernel(*refs, has_norm, has_res, split, res_split):
    it = iter(refs)
    x_ref = next(it)
    x2_ref = next(it) if split else None
    g_ref = next(it) if has_norm else None
    w_ref = next(it)
    r_ref = next(it) if has_res else None
    r2_ref = next(it) if res_split else r_ref
    o_ref = next(it)
    if has_norm:
        xn_ref = next(it)

        @pl.when(pl.program_id(1) == 0)
        def _():
            xn_ref[...] = _rms(x_ref[...], g_ref[...]).astype(BF16)

        x_ref = xn_ref

    def emit(src_ref, res_ref):
        acc = jnp.dot(src_ref[...], w_ref[...], preferred_element_type=F32)
        if has_res:
            acc = acc + res_ref[...]
        o_ref[...] = acc.astype(o_ref.dtype)

    if split:
        pl.when(pl.program_id(0) < split)(lambda: emit(x_ref, r_ref))
        pl.when(pl.program_id(0) >= split)(lambda: emit(x2_ref, r2_ref))
    else:
        emit(x_ref, r_ref)


def _mm(x, w, *, x2=None, gain=None, res=None, res2=None, out_dtype, tm, tn):
    m, k = x.shape
    n = w.shape[1]
    has_norm, has_res = gain is not None, res is not None
    split = 0
    in_specs = [pl.BlockSpec((tm, k), lambda i, j: (i, 0))]
    args = [x]
    if x2 is not None:
        assert not has_norm
        split = m // tm
        m += x2.shape[0]
        first = lambda i: jnp.minimum(i, split - 1)
        second = lambda i: jnp.maximum(i - split, 0)
        in_specs = [pl.BlockSpec((tm, k), lambda i, j: (first(i), 0)),
                    pl.BlockSpec((tm, k), lambda i, j: (second(i), 0))]
        args.append(x2)
    if has_norm:
        in_specs.append(pl.BlockSpec((1, k), lambda i, j: (0, 0)))
        args.append(gain.reshape(1, k))
    in_specs.append(pl.BlockSpec((k, tn), lambda i, j: (0, j)))
    args.append(w)
    res_split = res2 is not None
    if res_split:
        assert split and res.shape[0] == split * tm
        in_specs += [pl.BlockSpec((tm, tn), lambda i, j: (first(i), j)),
                     pl.BlockSpec((tm, tn), lambda i, j: (second(i), j))]
        args += [res, res2]
    elif has_res:
        in_specs.append(pl.BlockSpec((tm, tn), lambda i, j: (i, j)))
        args.append(res)
    return pl.pallas_call(
        functools.partial(_mm_kernel, has_norm=has_norm, has_res=has_res, split=split,
                          res_split=res_split),
        grid=(m // tm, n // tn),
        in_specs=in_specs,
        out_specs=pl.BlockSpec((tm, tn), lambda i, j: (i, j)),
        out_shape=jax.ShapeDtypeStruct((m, n), out_dtype),
        scratch_shapes=[pltpu.VMEM((tm, k), BF16)] if has_norm else [],
        compiler_params=_params("parallel", "arbitrary"),
        name="mm",
    )(*args)


def _mla_in_kernel(x_ref, x2_ref, g_ref, w_ref, gq_ref, gkv_ref, cq_ref, ckv_ref, ckvb_ref, kpe_ref,
                   *, split):
    def emit(src_ref):
        xn = _rms(src_ref[...], g_ref[...]).astype(BF16)
        z = jnp.dot(xn, w_ref[...], preferred_element_type=F32)
        cq_ref[...] = _rms(z[:, :Q_LORA], gq_ref[...]).astype(BF16)
        ckv = _rms(z[:, Q_LORA:Q_LORA + KV_LORA], gkv_ref[...])
        ckv_ref[...] = ckv
        ckvb_ref[...] = ckv.astype(BF16)
        kpe_ref[...] = z[:, Q_LORA + KV_LORA:Q_LORA + KV_LORA + ROPE_DIM]

    pl.when(pl.program_id(0) < split)(lambda: emit(x_ref))
    pl.when(pl.program_id(0) >= split)(lambda: emit(x2_ref))


def _mla_in(x, x2, g_mix, w_in_pad, g_q_lat, g_kv_lat):
    d = x.shape[1]
    tm = ROW_TILE
    split = x.shape[0] // tm
    n = x.shape[0] + x2.shape[0]
    wn = w_in_pad.shape[1]
    row = lambda i: (i, 0)
    fixed = lambda i: (0, 0)
    return pl.pallas_call(
        functools.partial(_mla_in_kernel, split=split),
        grid=(n // tm,),
        in_specs=[pl.BlockSpec((tm, d), lambda i: (jnp.minimum(i, split - 1), 0)),
                  pl.BlockSpec((tm, d), lambda i: (jnp.maximum(i - split, 0), 0)),
                  pl.BlockSpec((1, d), fixed),
                  pl.BlockSpec((d, wn), fixed), pl.BlockSpec((1, Q_LORA), fixed),
                  pl.BlockSpec((1, KV_LORA), fixed)],
        out_specs=[pl.BlockSpec((tm, Q_LORA), row), pl.BlockSpec((tm, KV_LORA), row),
                   pl.BlockSpec((tm, KV_LORA), row), pl.BlockSpec((tm, ROPE_DIM), row)],
        out_shape=[jax.ShapeDtypeStruct((n, Q_LORA), BF16), jax.ShapeDtypeStruct((n, KV_LORA), F32),
                   jax.ShapeDtypeStruct((n, KV_LORA), BF16), jax.ShapeDtypeStruct((n, ROPE_DIM), F32)],
        compiler_params=_params("parallel"),
        name="mla_in",
    )(x, x2, g_mix.reshape(1, d), w_in_pad, g_q_lat.reshape(1, -1), g_kv_lat.reshape(1, -1))


def _mla_q_kernel(cq_ref, w_ref, gn_ref, g2_ref, tab_ref, q_ref):
    cq = cq_ref[...]
    gtab = g2_ref[...] * tab_ref[...]
    is_rope = lax.broadcasted_iota(jnp.int32, gtab.shape, 1) < ROPE_DIM
    for h in range(MLA_HEADS):
        cols = slice(h * QK_PAD, (h + 1) * QK_PAD)
        t = jnp.dot(cq, w_ref[:, cols], preferred_element_type=F32)
        t1 = t[:, :NOPE_DIM]
        t2 = t[:, NOPE_DIM:]
        ss = jnp.sum(t1 * t1 + jnp.where(is_rope, t2 * t2, 0.0), axis=-1, keepdims=True)
        rs = lax.rsqrt(ss * (1.0 / QK_DIM) + EPS) * ATTN_SCALE
        u = t2 * gtab
        u = jnp.where(is_rope, u + pltpu.roll(u, ROPE_DIM, axis=1), 0.0)
        q_ref[:, cols] = jnp.concatenate([t1 * gn_ref[...] * rs, u * rs], axis=1).astype(BF16)


def _mla_q(cq, w_q_heads, g_q, tab):
    n = cq.shape[0]
    tm = ROW_TILE
    g1, g2 = g_q[NOPE_DIM:NOPE_DIM + ROPE_HALF], g_q[NOPE_DIM + ROPE_HALF:]
    g_rope = jnp.concatenate([g1, g2, g2, g1]).reshape(1, LANES)
    row = lambda i: (i, 0)
    fixed = lambda i: (0, 0)
    return pl.pallas_call(
        _mla_q_kernel,
        grid=(n // tm,),
        in_specs=[pl.BlockSpec((tm, Q_LORA), row),
                  pl.BlockSpec((Q_LORA, MLA_HEADS * QK_PAD), fixed),
                  pl.BlockSpec((1, NOPE_DIM), fixed),
                  pl.BlockSpec((1, LANES), fixed),
                  pl.BlockSpec((tm, LANES), row)],
        out_specs=pl.BlockSpec((tm, MLA_HEADS * QK_PAD), row),
        out_shape=jax.ShapeDtypeStruct((n, MLA_HEADS * QK_PAD), BF16),
        compiler_params=_params("parallel"),
        name="mla_q",
    )(cq, w_q_heads, g_q[:NOPE_DIM].reshape(1, NOPE_DIM), g_rope, tab)


def _rotate_half_rows(x):
    return jnp.concatenate([-x[ROPE_HALF:], x[:ROPE_HALF]], axis=0)


def _mla_kv_kernel(c_ref, kpe_ref, w_ref, gn_ref, gr_ref, tab_ref, k_ref, v_ref):
    c = c_ref[...]
    kpe = kpe_ref[...]
    sspe = jnp.sum(kpe * kpe, axis=-1, keepdims=True)
    kg = kpe * gr_ref[...]
    rot = jnp.concatenate([-kg[:, ROPE_HALF:], kg[:, :ROPE_HALF]], axis=1)
    tab = tab_ref[...]
    kr = kg * tab[:, :ROPE_DIM] + rot * tab[:, ROPE_DIM:]
    kr = jnp.concatenate([kr, jnp.zeros_like(kr)], axis=1)
    for h in range(MLA_HEADS):
        t = jnp.dot(c, w_ref[:, h * QK_PAD:(h + 1) * QK_PAD], preferred_element_type=F32)
        kn = t[:, :NOPE_DIM]
        rs = lax.rsqrt((jnp.sum(kn * kn, axis=-1, keepdims=True) + sspe) * (1.0 / QK_DIM) + EPS)
        k_ref[:, h * QK_PAD:(h + 1) * QK_PAD] = jnp.concatenate(
            [kn * gn_ref[...] * rs, kr * rs], axis=1).astype(BF16)
        v_ref[:, h * V_DIM:(h + 1) * V_DIM] = t[:, NOPE_DIM:].astype(BF16)


def _mla_kv(ckv_b, kpe, w_kv_heads, g_k, tab, n_rows):
    tm = ROW_TILE
    row = lambda i: (i, 0)
    fixed = lambda i: (0, 0)
    return pl.pallas_call(
        _mla_kv_kernel,
        grid=(n_rows // tm,),
        in_specs=[pl.BlockSpec((tm, KV_LORA), row),
                  pl.BlockSpec((tm, ROPE_DIM), row),
                  pl.BlockSpec((KV_LORA, MLA_HEADS * (NOPE_DIM + V_DIM)), fixed),
                  pl.BlockSpec((1, NOPE_DIM), fixed),
                  pl.BlockSpec((1, ROPE_DIM), fixed),
                  pl.BlockSpec((tm, LANES), row)],
        out_specs=[pl.BlockSpec((tm, MLA_HEADS * QK_PAD), row),
                   pl.BlockSpec((tm, MLA_HEADS * V_DIM), row)],
        out_shape=[jax.ShapeDtypeStruct((n_rows, MLA_HEADS * QK_PAD), BF16),
                   jax.ShapeDtypeStruct((n_rows, MLA_HEADS * V_DIM), BF16)],
        compiler_params=_params("parallel"),
        name="mla_kv",
    )(ckv_b, kpe, w_kv_heads, g_k[:NOPE_DIM].reshape(1, NOPE_DIM),
      g_k[NOPE_DIM:].reshape(1, ROPE_DIM), tab)


def _flash_kernel(q_ref, k_ref, v_ref, o_ref, *, tq, tk):
    qi = pl.program_id(2)

    def step(j, carries, masked):
        ks = pl.multiple_of(j * tk, tk)
        out = []
        for hh, (m, l, acc) in enumerate(carries):
            qk = slice(hh * QK_PAD, (hh + 1) * QK_PAD)
            vv = slice(hh * V_DIM, (hh + 1) * V_DIM)
            s = lax.dot_general(q_ref[:, qk], k_ref[pl.ds(ks, tk), qk], NT_DIMS,
                                preferred_element_type=F32)
            if masked:
                row = lax.broadcasted_iota(jnp.int32, s.shape, 0) // CHUNK
                col = lax.broadcasted_iota(jnp.int32, s.shape, 1) // CHUNK
                s = jnp.where(col <= row, s, -jnp.inf)
            m_new = jnp.maximum(m, jnp.max(s, axis=-1, keepdims=True))
            p = jnp.exp(s - m_new)
            alpha = jnp.exp(m - m_new)
            l = alpha * l + jnp.sum(p, axis=-1, keepdims=True)
            acc = alpha * acc + jnp.dot(p.astype(BF16), v_ref[pl.ds(ks, tk), vv],
                                        preferred_element_type=F32)
            out.append((m_new, l, acc))
        return tuple(out)

    init = tuple((jnp.full((tq, 1), -jnp.inf, F32), jnp.zeros((tq, 1), F32),
                  jnp.zeros((tq, V_DIM), F32)) for _ in range(FLASH_HEADS))
    carries = lax.fori_loop(0, qi, lambda j, c: step(j, c, False), init)
    carries = step(qi, carries, True)
    for hh, (_, l, acc) in enumerate(carries):
        o_ref[:, hh * V_DIM:(hh + 1) * V_DIM] = (acc / l).astype(BF16)


def _flash_prompt(q, k, v, batch, seq):
    tq, tk, hs = FLASH_TQ, FLASH_TK, FLASH_HEADS
    assert tq == tk
    nq = seq // tq
    return pl.pallas_call(
        functools.partial(_flash_kernel, tq=tq, tk=tk),
        grid=(batch, MLA_HEADS // hs, nq),
        in_specs=[pl.BlockSpec((tq, hs * QK_PAD), lambda b, h, i: (b * nq + i, h)),
                  pl.BlockSpec((seq, hs * QK_PAD), lambda b, h, i: (b, h)),
                  pl.BlockSpec((seq, hs * V_DIM), lambda b, h, i: (b, h))],
        out_specs=pl.BlockSpec((tq, hs * V_DIM), lambda b, h, i: (b * nq + i, h)),
        out_shape=jax.ShapeDtypeStruct((batch * seq, MLA_HEADS * V_DIM), BF16),
        compiler_params=_params("parallel", "parallel", "arbitrary"),
        name="flash_prompt",
    )(q, k, v)


def _sample_attn_kernel(q_ref, cache_ref, cnew_ref, kpet_ref, wukt_ref, gkn_ref, gkrt_ref, cost_ref,
                        sint_ref, olat_ref, c_sc, qabs_ref, qr_ref, *, n_keys, t_new):
    kp = c_sc.shape[0]
    past = n_keys - t_new
    c_sc[:past, :] = cache_ref[0, 0].astype(BF16)
    c_sc[past:n_keys, :] = cnew_ref[...]
    c_sc[n_keys:, :] = jnp.zeros((kp - n_keys, KV_LORA), BF16)
    c = c_sc[...]
    kpet = kpet_ref[0]
    sspe = jnp.sum(kpet * kpet, axis=0, keepdims=True)
    kg = kpet * gkrt_ref[...]
    krt = kg * cost_ref[...] + _rotate_half_rows(kg) * sint_ref[...]
    krt = jnp.concatenate([krt, jnp.zeros_like(krt)], axis=0).astype(BF16)
    gkn = gkn_ref[...]
    for h in range(MLA_HEADS):
        qh = q_ref[:, h * QK_PAD:(h + 1) * QK_PAD]
        qn = (qh[:, :NOPE_DIM].astype(F32) * gkn).astype(BF16)
        qa = jnp.dot(qn, wukt_ref[h * NOPE_DIM:(h + 1) * NOPE_DIM, :], preferred_element_type=F32)
        qabs_ref[h * t_new:(h + 1) * t_new, :] = qa.astype(BF16)
        qr_ref[h * t_new:(h + 1) * t_new, :] = qh[:, NOPE_DIM:]
    valid = lax.broadcasted_iota(jnp.int32, (t_new, kp), 1) < n_keys
    hg = SAMPLE_HEAD_GROUP
    for g in range(MLA_HEADS // hg):
        knt = lax.dot_general(wukt_ref[g * hg * NOPE_DIM:(g + 1) * hg * NOPE_DIM, :], c, NT_DIMS,
                              preferred_element_type=F32)
        ss = jnp.sum((knt * knt).reshape(hg, NOPE_DIM, kp), axis=1) + sspe
        rst = lax.rsqrt(ss * (1.0 / QK_DIM) + EPS)
        rows = slice(g * hg * t_new, (g + 1) * hg * t_new)
        s = (lax.dot_general(qabs_ref[rows, :], c, NT_DIMS, preferred_element_type=F32)
             + jnp.dot(qr_ref[rows, :], krt, preferred_element_type=F32))
        ps = []
        for hh in range(hg):
            sh = s[hh * t_new:(hh + 1) * t_new] * rst[hh:hh + 1]
            sh = jnp.where(valid, sh, -jnp.inf)
            e = jnp.exp(sh - jnp.max(sh, axis=-1, keepdims=True))
            ps.append((e / jnp.sum(e, axis=-1, keepdims=True)).astype(BF16))
        p = jnp.concatenate(ps, axis=0)
        olat_ref[0, rows, :] = jnp.dot(p, c, preferred_element_type=F32).astype(BF16)


def _sample_attn(q, cache_ckv, c_rows, kpet_all, w_ukt, g_k, cost, sint, *, row_block0, n_keys, t_new):
    nb, past = cache_ckv.shape[1], cache_ckv.shape[2]
    kp = kpet_all.shape[2]
    assert past + t_new == n_keys
    fixed = lambda b: (0, 0)
    return pl.pallas_call(
        functools.partial(_sample_attn_kernel, n_keys=n_keys, t_new=t_new),
        grid=(nb,),
        in_specs=[pl.BlockSpec((t_new, MLA_HEADS * QK_PAD), lambda b: (row_block0 + b, 0)),
                  pl.BlockSpec((1, 1, past, KV_LORA), lambda b: (0, b, 0, 0)),
                  pl.BlockSpec((t_new, KV_LORA), lambda b: (row_block0 + b, 0)),
                  pl.BlockSpec((1, ROPE_DIM, kp), lambda b: (b, 0, 0)),
                  pl.BlockSpec((MLA_HEADS * NOPE_DIM, KV_LORA), fixed),
                  pl.BlockSpec((1, NOPE_DIM), fixed),
                  pl.BlockSpec((ROPE_DIM, 1), fixed),
                  pl.BlockSpec((ROPE_DIM, kp), fixed),
                  pl.BlockSpec((ROPE_DIM, kp), fixed)],
        out_specs=pl.BlockSpec((1, MLA_HEADS * t_new, KV_LORA), lambda b: (b, 0, 0)),
        out_shape=jax.ShapeDtypeStruct((nb, MLA_HEADS * t_new, KV_LORA), BF16),
        scratch_shapes=[pltpu.VMEM((kp, KV_LORA), BF16),
                        pltpu.VMEM((MLA_HEADS * t_new, KV_LORA), BF16),
                        pltpu.VMEM((MLA_HEADS * t_new, LANES), BF16)],
        compiler_params=_params("parallel"),
        name="sample_attn",
    )(q, cache_ckv, c_rows, kpet_all, w_ukt, g_k[:NOPE_DIM].reshape(1, NOPE_DIM),
      g_k[NOPE_DIM:].reshape(ROPE_DIM, 1), cost, sint)


def _head_mm_kernel(x_ref, w_ref, o_ref):
    nb, t, r = x_ref.shape
    o_ref[...] = jnp.dot(x_ref[...].reshape(nb * t, r), w_ref[...],
                         preferred_element_type=F32).astype(o_ref.dtype)


def _latent_to_values(o_lat, w_uv2d, t_new):
    nb = o_lat.shape[0]
    return pl.pallas_call(
        _head_mm_kernel,
        grid=(MLA_HEADS,),
        in_specs=[pl.BlockSpec((nb, t_new, KV_LORA), lambda h: (0, h, 0)),
                  pl.BlockSpec((KV_LORA, V_DIM), lambda h: (0, h))],
        out_specs=pl.BlockSpec((nb * t_new, V_DIM), lambda h: (0, h)),
        out_shape=jax.ShapeDtypeStruct((nb * t_new, MLA_HEADS * V_DIM), BF16),
        compiler_params=_params("parallel"),
        name="latent_to_values",
    )(o_lat, w_uv2d)


def _gla_kernel(q_ref, k_ref, v_ref, gate_ref, a_ref, wa_ref, ba_ref, go_ref, s0_ref,
                o_ref, sout_ref, st_ref, *, nsub):
    c = pl.program_id(1)

    @pl.when(c == 0)
    def _():
        for h in range(GLA_HEADS):
            st_ref[h] = s0_ref[0, h].T

    tril = (lax.broadcasted_iota(jnp.int32, (CHUNK, CHUNK), 0)
            >= lax.broadcasted_iota(jnp.int32, (CHUNK, CHUNK), 1))
    tril_b = jnp.where(tril, 1.0, 0.0).astype(BF16)
    for j in range(nsub):
        sl = slice(j * CHUNK, (j + 1) * CHUNK)
        x = jnp.dot(a_ref[sl, :], wa_ref[...], preferred_element_type=F32) + ba_ref[...]
        la = (jnp.minimum(x, 0.0) - jnp.log(1.0 + jnp.exp(-jnp.abs(x)))) * (1.0 / GATE_TAU)
        la_hi = la.astype(BF16)
        la_lo = (la - la_hi.astype(F32)).astype(BF16)
        b_all = (jnp.dot(tril_b, la_hi, preferred_element_type=F32)
                 + jnp.dot(tril_b, la_lo, preferred_element_type=F32))
        for h in range(GLA_HEADS):
            kc = slice(h * GLA_DK, (h + 1) * GLA_DK)
            vc = slice(h * GLA_DV, (h + 1) * GLA_DV)
            b = b_all[:, kc]
            b_last = b[CHUNK - 1:CHUNK, :]
            q = q_ref[sl, kc].astype(F32) * (GLA_DK ** -0.5)
            k = k_ref[sl, kc].astype(F32)
            v = v_ref[sl, vc]
            qs = (q * jnp.exp(b)).astype(BF16)
            ks = (k * jnp.exp(-b)).astype(BF16)
            att = lax.dot_general(qs, ks, NT_DIMS, preferred_element_type=F32)
            att = jnp.where(tril, att, 0.0).astype(BF16)
            st = st_ref[h]
            o = (jnp.dot(att, v, preferred_element_type=F32)
                 + lax.dot_general(qs, st.astype(BF16), NT_DIMS, preferred_element_type=F32))
            kd = (k * jnp.exp(b_last - b)).astype(BF16)
            st_ref[h] = st * jnp.exp(b_last) + lax.dot_general(v, kd, TN_DIMS,
                                                               preferred_element_type=F32)
            gt = gate_ref[sl, vc].astype(F32)
            o_ref[sl, vc] = (_rms(o, go_ref[...]) * (gt / (1.0 + jnp.exp(-gt)))).astype(BF16)

    @pl.when(c == pl.num_programs(1) - 1)
    def _():
        for h in range(GLA_HEADS):
            sout_ref[0, h] = st_ref[h].T


def _gla_scan(z, w_a_pad, b_a, g_o, s0, *, row0, n_streams, t_len, nsub):
    tc = nsub * CHUNK
    nc = t_len // tc
    rb0 = row0 // tc
    rows = lambda b, c: rb0 + b * nc + c
    hk, hv = GLA_HEADS * GLA_DK, GLA_HEADS * GLA_DV
    a_blk = (2 * hk + 2 * hv) // LANES
    fixed = lambda b, c: (0, 0)
    state = lambda b, c: (b, 0, 0, 0)
    return pl.pallas_call(
        functools.partial(_gla_kernel, nsub=nsub),
        grid=(n_streams, nc),
        in_specs=[pl.BlockSpec((tc, hk), lambda b, c: (rows(b, c), 0)),
                  pl.BlockSpec((tc, hk), lambda b, c: (rows(b, c), 1)),
                  pl.BlockSpec((tc, hv), lambda b, c: (rows(b, c), 2 * hk // hv)),
                  pl.BlockSpec((tc, hv), lambda b, c: (rows(b, c), 2 * hk // hv + 1)),
                  pl.BlockSpec((tc, LANES), lambda b, c: (rows(b, c), a_blk)),
                  pl.BlockSpec((LANES, hk), fixed),
                  pl.BlockSpec((1, hk), fixed),
                  pl.BlockSpec((1, GLA_DV), fixed),
                  pl.BlockSpec((1, GLA_HEADS, GLA_DK, GLA_DV), state)],
        out_specs=[pl.BlockSpec((tc, hv), lambda b, c: (b * nc + c, 0)),
                   pl.BlockSpec((1, GLA_HEADS, GLA_DK, GLA_DV), state)],
        out_shape=[jax.ShapeDtypeStruct((n_streams * t_len, hv), BF16),
                   jax.ShapeDtypeStruct((n_streams, GLA_HEADS, GLA_DK, GLA_DV), F32)],
        scratch_shapes=[pltpu.VMEM((GLA_HEADS, GLA_DV, GLA_DK), F32)],
        compiler_params=_params("parallel", "arbitrary"),
        name="gla_scan",
    )(z, z, z, z, z, w_a_pad, b_a.reshape(1, hk), g_o.reshape(1, GLA_DV), s0)


def _router_kernel(x_ref, g_ref, w_ref, b_ref, ids_ref, gates_ref):
    xn = _rms(x_ref[...], g_ref[...]).astype(BF16)
    logits = jnp.dot(xn, w_ref[...], preferred_element_type=F32) + b_ref[...]
    lane = lax.broadcasted_iota(jnp.int32, logits.shape, 1)
    neg = -jnp.inf

    def top(mask):
        vals = jnp.where(mask, logits, neg)
        m = jnp.max(vals, axis=-1, keepdims=True)
        idx = jnp.min(jnp.where(vals == m, lane, LANES), axis=-1, keepdims=True)
        return m, idx

    is_grp = lane < N_GROUPS
    m_g, grp = top(is_grp)
    p_grp = 1.0 / jnp.sum(jnp.where(is_grp, jnp.exp(logits - m_g), 0.0), axis=-1, keepdims=True)
    lo = N_GROUPS + grp * EXPERTS_PER_GROUP
    in_grp = (lane >= lo) & (lane < lo + EXPERTS_PER_GROUP)
    m1, i1 = top(in_grp)
    m2, i2 = top(in_grp & (lane != i1))
    e2 = jnp.exp(m2 - m1)
    g1 = p_grp / (1.0 + e2)
    g2 = p_grp * e2 / (1.0 + e2)
    ids_ref[...] = jnp.where(lane == 0, i1 - N_GROUPS, jnp.where(lane == 1, i2 - N_GROUPS, 0))
    gates_ref[...] = jnp.where(lane == 0, g1, jnp.where(lane == 1, g2, 0.0))


def _router(x, g_ffn, w_r_pad, b_r_pad):
    n, d = x.shape
    tm = ROW_TILE
    row = lambda i: (i, 0)
    fixed = lambda i: (0, 0)
    return pl.pallas_call(
        _router_kernel,
        grid=(n // tm,),
        in_specs=[pl.BlockSpec((tm, d), row), pl.BlockSpec((1, d), fixed),
                  pl.BlockSpec((d, LANES), fixed), pl.BlockSpec((1, LANES), fixed)],
        out_specs=[pl.BlockSpec((tm, LANES), row), pl.BlockSpec((tm, LANES), row)],
        out_shape=[jax.ShapeDtypeStruct((n, LANES), jnp.int32), jax.ShapeDtypeStruct((n, LANES), F32)],
        compiler_params=_params("parallel"),
        name="router",
    )(x, g_ffn.reshape(1, d), w_r_pad, b_r_pad)


def _row_copy(src_hbm, row, dst_vmem, r, sem):
    return pltpu.make_async_copy(src_hbm.at[pl.ds(row, 1)], dst_vmem.at[pl.ds(r, 1)], sem)


def _ffn_kernel(tile0_ref, tile_e_ref, start_ref, order_ref, x_hbm, g_ref, wg_ref, wu_ref, wd_ref,
                y_hbm, pos_ref, xbuf, ybuf, wg_b, wu_b, wd_b, gsem, osem):
    e = pl.program_id(0)
    tm = ybuf.shape[1]
    t_lo, t_hi, n_used = tile0_ref[e], tile0_ref[e + 1], tile0_ref[N_EXPERTS]
    nbuf = xbuf.shape[0]
    ahead = nbuf - 1

    def gather_start(t, slot, unrolled):
        ee = tile_e_ref[t]
        base = start_ref[ee]
        last = start_ref[ee + 1] - base - 1
        off0 = (t - tile0_ref[ee]) * tm

        def one(r):
            a = order_ref[base + jnp.minimum(off0 + r, last)]
            tok = lax.shift_right_logical(a, TOP_K_SHIFT)
            _row_copy(x_hbm, tok, xbuf.at[slot], r, gsem.at[slot]).start(
                priority=r % 2 if isinstance(r, int) else 1)
            pos_ref[a] = t * tm + r

        if unrolled:
            for r in range(tm):
                one(r)
        else:
            lax.fori_loop(0, tm, lambda r, c: (one(r), c)[1], 0)

    def gather_wait(slot):
        pltpu.make_async_copy(x_hbm.at[pl.ds(0, tm)], xbuf.at[slot], gsem.at[slot]).wait()

    nout = ybuf.shape[0]

    def out_copy(t):
        s = lax.rem(t, nout)
        return pltpu.make_async_copy(ybuf.at[s], y_hbm.at[pl.ds(t * tm, tm)], osem.at[s])

    @pl.when((e == 0) & (n_used > 0))
    def _():
        for k in range(ahead):
            gather_start(jnp.minimum(k, n_used - 1), k, False)

    @pl.when(t_hi > t_lo)
    def _():
        wg_b[...] = wg_ref[0, 0].astype(BF16)
        wu_b[...] = wu_ref[0, 0].astype(BF16)
        wd_b[...] = wd_ref[0, 0].astype(BF16)

        def tile(t, _):
            slot = lax.rem(t, nbuf)
            gather_wait(slot)

            @pl.when(t >= nout)
            def _():
                out_copy(t - nout).wait()

            xn = _rms(xbuf[slot], g_ref[...]).astype(BF16)
            gather_start(jnp.minimum(t + ahead, n_used - 1), lax.rem(t + ahead, nbuf), True)
            hg = jnp.dot(xn, wg_b[...], preferred_element_type=F32)
            hu = jnp.dot(xn, wu_b[...], preferred_element_type=F32)
            hid = (hg / (1.0 + jnp.exp(-hg)) * hu).astype(BF16)
            ybuf[lax.rem(t, nout)] = jnp.dot(hid, wd_b[...], preferred_element_type=F32)
            out_copy(t).start()
            return 0

        lax.fori_loop(t_lo, t_hi, tile, 0)

    @pl.when(e == pl.num_programs(0) - 1)
    def _():
        @pl.when(n_used >= 1)
        def _():
            for k in range(ahead):
                gather_wait(lax.rem(n_used + k, nbuf))

        for back in range(1, nout + 1):
            @pl.when(n_used >= back)
            def _():
                out_copy(n_used - back).wait()

        n_tiles = y_hbm.shape[0] // tm
        ybuf[0] = jnp.zeros(ybuf.shape[1:], F32)

        def zero_copy(t):
            return pltpu.make_async_copy(ybuf.at[0], y_hbm.at[pl.ds(t * tm, tm)], osem.at[0])

        lax.fori_loop(n_used, n_tiles, lambda t, c: (zero_copy(t).start(), c)[1], 0)
        lax.fori_loop(n_used, n_tiles, lambda t, c: (zero_copy(t).wait(), c)[1], 0)


def _expert_ffn(x, g_ffn, w_gate, w_up, w_down, layer, tile0, tile_e, start, order):
    n, d = x.shape
    tm = EXPERT_TILE
    de = w_gate.shape[3]
    fixed = lambda e, *_: (0, 0)
    w_map = lambda e, *_: (layer, e, 0, 0)
    grid_spec = pltpu.PrefetchScalarGridSpec(
        num_scalar_prefetch=4,
        grid=(N_EXPERTS,),
        in_specs=[pl.BlockSpec(memory_space=pl.ANY),
                  pl.BlockSpec((1, d), fixed),
                  pl.BlockSpec((1, 1, d, de), w_map),
                  pl.BlockSpec((1, 1, d, de), w_map),
                  pl.BlockSpec((1, 1, de, d), w_map)],
        out_specs=[pl.BlockSpec(memory_space=pl.ANY), pl.BlockSpec(memory_space=pltpu.SMEM)],
        scratch_shapes=[pltpu.VMEM((GATHER_BUFS, tm, d), F32), pltpu.VMEM((OUT_BUFS, tm, d), F32),
                        pltpu.VMEM((d, de), BF16), pltpu.VMEM((d, de), BF16), pltpu.VMEM((de, d), BF16),
                        pltpu.SemaphoreType.DMA((GATHER_BUFS,)), pltpu.SemaphoreType.DMA((OUT_BUFS,))],
    )
    return pl.pallas_call(
        _ffn_kernel,
        grid_spec=grid_spec,
        out_shape=[jax.ShapeDtypeStruct((tile_e.shape[0] * tm, d), F32),
                   jax.ShapeDtypeStruct(order.shape, jnp.int32)],
        compiler_params=_params("arbitrary"),
        name="expert_ffn",
    )(tile0, tile_e, start, order, x, g_ffn.reshape(1, d), w_gate, w_up, w_down)


def _combine_kernel(pos_ref, y_hbm, gates_ref, x_ref, *rest, split):
    if split:
        go_ref, o_ref, o2_ref, buf0, buf1, sem = rest
    else:
        o_ref, buf0, buf1, sem = rest
    i = pl.program_id(0)
    tc = buf0.shape[1]

    last = pl.num_programs(0) - 1

    def gather_start(t, slot, unrolled):
        def one(r):
            a = TOP_K * (t * tc + r)
            _row_copy(y_hbm, pos_ref[a], buf0.at[slot], r, sem.at[slot]).start(priority=0)
            _row_copy(y_hbm, pos_ref[a + 1], buf1.at[slot], r, sem.at[slot]).start(priority=1)

        if unrolled:
            for r in range(tc):
                one(r)
        else:
            lax.fori_loop(0, tc, lambda r, c: (one(r), c)[1], 0)

    def gather_wait(slot):
        for buf in (buf0, buf1):
            pltpu.make_async_copy(y_hbm.at[pl.ds(0, tc)], buf.at[slot], sem.at[slot]).wait()

    nbuf = buf0.shape[0]
    ahead = nbuf - 1

    @pl.when(i == 0)
    def _():
        for k in range(ahead):
            gather_start(jnp.minimum(k, last), k, False)

    slot = lax.rem(i, nbuf)
    gather_wait(slot)
    gates = gates_ref[...]
    out = x_ref[...] + buf0[slot] * gates[:, 0:1] + buf1[slot] * gates[:, 1:2]
    gather_start(jnp.minimum(i + ahead, last), lax.rem(i + ahead, nbuf), True)
    if split:
        out = _rms(out, go_ref[...])

        @pl.when(i < split)
        def _():
            o_ref[...] = out

        @pl.when(i >= split)
        def _():
            o2_ref[...] = out
    else:
        o_ref[...] = out

    @pl.when(i == last)
    def _():
        for k in range(1, nbuf):
            gather_wait(lax.rem(i + k, nbuf))


def _combine(x, y_slots, gates, pos, g_out=None, n_first=None):
    n, d = x.shape
    tc = COMBINE_TILE
    split = 0 if g_out is None else n_first // tc
    in_specs = [pl.BlockSpec(memory_space=pl.ANY),
                pl.BlockSpec((tc, LANES), lambda i, p: (i, 0)),
                pl.BlockSpec((tc, d), lambda i, p: (i, 0))]
    args = [pos, y_slots, gates, x]
    out_specs = pl.BlockSpec((tc, d), lambda i, p: (i, 0))
    out_shape = jax.ShapeDtypeStruct((n, d), F32)
    if split:
        in_specs.append(pl.BlockSpec((1, d), lambda i, p: (0, 0)))
        args.append(g_out.reshape(1, d))
        out_specs = [pl.BlockSpec((tc, d), lambda i, p: (jnp.minimum(i, split - 1), 0)),
                     pl.BlockSpec((tc, d), lambda i, p: (jnp.maximum(i - split, 0), 0))]
        out_shape = [jax.ShapeDtypeStruct((n_first, d), F32), jax.ShapeDtypeStruct((n - n_first, d), F32)]
    grid_spec = pltpu.PrefetchScalarGridSpec(
        num_scalar_prefetch=1,
        grid=(n // tc,),
        in_specs=in_specs,
        out_specs=out_specs,
        scratch_shapes=[pltpu.VMEM((COMBINE_BUFS, tc, d), F32), pltpu.VMEM((COMBINE_BUFS, tc, d), F32),
                        pltpu.SemaphoreType.DMA((COMBINE_BUFS,))],
    )
    return pl.pallas_call(
        functools.partial(_combine_kernel, split=split),
        grid_spec=grid_spec,
        out_shape=out_shape,
        compiler_params=_params("arbitrary"),
        name="moe_combine",
    )(*args)


def _dispatch_plan(ids):
    n = ids.shape[0]
    a = n * TOP_K
    tm = EXPERT_TILE
    i32 = jnp.int32
    e_flat = ids[:, :TOP_K].reshape(a)
    order = jnp.argsort(e_flat, stable=True).astype(i32)
    experts = jnp.arange(N_EXPERTS, dtype=i32)
    counts = jnp.sum((e_flat[:, None] == experts[None, :]).astype(i32), axis=0)
    zero = jnp.zeros((1,), i32)
    start = jnp.concatenate([zero, jnp.cumsum(counts)]).astype(i32)
    tile_end = jnp.cumsum((counts + tm - 1) // tm)
    tile0 = jnp.concatenate([zero, tile_end]).astype(i32)
    n_tiles = (a + N_EXPERTS * (tm - 1)) // tm
    tile_ids = jnp.arange(n_tiles, dtype=i32)
    tile_e = jnp.minimum(jnp.sum((tile_end[None, :] <= tile_ids[:, None]).astype(i32), axis=1),
                         N_EXPERTS - 1).astype(i32)
    return tile0, tile_e, start, order


def _hier_moe(x, g_ffn, w_r_pad, b_r_pad, w_gate, w_up, w_down, layer, g_out=None, n_first=None):
    ids, gates = _router(x, g_ffn, w_r_pad, b_r_pad)
    tile0, tile_e, start, order = _dispatch_plan(ids)
    y_slots, pos = _expert_ffn(x, g_ffn, w_gate, w_up, w_down, layer, tile0, tile_e, start, order)
    return _combine(x, y_slots, gates, pos, g_out, n_first)


def _rope_table(pos):
    inv = jnp.power(ROPE_THETA, -jnp.arange(ROPE_HALF, dtype=F32) * (2.0 / ROPE_DIM))
    ang = pos[:, None] * inv[None, :]
    return jnp.cos(ang), jnp.sin(ang)


def _pad_cols(w, n):
    return jnp.pad(w, ((0, 0), (0, n - w.shape[1])))


def _q_head_weights(w_uq):
    w = w_uq.reshape(Q_LORA, MLA_HEADS, QK_DIM)
    r1 = w[:, :, NOPE_DIM:NOPE_DIM + ROPE_HALF]
    r2 = w[:, :, NOPE_DIM + ROPE_HALF:]
    return jnp.concatenate([w, -r2, r1], axis=-1).reshape(Q_LORA, MLA_HEADS * QK_PAD).astype(BF16)


def kernel(x_prompt, x_sample, cache_mla_ckv, cache_mla_kpe, state_gla, norm_mix, norm_ffn, norm_out, mla_w_in, mla_g_q_lat, mla_g_kv_lat, mla_w_uq, mla_w_uk, mla_w_uv, mla_g_q, mla_g_k, mla_w_o, gla_w_in, gla_w_a, gla_b_a, gla_g_o, gla_w_o, moe_w_router, moe_b_router, moe_w_gate, moe_w_up, moe_w_down):
    batch, seq, d = x_prompt.shape
    nb, t_new, _ = x_sample.shape
    past = cache_mla_ckv.shape[2]
    n_p, n_s = batch * seq, nb * t_new
    x_p, x_s = x_prompt.reshape(n_p, d), x_sample.reshape(n_s, d)

    pos_rows = jnp.concatenate([jnp.tile(jnp.arange(seq, dtype=F32), batch),
                                jnp.tile(past + jnp.arange(t_new, dtype=F32), nb)])
    cos_r, sin_r = _rope_table(pos_rows)
    tab = jnp.concatenate([cos_r, cos_r, sin_r, sin_r], axis=1)
    w_in_pad = _pad_cols(mla_w_in[0], 9 * LANES).astype(BF16)
    c_q, c_kv, c_kv_b, k_pe = _mla_in(x_p, x_s, norm_mix[0], w_in_pad, mla_g_q_lat[0], mla_g_kv_lat[0])
    q = _mla_q(c_q, _q_head_weights(mla_w_uq[0]), mla_g_q[0], tab)

    w_uk, w_uv = mla_w_uk[0], mla_w_uv[0]
    w_kv_heads = jnp.concatenate([w_uk, w_uv], axis=-1).reshape(KV_LORA, -1).astype(BF16)
    k_p, v_p = _mla_kv(c_kv_b, k_pe, w_kv_heads, mla_g_k[0], tab, n_p)
    o_p = _flash_prompt(q, k_p, v_p, batch, seq)

    n_keys = past + t_new
    kp = (n_keys + LANES - 1) // LANES * LANES
    kpe_new = k_pe[n_p:].reshape(nb, t_new, ROPE_DIM)
    kpe_all = jnp.concatenate([cache_mla_kpe[0], kpe_new,
                               jnp.zeros((nb, kp - n_keys, ROPE_DIM), F32)], axis=1)
    kpet_all = kpe_all.transpose(0, 2, 1)
    cos_k, sin_k = _rope_table(jnp.arange(kp, dtype=F32))
    cost = jnp.concatenate([cos_k, cos_k], axis=1).T
    sint = jnp.concatenate([sin_k, sin_k], axis=1).T
    w_ukt = w_uk.reshape(KV_LORA, -1).T.astype(BF16)
    o_lat = _sample_attn(q, cache_mla_ckv, c_kv_b, kpet_all, w_ukt, mla_g_k[0], cost, sint,
                         row_block0=n_p // t_new, n_keys=n_keys, t_new=t_new)
    o_s = _latent_to_values(o_lat, w_uv.reshape(KV_LORA, -1).astype(BF16), t_new)

    x = _mm(o_p, mla_w_o[0].astype(BF16), x2=o_s, res=x_p, res2=x_s, out_dtype=F32,
            tm=MM_ROW_TILE, tn=512)

    w_r_pad = [_pad_cols(moe_w_router[i], LANES).astype(BF16) for i in range(2)]
    b_r_pad = [_pad_cols(moe_b_router[i].reshape(1, -1), LANES) for i in range(2)]
    x = _hier_moe(x, norm_ffn[0], w_r_pad[0], b_r_pad[0], moe_w_gate, moe_w_up, moe_w_down, 0)

    hk, hv = GLA_HEADS * GLA_DK, GLA_HEADS * GLA_DV
    wg = gla_w_in[0]
    w_gla = jnp.concatenate([wg[:, :2 * hk + hv], wg[:, 2 * hk + hv + GATE_RANK:],
                             _pad_cols(wg[:, 2 * hk + hv:2 * hk + hv + GATE_RANK], LANES)],
                            axis=1).astype(BF16)
    z = _mm(x, w_gla, gain=norm_mix[1], out_dtype=BF16, tm=MM_ROW_TILE, tn=7 * LANES)
    w_a_pad = jnp.pad(gla_w_a[0], ((0, LANES - GATE_RANK), (0, 0))).astype(BF16)
    s0_p = jnp.zeros((batch, GLA_HEADS, GLA_DK, GLA_DV), F32)
    og_p, st_p = _gla_scan(z, w_a_pad, gla_b_a[0], gla_g_o[0], s0_p,
                           row0=0, n_streams=batch, t_len=seq, nsub=GLA_SUBCHUNKS)
    og_s, st_s = _gla_scan(z, w_a_pad, gla_b_a[0], gla_g_o[0], state_gla[0],
                           row0=n_p, n_streams=nb, t_len=t_new, nsub=t_new // CHUNK)
    x = _mm(og_p, gla_w_o[0].astype(BF16), x2=og_s, res=x, out_dtype=F32, tm=MM_ROW_TILE, tn=512)
    y_p, y_s = _hier_moe(x, norm_ffn[1], w_r_pad[1], b_r_pad[1], moe_w_gate, moe_w_up, moe_w_down, 1,
                         g_out=norm_out, n_first=n_p)

    return (y_p.reshape(batch, seq, d),
            y_s.reshape(nb, t_new, d),
            c_kv[:n_p].reshape(1, batch, seq, KV_LORA),
            k_pe[:n_p].reshape(1, batch, seq, ROPE_DIM),
            st_p[None],
            c_kv[n_p:].reshape(1, nb, t_new, KV_LORA),
            k_pe[n_p:].reshape(1, nb, t_new, ROPE_DIM),
            st_s[None])
```

```python
import functools

import jax
import jax.numpy as jnp
from jax import lax
from jax.experimental import pallas as pl
from jax.experimental.pallas import tpu as pltpu

F32 = jnp.float32
BF16 = jnp.bfloat16

LANES = 128
MXU_WIDTH = 256
V7X_VMEM_BYTES = 64 * 1024 * 1024
VMEM_LIMIT = V7X_VMEM_BYTES * 7 // 8

EPS = 1e-6
CHUNK = 64
MLA_HEADS = 16
NOPE_DIM = 128
ROPE_DIM = 64
ROPE_HALF = ROPE_DIM // 2
QK_DIM = NOPE_DIM + ROPE_DIM
QK_PAD = 2 * LANES
V_DIM = 128
Q_LORA = 512
KV_LORA = 512
ROPE_THETA = 10000.0
ATTN_SCALE = QK_DIM ** -0.5
GLA_HEADS = 4
GLA_DK = 256
GLA_DV = 512
GATE_RANK = 16
GATE_TAU = 16.0
N_GROUPS = 8
EXPERTS_PER_GROUP = 8
N_EXPERTS = N_GROUPS * EXPERTS_PER_GROUP
TOP_K = 2
TOP_K_SHIFT = 1
assert 1 << TOP_K_SHIFT == TOP_K

ROW_TILE = 512
MM_ROW_TILE = 1024
FLASH_TQ = 512
FLASH_TK = 512
FLASH_HEADS = 4
SAMPLE_HEAD_GROUP = 4
GLA_SUBCHUNKS = 4
EXPERT_TILE = 176
COMBINE_TILE = 128
GATHER_BUFS = 6
COMBINE_BUFS = 3
OUT_BUFS = 3

NT_DIMS = (((1,), (1,)), ((), ()))
TN_DIMS = (((0,), (0,)), ((), ()))


def _params(*sem):
    return pltpu.CompilerParams(dimension_semantics=sem, vmem_limit_bytes=VMEM_LIMIT)


def _rms(x, g):
    return x * lax.rsqrt(jnp.mean(x * x, axis=-1, keepdims=True) + EPS) * g


def _mm_kernel(*refs, has_norm, has_res, split, res_split):
    it = iter(refs)
    x_ref = next(it)
    x2_ref = next(it) if split else None
    g_ref = next(it) if has_norm else None
    w_ref = next(it)
    r_ref = next(it) if has_res else None
    r2_ref = next(it) if res_split else r_ref
    o_ref = next(it)
    if has_norm:
        xn_ref = next(it)

        @pl.when(pl.program_id(1) == 0)
        def _():
            xn_ref[...] = _rms(x_ref[...], g_ref[...]).astype(BF16)

        x_ref = xn_ref

    def emit(src_ref, res_ref):
        acc = jnp.dot(src_ref[...], w_ref[...], preferred_element_type=F32)
        if has_res:
            acc = acc + res_ref[...]
        o_ref[...] = acc.astype(o_ref.dtype)

    if split:
        pl.when(pl.program_id(0) < split)(lambda: emit(x_ref, r_ref))
        pl.when(pl.program_id(0) >= split)(lambda: emit(x2_ref, r2_ref))
    else:
        emit(x_ref, r_ref)


def _mm(x, w, *, x2=None, gain=None, res=None, res2=None, out_dtype, tm, tn):
    m, k = x.shape
    n = w.shape[1]
    has_norm, has_res = gain is not None, res is not None
    split = 0
    in_specs = [pl.BlockSpec((tm, k), lambda i, j: (i, 0))]
    args = [x]
    if x2 is not None:
        assert not has_norm
        split = m // tm
        m += x2.shape[0]
        first = lambda i: jnp.minimum(i, split - 1)
        second = lambda i: jnp.maximum(i - split, 0)
        in_specs = [pl.BlockSpec((tm, k), lambda i, j: (first(i), 0)),
                    pl.BlockSpec((tm, k), lambda i, j: (second(i), 0))]
        args.append(x2)
    if has_norm:
        in_specs.append(pl.BlockSpec((1, k), lambda i, j: (0, 0)))
        args.append(gain.reshape(1, k))
    in_specs.append(pl.BlockSpec((k, tn), lambda i, j: (0, j)))
    args.append(w)
    res_split = res2 is not None
    if res_split:
        assert split and res.shape[0] == split * tm
        in_specs += [pl.BlockSpec((tm, tn), lambda i, j: (first(i), j)),
                     pl.BlockSpec((tm, tn), lambda i, j: (second(i), j))]
        args += [res, res2]
    elif has_res:
        in_specs.append(pl.BlockSpec((tm, tn), lambda i, j: (i, j)))
        args.append(res)
    return pl.pallas_call(
        functools.partial(_mm_kernel, has_norm=has_norm, has_res=has_res, split=split,
                          res_split=res_split),
        grid=(m // tm, n // tn),
        in_specs=in_specs,
        out_specs=pl.BlockSpec((tm, tn), lambda i, j: (i, j)),
        out_shape=jax.ShapeDtypeStruct((m, n), out_dtype),
        scratch_shapes=[pltpu.VMEM((tm, k), BF16)] if has_norm else [],
        compiler_params=_params("parallel", "arbitrary"),
        name="mm",
    )(*args)


def _mla_in_kernel(x_ref, x2_ref, g_ref, w_ref, gq_ref, gkv_ref, cq_ref, ckv_ref, ckvb_ref, kpe_ref,
                   *, split):
    def emit(src_ref):
        xn = _rms(src_ref[...], g_ref[...]).astype(BF16)
        z = jnp.dot(xn, w_ref[...], preferred_element_type=F32)
        cq_ref[...] = _rms(z[:, :Q_LORA], gq_ref[...]).astype(BF16)
        ckv = _rms(z[:, Q_LORA:Q_LORA + KV_LORA], gkv_ref[...])
        ckv_ref[...] = ckv
        ckvb_ref[...] = ckv.astype(BF16)
        kpe_ref[...] = z[:, Q_LORA + KV_LORA:Q_LORA + KV_LORA + ROPE_DIM]

    pl.when(pl.program_id(0) < split)(lambda: emit(x_ref))
    pl.when(pl.program_id(0) >= split)(lambda: emit(x2_ref))


def _mla_in(x, x2, g_mix, w_in_pad, g_q_lat, g_kv_lat):
    d = x.shape[1]
    tm = ROW_TILE
    split = x.shape[0] // tm
    n = x.shape[0] + x2.shape[0]
    wn = w_in_pad.shape[1]
    row = lambda i: (i, 0)
    fixed = lambda i: (0, 0)
    return pl.pallas_call(
        functools.partial(_mla_in_kernel, split=split),
        grid=(n // tm,),
        in_specs=[pl.BlockSpec((tm, d), lambda i: (jnp.minimum(i, split - 1), 0)),
                  pl.BlockSpec((tm, d), lambda i: (jnp.maximum(i - split, 0), 0)),
                  pl.BlockSpec((1, d), fixed),
                  pl.BlockSpec((d, wn), fixed), pl.BlockSpec((1, Q_LORA), fixed),
                  pl.BlockSpec((1, KV_LORA), fixed)],
        out_specs=[pl.BlockSpec((tm, Q_LORA), row), pl.BlockSpec((tm, KV_LORA), row),
                   pl.BlockSpec((tm, KV_LORA), row), pl.BlockSpec((tm, ROPE_DIM), row)],
        out_shape=[jax.ShapeDtypeStruct((n, Q_LORA), BF16), jax.ShapeDtypeStruct((n, KV_LORA), F32),
                   jax.ShapeDtypeStruct((n, KV_LORA), BF16), jax.ShapeDtypeStruct((n, ROPE_DIM), F32)],
        compiler_params=_params("parallel"),
        name="mla_in",
    )(x, x2, g_mix.reshape(1, d), w_in_pad, g_q_lat.reshape(1, -1), g_kv_lat.reshape(1, -1))


def _mla_q_kernel(cq_ref, w_ref, gn_ref, g2_ref, tab_ref, q_ref):
    cq = cq_ref[...]
    gtab = g2_ref[...] * tab_ref[...]
    is_rope = lax.broadcasted_iota(jnp.int32, gtab.shape, 1) < ROPE_DIM
    for h in range(MLA_HEADS):
        cols = slice(h * QK_PAD, (h + 1) * QK_PAD)
        t = jnp.dot(cq, w_ref[:, cols], preferred_element_type=F32)
        t1 = t[:, :NOPE_DIM]
        t2 = t[:, NOPE_DIM:]
        ss = (jnp.sum(t1 * t1, axis=-1, keepdims=True)
              + jnp.sum(jnp.where(is_rope, t2 * t2, 0.0), axis=-1, keepdims=True))
        rs = lax.rsqrt(ss * (1.0 / QK_DIM) + EPS) * ATTN_SCALE
        u = t2 * gtab
        u = jnp.where(is_rope, u + pltpu.roll(u, ROPE_DIM, axis=1), 0.0)
        q_ref[:, cols] = jnp.concatenate([t1 * gn_ref[...] * rs, u * rs], axis=1).astype(BF16)


def _mla_q(cq, w_q_heads, g_q, tab):
    n = cq.shape[0]
    tm = ROW_TILE
    g1, g2 = g_q[NOPE_DIM:NOPE_DIM + ROPE_HALF], g_q[NOPE_DIM + ROPE_HALF:]
    g_rope = jnp.concatenate([g1, g2, g2, g1]).reshape(1, LANES)
    row = lambda i: (i, 0)
    fixed = lambda i: (0, 0)
    return pl.pallas_call(
        _mla_q_kernel,
        grid=(n // tm,),
        in_specs=[pl.BlockSpec((tm, Q_LORA), row),
                  pl.BlockSpec((Q_LORA, MLA_HEADS * QK_PAD), fixed),
                  pl.BlockSpec((1, NOPE_DIM), fixed),
                  pl.BlockSpec((1, LANES), fixed),
                  pl.BlockSpec((tm, LANES), row)],
        out_specs=pl.BlockSpec((tm, MLA_HEADS * QK_PAD), row),
        out_shape=jax.ShapeDtypeStruct((n, MLA_HEADS * QK_PAD), BF16),
        compiler_params=_params("parallel"),
        name="mla_q",
    )(cq, w_q_heads, g_q[:NOPE_DIM].reshape(1, NOPE_DIM), g_rope, tab)


def _rotate_half_rows(x):
    return jnp.concatenate([-x[ROPE_HALF:], x[:ROPE_HALF]], axis=0)


def _mla_kv_kernel(c_ref, kpe_ref, w_ref, gn_ref, gr_ref, tab_ref, k_ref, v_ref):
    c = c_ref[...]
    kpe = kpe_ref[...]
    sspe = jnp.sum(kpe * kpe, axis=-1, keepdims=True)
    kg = kpe * gr_ref[...]
    rot = jnp.concatenate([-kg[:, ROPE_HALF:], kg[:, :ROPE_HALF]], axis=1)
    tab = tab_ref[...]
    kr = kg * tab[:, :ROPE_DIM] + rot * tab[:, ROPE_DIM:]
    kr = jnp.concatenate([kr, jnp.zeros_like(kr)], axis=1)
    for h in range(MLA_HEADS):
        t = jnp.dot(c, w_ref[:, h * QK_PAD:(h + 1) * QK_PAD], preferred_element_type=F32)
        kn = t[:, :NOPE_DIM]
        rs = lax.rsqrt((jnp.sum(kn * kn, axis=-1, keepdims=True) + sspe) * (1.0 / QK_DIM) + EPS)
        k_ref[:, h * QK_PAD:(h + 1) * QK_PAD] = jnp.concatenate(
            [kn * gn_ref[...] * rs, kr * rs], axis=1).astype(BF16)
        v_ref[:, h * V_DIM:(h + 1) * V_DIM] = t[:, NOPE_DIM:].astype(BF16)


def _mla_kv(ckv_b, kpe, w_kv_heads, g_k, tab, n_rows):
    tm = ROW_TILE
    row = lambda i: (i, 0)
    fixed = lambda i: (0, 0)
    return pl.pallas_call(
        _mla_kv_kernel,
        grid=(n_rows // tm,),
        in_specs=[pl.BlockSpec((tm, KV_LORA), row),
                  pl.BlockSpec((tm, ROPE_DIM), row),
                  pl.BlockSpec((KV_LORA, MLA_HEADS * (NOPE_DIM + V_DIM)), fixed),
                  pl.BlockSpec((1, NOPE_DIM), fixed),
                  pl.BlockSpec((1, ROPE_DIM), fixed),
                  pl.BlockSpec((tm, LANES), row)],
        out_specs=[pl.BlockSpec((tm, MLA_HEADS * QK_PAD), row),
                   pl.BlockSpec((tm, MLA_HEADS * V_DIM), row)],
        out_shape=[jax.ShapeDtypeStruct((n_rows, MLA_HEADS * QK_PAD), BF16),
                   jax.ShapeDtypeStruct((n_rows, MLA_HEADS * V_DIM), BF16)],
        compiler_params=_params("parallel"),
        name="mla_kv",
    )(ckv_b, kpe, w_kv_heads, g_k[:NOPE_DIM].reshape(1, NOPE_DIM),
      g_k[NOPE_DIM:].reshape(1, ROPE_DIM), tab)


def _flash_kernel(q_ref, k_ref, v_ref, o_ref, *, tq, tk):
    qi = pl.program_id(2)

    def step(j, carries, masked):
        ks = pl.multiple_of(j * tk, tk)
        out = []
        for hh, (m, l, acc) in enumerate(carries):
            qk = slice(hh * QK_PAD, (hh + 1) * QK_PAD)
            vv = slice(hh * V_DIM, (hh + 1) * V_DIM)
            s = lax.dot_general(q_ref[:, qk], k_ref[pl.ds(ks, tk), qk], NT_DIMS,
                                preferred_element_type=F32)
            if masked:
                row = lax.broadcasted_iota(jnp.int32, s.shape, 0) // CHUNK
                col = lax.broadcasted_iota(jnp.int32, s.shape, 1) // CHUNK
                s = jnp.where(col <= row, s, -jnp.inf)
            m_new = jnp.maximum(m, jnp.max(s, axis=-1, keepdims=True))
            p = jnp.exp(s - m_new)
            alpha = jnp.exp(m - m_new)
            l = alpha * l + jnp.sum(p, axis=-1, keepdims=True)
            acc = alpha * acc + jnp.dot(p.astype(BF16), v_ref[pl.ds(ks, tk), vv],
                                        preferred_element_type=F32)
            out.append((m_new, l, acc))
        return tuple(out)

    init = tuple((jnp.full((tq, 1), -jnp.inf, F32), jnp.zeros((tq, 1), F32),
                  jnp.zeros((tq, V_DIM), F32)) for _ in range(FLASH_HEADS))
    carries = lax.fori_loop(0, qi, lambda j, c: step(j, c, False), init)
    carries = step(qi, carries, True)
    for hh, (_, l, acc) in enumerate(carries):
        o_ref[:, hh * V_DIM:(hh + 1) * V_DIM] = (acc / l).astype(BF16)


def _flash_prompt(q, k, v, batch, seq):
    tq, tk, hs = FLASH_TQ, FLASH_TK, FLASH_HEADS
    assert tq == tk
    nq = seq // tq
    return pl.pallas_call(
        functools.partial(_flash_kernel, tq=tq, tk=tk),
        grid=(batch, MLA_HEADS // hs, nq),
        in_specs=[pl.BlockSpec((tq, hs * QK_PAD), lambda b, h, i: (b * nq + i, h)),
                  pl.BlockSpec((seq, hs * QK_PAD), lambda b, h, i: (b, h)),
                  pl.BlockSpec((seq, hs * V_DIM), lambda b, h, i: (b, h))],
        out_specs=pl.BlockSpec((tq, hs * V_DIM), lambda b, h, i: (b * nq + i, h)),
        out_shape=jax.ShapeDtypeStruct((batch * seq, MLA_HEADS * V_DIM), BF16),
        compiler_params=_params("parallel", "parallel", "arbitrary"),
        name="flash_prompt",
    )(q, k, v)


def _sample_attn_kernel(q_ref, cache_ref, cnew_ref, kpet_ref, wukt_ref, gkn_ref, gkrt_ref, cost_ref,
                        sint_ref, olat_ref, c_sc, qabs_ref, qr_ref, *, n_keys, t_new):
    kp = c_sc.shape[0]
    past = n_keys - t_new
    c_sc[:past, :] = cache_ref[0, 0].astype(BF16)
    c_sc[past:n_keys, :] = cnew_ref[...]
    c_sc[n_keys:, :] = jnp.zeros((kp - n_keys, KV_LORA), BF16)
    c = c_sc[...]
    kpet = kpet_ref[0]
    sspe = jnp.sum(kpet * kpet, axis=0, keepdims=True)
    kg = kpet * gkrt_ref[...]
    krt = kg * cost_ref[...] + _rotate_half_rows(kg) * sint_ref[...]
    krt = jnp.concatenate([krt, jnp.zeros_like(krt)], axis=0).astype(BF16)
    gkn = gkn_ref[...]
    for h in range(MLA_HEADS):
        qh = q_ref[:, h * QK_PAD:(h + 1) * QK_PAD]
        qn = (qh[:, :NOPE_DIM].astype(F32) * gkn).astype(BF16)
        qa = jnp.dot(qn, wukt_ref[h * NOPE_DIM:(h + 1) * NOPE_DIM, :], preferred_element_type=F32)
        qabs_ref[h * t_new:(h + 1) * t_new, :] = qa.astype(BF16)
        qr_ref[h * t_new:(h + 1) * t_new, :] = qh[:, NOPE_DIM:]
    valid = lax.broadcasted_iota(jnp.int32, (t_new, kp), 1) < n_keys
    hg = SAMPLE_HEAD_GROUP
    for g in range(MLA_HEADS // hg):
        knt = lax.dot_general(wukt_ref[g * hg * NOPE_DIM:(g + 1) * hg * NOPE_DIM, :], c, NT_DIMS,
                              preferred_element_type=F32)
        ss = jnp.sum((knt * knt).reshape(hg, NOPE_DIM, kp), axis=1) + sspe
        rst = lax.rsqrt(ss * (1.0 / QK_DIM) + EPS)
        rows = slice(g * hg * t_new, (g + 1) * hg * t_new)
        s = (lax.dot_general(qabs_ref[rows, :], c, NT_DIMS, preferred_element_type=F32)
             + jnp.dot(qr_ref[rows, :], krt, preferred_element_type=F32))
        ps = []
        for hh in range(hg):
            sh = s[hh * t_new:(hh + 1) * t_new] * rst[hh:hh + 1]
            sh = jnp.where(valid, sh, -jnp.inf)
            e = jnp.exp(sh - jnp.max(sh, axis=-1, keepdims=True))
            ps.append((e / jnp.sum(e, axis=-1, keepdims=True)).astype(BF16))
        p = jnp.concatenate(ps, axis=0)
        olat_ref[0, rows, :] = jnp.dot(p, c, preferred_element_type=F32).astype(BF16)


def _sample_attn(q, cache_ckv, c_rows, kpet_all, w_ukt, g_k, cost, sint, *, row_block0, n_keys, t_new):
    nb, past = cache_ckv.shape[1], cache_ckv.shape[2]
    kp = kpet_all.shape[2]
    assert past + t_new == n_keys
    fixed = lambda b: (0, 0)
    return pl.pallas_call(
        functools.partial(_sample_attn_kernel, n_keys=n_keys, t_new=t_new),
        grid=(nb,),
        in_specs=[pl.BlockSpec((t_new, MLA_HEADS * QK_PAD), lambda b: (row_block0 + b, 0)),
                  pl.BlockSpec((1, 1, past, KV_LORA), lambda b: (0, b, 0, 0)),
                  pl.BlockSpec((t_new, KV_LORA), lambda b: (row_block0 + b, 0)),
                  pl.BlockSpec((1, ROPE_DIM, kp), lambda b: (b, 0, 0)),
                  pl.BlockSpec((MLA_HEADS * NOPE_DIM, KV_LORA), fixed),
                  pl.BlockSpec((1, NOPE_DIM), fixed),
                  pl.BlockSpec((ROPE_DIM, 1), fixed),
                  pl.BlockSpec((ROPE_DIM, kp), fixed),
                  pl.BlockSpec((ROPE_DIM, kp), fixed)],
        out_specs=pl.BlockSpec((1, MLA_HEADS * t_new, KV_LORA), lambda b: (b, 0, 0)),
        out_shape=jax.ShapeDtypeStruct((nb, MLA_HEADS * t_new, KV_LORA), BF16),
        scratch_shapes=[pltpu.VMEM((kp, KV_LORA), BF16),
                        pltpu.VMEM((MLA_HEADS * t_new, KV_LORA), BF16),
                        pltpu.VMEM((MLA_HEADS * t_new, LANES), BF16)],
        compiler_params=_params("parallel"),
        name="sample_attn",
    )(q, cache_ckv, c_rows, kpet_all, w_ukt, g_k[:NOPE_DIM].reshape(1, NOPE_DIM),
      g_k[NOPE_DIM:].reshape(ROPE_DIM, 1), cost, sint)


def _head_mm_kernel(x_ref, w_ref, o_ref):
    nb, t, r = x_ref.shape
    o_ref[...] = jnp.dot(x_ref[...].reshape(nb * t, r), w_ref[...],
                         preferred_element_type=F32).astype(o_ref.dtype)


def _latent_to_values(o_lat, w_uv2d, t_new):
    nb = o_lat.shape[0]
    return pl.pallas_call(
        _head_mm_kernel,
        grid=(MLA_HEADS,),
        in_specs=[pl.BlockSpec((nb, t_new, KV_LORA), lambda h: (0, h, 0)),
                  pl.BlockSpec((KV_LORA, V_DIM), lambda h: (0, h))],
        out_specs=pl.BlockSpec((nb * t_new, V_DIM), lambda h: (0, h)),
        out_shape=jax.ShapeDtypeStruct((nb * t_new, MLA_HEADS * V_DIM), BF16),
        compiler_params=_params("parallel"),
        name="latent_to_values",
    )(o_lat, w_uv2d)


def _gla_kernel(q_ref, k_ref, v_ref, gate_ref, a_ref, wa_ref, ba_ref, go_ref, s0_ref,
                o_ref, sout_ref, st_ref, *, nsub):
    c = pl.program_id(1)

    @pl.when(c == 0)
    def _():
        for h in range(GLA_HEADS):
            st_ref[h] = s0_ref[0, h].T

    tril = (lax.broadcasted_iota(jnp.int32, (CHUNK, CHUNK), 0)
            >= lax.broadcasted_iota(jnp.int32, (CHUNK, CHUNK), 1))
    tril_b = jnp.where(tril, 1.0, 0.0).astype(BF16)
    for j in range(nsub):
        sl = slice(j * CHUNK, (j + 1) * CHUNK)
        x = jnp.dot(a_ref[sl, :], wa_ref[...], preferred_element_type=F32) + ba_ref[...]
        la = (jnp.minimum(x, 0.0) - jnp.log(1.0 + jnp.exp(-jnp.abs(x)))) * (1.0 / GATE_TAU)
        la_hi = la.astype(BF16)
        la_lo = (la - la_hi.astype(F32)).astype(BF16)
        b_all = (jnp.dot(tril_b, la_hi, preferred_element_type=F32)
                 + jnp.dot(tril_b, la_lo, preferred_element_type=F32))
        for h in range(GLA_HEADS):
            kc = slice(h * GLA_DK, (h + 1) * GLA_DK)
            vc = slice(h * GLA_DV, (h + 1) * GLA_DV)
            b = b_all[:, kc]
            b_last = b[CHUNK - 1:CHUNK, :]
            q = q_ref[sl, kc].astype(F32) * (GLA_DK ** -0.5)
            k = k_ref[sl, kc].astype(F32)
            v = v_ref[sl, vc]
            qs = (q * jnp.exp(b)).astype(BF16)
            ks = (k * jnp.exp(-b)).astype(BF16)
            att = lax.dot_general(qs, ks, NT_DIMS, preferred_element_type=F32)
            att = jnp.where(tril, att, 0.0).astype(BF16)
            st = st_ref[h]
            o = (jnp.dot(att, v, preferred_element_type=F32)
                 + lax.dot_general(qs, st.astype(BF16), NT_DIMS, preferred_element_type=F32))
            kd = (k * jnp.exp(b_last - b)).astype(BF16)
            st_ref[h] = st * jnp.exp(b_last) + lax.dot_general(v, kd, TN_DIMS,
                                                               preferred_element_type=F32)
            gt = gate_ref[sl, vc].astype(F32)
            o_ref[sl, vc] = (_rms(o, go_ref[...]) * (gt / (1.0 + jnp.exp(-gt)))).astype(BF16)

    @pl.when(c == pl.num_programs(1) - 1)
    def _():
        for h in range(GLA_HEADS):
            sout_ref[0, h] = st_ref[h].T


def _gla_scan(z, w_a_pad, b_a, g_o, s0, *, row0, n_streams, t_len, nsub):
    tc = nsub * CHUNK
    nc = t_len // tc
    rb0 = row0 // tc
    rows = lambda b, c: rb0 + b * nc + c
    hk, hv = GLA_HEADS * GLA_DK, GLA_HEADS * GLA_DV
    a_blk = (2 * hk + 2 * hv) // LANES
    fixed = lambda b, c: (0, 0)
    state = lambda b, c: (b, 0, 0, 0)
    return pl.pallas_call(
        functools.partial(_gla_kernel, nsub=nsub),
        grid=(n_streams, nc),
        in_specs=[pl.BlockSpec((tc, hk), lambda b, c: (rows(b, c), 0)),
                  pl.BlockSpec((tc, hk), lambda b, c: (rows(b, c), 1)),
                  pl.BlockSpec((tc, hv), lambda b, c: (rows(b, c), 2 * hk // hv)),
                  pl.BlockSpec((tc, hv), lambda b, c: (rows(b, c), 2 * hk // hv + 1)),
                  pl.BlockSpec((tc, LANES), lambda b, c: (rows(b, c), a_blk)),
                  pl.BlockSpec((LANES, hk), fixed),
                  pl.BlockSpec((1, hk), fixed),
                  pl.BlockSpec((1, GLA_DV), fixed),
                  pl.BlockSpec((1, GLA_HEADS, GLA_DK, GLA_DV), state)],
        out_specs=[pl.BlockSpec((tc, hv), lambda b, c: (b * nc + c, 0)),
                   pl.BlockSpec((1, GLA_HEADS, GLA_DK, GLA_DV), state)],
        out_shape=[jax.ShapeDtypeStruct((n_streams * t_len, hv), BF16),
                   jax.ShapeDtypeStruct((n_streams, GLA_HEADS, GLA_DK, GLA_DV), F32)],
        scratch_shapes=[pltpu.VMEM((GLA_HEADS, GLA_DV, GLA_DK), F32)],
        compiler_params=_params("parallel", "arbitrary"),
        name="gla_scan",
    )(z, z, z, z, z, w_a_pad, b_a.reshape(1, hk), g_o.reshape(1, GLA_DV), s0)


def _router_kernel(x_ref, g_ref, w_ref, b_ref, ids_ref, gates_ref):
    xn = _rms(x_ref[...], g_ref[...]).astype(BF16)
    logits = jnp.dot(xn, w_ref[...], preferred_element_type=F32) + b_ref[...]
    lane = lax.broadcasted_iota(jnp.int32, logits.shape, 1)
    neg = -jnp.inf

    def top(mask):
        vals = jnp.where(mask, logits, neg)
        m = jnp.max(vals, axis=-1, keepdims=True)
        idx = jnp.min(jnp.where(vals == m, lane, LANES), axis=-1, keepdims=True)
        return m, idx

    is_grp = lane < N_GROUPS
    m_g, grp = top(is_grp)
    p_grp = 1.0 / jnp.sum(jnp.where(is_grp, jnp.exp(logits - m_g), 0.0), axis=-1, keepdims=True)
    lo = N_GROUPS + grp * EXPERTS_PER_GROUP
    in_grp = (lane >= lo) & (lane < lo + EXPERTS_PER_GROUP)
    m1, i1 = top(in_grp)
    m2, i2 = top(in_grp & (lane != i1))
    e2 = jnp.exp(m2 - m1)
    g1 = p_grp / (1.0 + e2)
    g2 = p_grp * e2 / (1.0 + e2)
    ids_ref[...] = jnp.where(lane == 0, i1 - N_GROUPS, jnp.where(lane == 1, i2 - N_GROUPS, 0))
    gates_ref[...] = jnp.where(lane == 0, g1, jnp.where(lane == 1, g2, 0.0))


def _router(x, g_ffn, w_r_pad, b_r_pad):
    n, d = x.shape
    tm = ROW_TILE
    row = lambda i: (i, 0)
    fixed = lambda i: (0, 0)
    return pl.pallas_call(
        _router_kernel,
        grid=(n // tm,),
        in_specs=[pl.BlockSpec((tm, d), row), pl.BlockSpec((1, d), fixed),
                  pl.BlockSpec((d, LANES), fixed), pl.BlockSpec((1, LANES), fixed)],
        out_specs=[pl.BlockSpec((tm, LANES), row), pl.BlockSpec((tm, LANES), row)],
        out_shape=[jax.ShapeDtypeStruct((n, LANES), jnp.int32), jax.ShapeDtypeStruct((n, LANES), F32)],
        compiler_params=_params("parallel"),
        name="router",
    )(x, g_ffn.reshape(1, d), w_r_pad, b_r_pad)


def _row_copy(src_hbm, row, dst_vmem, r, sem):
    return pltpu.make_async_copy(src_hbm.at[pl.ds(row, 1)], dst_vmem.at[pl.ds(r, 1)], sem)


def _ffn_kernel(tile0_ref, tile_e_ref, start_ref, order_ref, x_hbm, g_ref, wg_ref, wu_ref, wd_ref,
                y_hbm, pos_ref, xbuf, ybuf, wg_b, wu_b, wd_b, gsem, osem):
    e = pl.program_id(0)
    tm = ybuf.shape[1]
    t_lo, t_hi, n_used = tile0_ref[e], tile0_ref[e + 1], tile0_ref[N_EXPERTS]
    nbuf = xbuf.shape[0]
    ahead = nbuf - 1

    def gather_start(t, slot, unrolled):
        ee = tile_e_ref[t]
        base = start_ref[ee]
        last = start_ref[ee + 1] - base - 1
        off0 = (t - tile0_ref[ee]) * tm

        def one(r):
            a = order_ref[base + jnp.minimum(off0 + r, last)]
            tok = lax.shift_right_logical(a, TOP_K_SHIFT)
            _row_copy(x_hbm, tok, xbuf.at[slot], r, gsem.at[slot]).start(priority=1)
            pos_ref[a] = t * tm + r

        if unrolled:
            for r in range(tm):
                one(r)
        else:
            lax.fori_loop(0, tm, lambda r, c: (one(r), c)[1], 0)

    def gather_wait(slot):
        pltpu.make_async_copy(x_hbm.at[pl.ds(0, tm)], xbuf.at[slot], gsem.at[slot]).wait()

    nout = ybuf.shape[0]

    def out_copy(t):
        s = lax.rem(t, nout)
        return pltpu.make_async_copy(ybuf.at[s], y_hbm.at[pl.ds(t * tm, tm)], osem.at[s])

    @pl.when((e == 0) & (n_used > 0))
    def _():
        for k in range(ahead):
            gather_start(jnp.minimum(k, n_used - 1), k, False)

    @pl.when(t_hi > t_lo)
    def _():
        wg_b[...] = wg_ref[0, 0].astype(BF16)
        wu_b[...] = wu_ref[0, 0].astype(BF16)
        wd_b[...] = wd_ref[0, 0].astype(BF16)

        def tile(t, _):
            slot = lax.rem(t, nbuf)
            gather_wait(slot)

            @pl.when(t >= nout)
            def _():
                out_copy(t - nout).wait()

            xn = _rms(xbuf[slot], g_ref[...]).astype(BF16)
            gather_start(jnp.minimum(t + ahead, n_used - 1), lax.rem(t + ahead, nbuf), True)
            hg = jnp.dot(xn, wg_b[...], preferred_element_type=F32)
            hu = jnp.dot(xn, wu_b[...], preferred_element_type=F32)
            hid = (hg / (1.0 + jnp.exp(-hg)) * hu).astype(BF16)
            ybuf[lax.rem(t, nout)] = jnp.dot(hid, wd_b[...], preferred_element_type=F32)
            out_copy(t).start()
            return 0

        lax.fori_loop(t_lo, t_hi, tile, 0)

    @pl.when(e == pl.num_programs(0) - 1)
    def _():
        @pl.when(n_used >= 1)
        def _():
            for k in range(ahead):
                gather_wait(lax.rem(n_used + k, nbuf))

        for back in range(1, nout + 1):
            @pl.when(n_used >= back)
            def _():
                out_copy(n_used - back).wait()

        n_tiles = y_hbm.shape[0] // tm
        ybuf[0] = jnp.zeros(ybuf.shape[1:], F32)

        def zero_copy(t):
            return pltpu.make_async_copy(ybuf.at[0], y_hbm.at[pl.ds(t * tm, tm)], osem.at[0])

        lax.fori_loop(n_used, n_tiles, lambda t, c: (zero_copy(t).start(), c)[1], 0)
        lax.fori_loop(n_used, n_tiles, lambda t, c: (zero_copy(t).wait(), c)[1], 0)


def _expert_ffn(x, g_ffn, w_gate, w_up, w_down, layer, tile0, tile_e, start, order):
    n, d = x.shape
    tm = EXPERT_TILE
    de = w_gate.shape[3]
    fixed = lambda e, *_: (0, 0)
    w_map = lambda e, *_: (layer, e, 0, 0)
    grid_spec = pltpu.PrefetchScalarGridSpec(
        num_scalar_prefetch=4,
        grid=(N_EXPERTS,),
        in_specs=[pl.BlockSpec(memory_space=pl.ANY),
                  pl.BlockSpec((1, d), fixed),
                  pl.BlockSpec((1, 1, d, de), w_map),
                  pl.BlockSpec((1, 1, d, de), w_map),
                  pl.BlockSpec((1, 1, de, d), w_map)],
        out_specs=[pl.BlockSpec(memory_space=pl.ANY), pl.BlockSpec(memory_space=pltpu.SMEM)],
        scratch_shapes=[pltpu.VMEM((GATHER_BUFS, tm, d), F32), pltpu.VMEM((OUT_BUFS, tm, d), F32),
                        pltpu.VMEM((d, de), BF16), pltpu.VMEM((d, de), BF16), pltpu.VMEM((de, d), BF16),
                        pltpu.SemaphoreType.DMA((GATHER_BUFS,)), pltpu.SemaphoreType.DMA((OUT_BUFS,))],
    )
    return pl.pallas_call(
        _ffn_kernel,
        grid_spec=grid_spec,
        out_shape=[jax.ShapeDtypeStruct((tile_e.shape[0] * tm, d), F32),
                   jax.ShapeDtypeStruct(order.shape, jnp.int32)],
        compiler_params=_params("arbitrary"),
        name="expert_ffn",
    )(tile0, tile_e, start, order, x, g_ffn.reshape(1, d), w_gate, w_up, w_down)


def _combine_kernel(pos_ref, y_hbm, gates_ref, x_ref, *rest, split):
    if split:
        go_ref, o_ref, o2_ref, buf0, buf1, sem = rest
    else:
        o_ref, buf0, buf1, sem = rest
    i = pl.program_id(0)
    tc = buf0.shape[1]

    last = pl.num_programs(0) - 1

    def gather_start(t, slot, unrolled):
        def one(r):
            a = TOP_K * (t * tc + r)
            _row_copy(y_hbm, pos_ref[a], buf0.at[slot], r, sem.at[slot]).start(priority=0)
            _row_copy(y_hbm, pos_ref[a + 1], buf1.at[slot], r, sem.at[slot]).start(priority=1)

        if unrolled:
            for r in range(tc):
                one(r)
        else:
            lax.fori_loop(0, tc, lambda r, c: (one(r), c)[1], 0)

    def gather_wait(slot):
        for buf in (buf0, buf1):
            pltpu.make_async_copy(y_hbm.at[pl.ds(0, tc)], buf.at[slot], sem.at[slot]).wait()

    nbuf = buf0.shape[0]
    ahead = nbuf - 1

    @pl.when(i == 0)
    def _():
        for k in range(ahead):
            gather_start(jnp.minimum(k, last), k, False)

    slot = lax.rem(i, nbuf)
    gather_wait(slot)
    gates = gates_ref[...]
    out = x_ref[...] + buf0[slot] * gates[:, 0:1] + buf1[slot] * gates[:, 1:2]
    gather_start(jnp.minimum(i + ahead, last), lax.rem(i + ahead, nbuf), True)
    if split:
        out = _rms(out, go_ref[...])

        @pl.when(i < split)
        def _():
            o_ref[...] = out

        @pl.when(i >= split)
        def _():
            o2_ref[...] = out
    else:
        o_ref[...] = out

    @pl.when(i == last)
    def _():
        for k in range(1, nbuf):
            gather_wait(lax.rem(i + k, nbuf))


def _combine(x, y_slots, gates, pos, g_out=None, n_first=None):
    n, d = x.shape
    tc = COMBINE_TILE
    split = 0 if g_out is None else n_first // tc
    in_specs = [pl.BlockSpec(memory_space=pl.ANY),
                pl.BlockSpec((tc, LANES), lambda i, p: (i, 0)),
                pl.BlockSpec((tc, d), lambda i, p: (i, 0))]
    args = [pos, y_slots, gates, x]
    out_specs = pl.BlockSpec((tc, d), lambda i, p: (i, 0))
    out_shape = jax.ShapeDtypeStruct((n, d), F32)
    if split:
        in_specs.append(pl.BlockSpec((1, d), lambda i, p: (0, 0)))
        args.append(g_out.reshape(1, d))
        out_specs = [pl.BlockSpec((tc, d), lambda i, p: (jnp.minimum(i, split - 1), 0)),
                     pl.BlockSpec((tc, d), lambda i, p: (jnp.maximum(i - split, 0), 0))]
        out_shape = [jax.ShapeDtypeStruct((n_first, d), F32), jax.ShapeDtypeStruct((n - n_first, d), F32)]
    grid_spec = pltpu.PrefetchScalarGridSpec(
        num_scalar_prefetch=1,
        grid=(n // tc,),
        in_specs=in_specs,
        out_specs=out_specs,
        scratch_shapes=[pltpu.VMEM((COMBINE_BUFS, tc, d), F32), pltpu.VMEM((COMBINE_BUFS, tc, d), F32),
                        pltpu.SemaphoreType.DMA((COMBINE_BUFS,))],
    )
    return pl.pallas_call(
        functools.partial(_combine_kernel, split=split),
        grid_spec=grid_spec,
        out_shape=out_shape,
        compiler_params=_params("arbitrary"),
        name="moe_combine",
    )(*args)


def _dispatch_plan(ids):
    n = ids.shape[0]
    a = n * TOP_K
    tm = EXPERT_TILE
    i32 = jnp.int32
    e_flat = ids[:, :TOP_K].reshape(a)
    order = jnp.argsort(e_flat, stable=True).astype(i32)
    experts = jnp.arange(N_EXPERTS, dtype=i32)
    counts = jnp.sum((e_flat[:, None] == experts[None, :]).astype(i32), axis=0)
    zero = jnp.zeros((1,), i32)
    start = jnp.concatenate([zero, jnp.cumsum(counts)]).astype(i32)
    tile_end = jnp.cumsum((counts + tm - 1) // tm)
    tile0 = jnp.concatenate([zero, tile_end]).astype(i32)
    n_tiles = (a + N_EXPERTS * (tm - 1)) // tm
    tile_ids = jnp.arange(n_tiles, dtype=i32)
    tile_e = jnp.minimum(jnp.sum((tile_end[None, :] <= tile_ids[:, None]).astype(i32), axis=1),
                         N_EXPERTS - 1).astype(i32)
    return tile0, tile_e, start, order


def _hier_moe(x, g_ffn, w_r_pad, b_r_pad, w_gate, w_up, w_down, layer, g_out=None, n_first=None):
    ids, gates = _router(x, g_ffn, w_r_pad, b_r_pad)
    tile0, tile_e, start, order = _dispatch_plan(ids)
    y_slots, pos = _expert_ffn(x, g_ffn, w_gate, w_up, w_down, layer, tile0, tile_e, start, order)
    return _combine(x, y_slots, gates, pos, g_out, n_first)


def _rope_table(pos):
    inv = jnp.power(ROPE_THETA, -jnp.arange(ROPE_HALF, dtype=F32) * (2.0 / ROPE_DIM))
    ang = pos[:, None] * inv[None, :]
    return jnp.cos(ang), jnp.sin(ang)


def _pad_cols(w, n):
    return jnp.pad(w, ((0, 0), (0, n - w.shape[1])))


def _q_head_weights(w_uq):
    w = w_uq.reshape(Q_LORA, MLA_HEADS, QK_DIM)
    r1 = w[:, :, NOPE_DIM:NOPE_DIM + ROPE_HALF]
    r2 = w[:, :, NOPE_DIM + ROPE_HALF:]
    return jnp.concatenate([w, -r2, r1], axis=-1).reshape(Q_LORA, MLA_HEADS * QK_PAD).astype(BF16)


def kernel(x_prompt, x_sample, cache_mla_ckv, cache_mla_kpe, state_gla, norm_mix, norm_ffn, norm_out, mla_w_in, mla_g_q_lat, mla_g_kv_lat, mla_w_uq, mla_w_uk, mla_w_uv, mla_g_q, mla_g_k, mla_w_o, gla_w_in, gla_w_a, gla_b_a, gla_g_o, gla_w_o, moe_w_router, moe_b_router, moe_w_gate, moe_w_up, moe_w_down):
    batch, seq, d = x_prompt.shape
    nb, t_new, _ = x_sample.shape
    past = cache_mla_ckv.shape[2]
    n_p, n_s = batch * seq, nb * t_new
    x_p, x_s = x_prompt.reshape(n_p, d), x_sample.reshape(n_s, d)

    pos_rows = jnp.concatenate([jnp.tile(jnp.arange(seq, dtype=F32), batch),
                                jnp.tile(past + jnp.arange(t_new, dtype=F32), nb)])
    cos_r, sin_r = _rope_table(pos_rows)
    tab = jnp.concatenate([cos_r, cos_r, sin_r, sin_r], axis=1)
    w_in_pad = _pad_cols(mla_w_in[0], 9 * LANES).astype(BF16)
    c_q, c_kv, c_kv_b, k_pe = _mla_in(x_p, x_s, norm_mix[0], w_in_pad, mla_g_q_lat[0], mla_g_kv_lat[0])
    q = _mla_q(c_q, _q_head_weights(mla_w_uq[0]), mla_g_q[0], tab)

    w_uk, w_uv = mla_w_uk[0], mla_w_uv[0]
    w_kv_heads = jnp.concatenate([w_uk, w_uv], axis=-1).reshape(KV_LORA, -1).astype(BF16)
    k_p, v_p = _mla_kv(c_kv_b, k_pe, w_kv_heads, mla_g_k[0], tab, n_p)
    o_p = _flash_prompt(q, k_p, v_p, batch, seq)

    n_keys = past + t_new
    kp = (n_keys + LANES - 1) // LANES * LANES
    kpe_new = k_pe[n_p:].reshape(nb, t_new, ROPE_DIM)
    kpe_all = jnp.concatenate([cache_mla_kpe[0], kpe_new,
                               jnp.zeros((nb, kp - n_keys, ROPE_DIM), F32)], axis=1)
    kpet_all = kpe_all.transpose(0, 2, 1)
    cos_k, sin_k = _rope_table(jnp.arange(kp, dtype=F32))
    cost = jnp.concatenate([cos_k, cos_k], axis=1).T
    sint = jnp.concatenate([sin_k, sin_k], axis=1).T
    w_ukt = w_uk.reshape(KV_LORA, -1).T.astype(BF16)
    o_lat = _sample_attn(q, cache_mla_ckv, c_kv_b, kpet_all, w_ukt, mla_g_k[0], cost, sint,
                         row_block0=n_p // t_new, n_keys=n_keys, t_new=t_new)
    o_s = _latent_to_values(o_lat, w_uv.reshape(KV_LORA, -1).astype(BF16), t_new)

    x = _mm(o_p, mla_w_o[0].astype(BF16), x2=o_s, res=x_p, res2=x_s, out_dtype=F32,
            tm=MM_ROW_TILE, tn=1024)

    w_r_pad = [_pad_cols(moe_w_router[i], LANES).astype(BF16) for i in range(2)]
    b_r_pad = [_pad_cols(moe_b_router[i].reshape(1, -1), LANES) for i in range(2)]
    x = _hier_moe(x, norm_ffn[0], w_r_pad[0], b_r_pad[0], moe_w_gate, moe_w_up, moe_w_down, 0)

    hk, hv = GLA_HEADS * GLA_DK, GLA_HEADS * GLA_DV
    wg = gla_w_in[0]
    w_gla = jnp.concatenate([wg[:, :2 * hk + hv], wg[:, 2 * hk + hv + GATE_RANK:],
                             _pad_cols(wg[:, 2 * hk + hv:2 * hk + hv + GATE_RANK], MXU_WIDTH)],
                            axis=1).astype(BF16)
    z = _mm(x, w_gla, gain=norm_mix[1], out_dtype=BF16, tm=MM_ROW_TILE, tn=5 * MXU_WIDTH)
    w_a_pad = jnp.pad(gla_w_a[0], ((0, LANES - GATE_RANK), (0, 0))).astype(BF16)
    s0_p = jnp.zeros((batch, GLA_HEADS, GLA_DK, GLA_DV), F32)
    og_p, st_p = _gla_scan(z, w_a_pad, gla_b_a[0], gla_g_o[0], s0_p,
                           row0=0, n_streams=batch, t_len=seq, nsub=GLA_SUBCHUNKS)
    og_s, st_s = _gla_scan(z, w_a_pad, gla_b_a[0], gla_g_o[0], state_gla[0],
                           row0=n_p, n_streams=nb, t_len=t_new, nsub=t_new // CHUNK)
    x = _mm(og_p, gla_w_o[0].astype(BF16), x2=og_s, res=x, out_dtype=F32, tm=MM_ROW_TILE, tn=1024)
    y_p, y_s = _hier_moe(x, norm_ffn[1], w_r_pad[1], b_r_pad[1], moe_w_gate, moe_w_up, moe_w_down, 1,
                         g_out=norm_out, n_first=n_p)

    return (y_p.reshape(batch, seq, d),
            y_s.reshape(nb, t_new, d),
            c_kv[:n_p].reshape(1, batch, seq, KV_LORA),
            k_pe[:n_p].reshape(1, batch, seq, ROPE_DIM),
            st_p[None],
            c_kv[n_p:].reshape(1, nb, t_new, KV_LORA),
            k_pe[n_p:].reshape(1, nb, t_new, ROPE_DIM),
            st_s[None])
```

```python
import functools

import jax
import jax.numpy as jnp
from jax import lax
from jax.experimental import pallas as pl
from jax.experimental.pallas import tpu as pltpu

F32 = jnp.float32
BF16 = jnp.bfloat16

LANES = 128
MXU_WIDTH = 256
V7X_VMEM_BYTES = 64 * 1024 * 1024
VMEM_LIMIT = V7X_VMEM_BYTES * 7 // 8

EPS = 1e-6
CHUNK = 64
MLA_HEADS = 16
NOPE_DIM = 128
ROPE_DIM = 64
ROPE_HALF = ROPE_DIM // 2
QK_DIM = NOPE_DIM + ROPE_DIM
QK_PAD = 2 * LANES
V_DIM = 128
Q_LORA = 512
KV_LORA = 512
ROPE_THETA = 10000.0
ATTN_SCALE = QK_DIM ** -0.5
GLA_HEADS = 4
GLA_DK = 256
GLA_DV = 512
GATE_RANK = 16
GATE_TAU = 16.0
N_GROUPS = 8
EXPERTS_PER_GROUP = 8
N_EXPERTS = N_GROUPS * EXPERTS_PER_GROUP
TOP_K = 2
TOP_K_SHIFT = 1
assert 1 << TOP_K_SHIFT == TOP_K

ROW_TILE = 512
MM_ROW_TILE = 1024
FLASH_TQ = 512
FLASH_TK = 512
FLASH_HEADS = 4
SAMPLE_HEAD_GROUP = 4
GLA_SUBCHUNKS = 4
EXPERT_TILE = 176
COMBINE_TILE = 128
GATHER_BUFS = 6
COMBINE_BUFS = 3
OUT_BUFS = 3

NT_DIMS = (((1,), (1,)), ((), ()))
TN_DIMS = (((0,), (0,)), ((), ()))


def _params(*sem):
    return pltpu.CompilerParams(dimension_semantics=sem, vmem_limit_bytes=VMEM_LIMIT)


def _rms(x, g):
    return x * lax.rsqrt(jnp.mean(x * x, axis=-1, keepdims=True) + EPS) * g


def _mm_kernel(*refs, has_norm, has_res, split, res_split):
    it = iter(refs)
    x_ref = next(it)
    x2_ref = next(it) if split else None
    g_ref = next(it) if has_norm else None
    w_ref = next(it)
    r_ref = next(it) if has_res else None
    r2_ref = next(it) if res_split else r_ref
    o_ref = next(it)
    if has_norm:
        xn_ref = next(it)

        @pl.when(pl.program_id(1) == 0)
        def _():
            xn_ref[...] = _rms(x_ref[...], g_ref[...]).astype(BF16)

        x_ref = xn_ref

    def emit(src_ref, res_ref):
        acc = jnp.dot(src_ref[...], w_ref[...], preferred_element_type=F32)
        if has_res:
            acc = acc + res_ref[...]
        o_ref[...] = acc.astype(o_ref.dtype)

    if split:
        pl.when(pl.program_id(0) < split)(lambda: emit(x_ref, r_ref))
        pl.when(pl.program_id(0) >= split)(lambda: emit(x2_ref, r2_ref))
    else:
        emit(x_ref, r_ref)


def _mm(x, w, *, x2=None, gain=None, res=None, res2=None, out_dtype, tm, tn):
    m, k = x.shape
    n = w.shape[1]
    has_norm, has_res = gain is not None, res is not None
    split = 0
    in_specs = [pl.BlockSpec((tm, k), lambda i, j: (i, 0))]
    args = [x]
    if x2 is not None:
        assert not has_norm
        split = m // tm
        m += x2.shape[0]
        first = lambda i: jnp.minimum(i, split - 1)
        second = lambda i: jnp.maximum(i - split, 0)
        in_specs = [pl.BlockSpec((tm, k), lambda i, j: (first(i), 0)),
                    pl.BlockSpec((tm, k), lambda i, j: (second(i), 0))]
        args.append(x2)
    if has_norm:
        in_specs.append(pl.BlockSpec((1, k), lambda i, j: (0, 0)))
        args.append(gain.reshape(1, k))
    in_specs.append(pl.BlockSpec((k, tn), lambda i, j: (0, j)))
    args.append(w)
    res_split = res2 is not None
    if res_split:
        assert split and res.shape[0] == split * tm
        in_specs += [pl.BlockSpec((tm, tn), lambda i, j: (first(i), j)),
                     pl.BlockSpec((tm, tn), lambda i, j: (second(i), j))]
        args += [res, res2]
    elif has_res:
        in_specs.append(pl.BlockSpec((tm, tn), lambda i, j: (i, j)))
        args.append(res)
    return pl.pallas_call(
        functools.partial(_mm_kernel, has_norm=has_norm, has_res=has_res, split=split,
                          res_split=res_split),
        grid=(m // tm, n // tn),
        in_specs=in_specs,
        out_specs=pl.BlockSpec((tm, tn), lambda i, j: (i, j)),
        out_shape=jax.ShapeDtypeStruct((m, n), out_dtype),
        scratch_shapes=[pltpu.VMEM((tm, k), BF16)] if has_norm else [],
        compiler_params=_params("parallel", "arbitrary"),
        name="mm",
    )(*args)


def _mla_in_kernel(x_ref, x2_ref, g_ref, w_ref, gq_ref, gkv_ref, cq_ref, ckvb_ref, ckv_ref, ckv2_ref,
                   kpe_ref, kpe2_ref, *, split):
    def emit(src_ref, ckv_out, kpe_out):
        xn = _rms(src_ref[...], g_ref[...]).astype(BF16)
        z = jnp.dot(xn, w_ref[...], preferred_element_type=F32)
        cq_ref[...] = _rms(z[:, :Q_LORA], gq_ref[...]).astype(BF16)
        ckv = _rms(z[:, Q_LORA:Q_LORA + KV_LORA], gkv_ref[...])
        ckv_out[...] = ckv
        ckvb_ref[...] = ckv.astype(BF16)
        kpe_out[...] = z[:, Q_LORA + KV_LORA:Q_LORA + KV_LORA + ROPE_DIM]

    pl.when(pl.program_id(0) < split)(lambda: emit(x_ref, ckv_ref, kpe_ref))
    pl.when(pl.program_id(0) >= split)(lambda: emit(x2_ref, ckv2_ref, kpe2_ref))


def _mla_in(x, x2, g_mix, w_in_pad, g_q_lat, g_kv_lat):
    d = x.shape[1]
    tm = ROW_TILE
    split = x.shape[0] // tm
    n1, n2 = x.shape[0], x2.shape[0]
    n = n1 + n2
    wn = w_in_pad.shape[1]
    row = lambda i: (i, 0)
    first = lambda i: (jnp.minimum(i, split - 1), 0)
    second = lambda i: (jnp.maximum(i - split, 0), 0)
    fixed = lambda i: (0, 0)
    return pl.pallas_call(
        functools.partial(_mla_in_kernel, split=split),
        grid=(n // tm,),
        in_specs=[pl.BlockSpec((tm, d), first),
                  pl.BlockSpec((tm, d), second),
                  pl.BlockSpec((1, d), fixed),
                  pl.BlockSpec((d, wn), fixed), pl.BlockSpec((1, Q_LORA), fixed),
                  pl.BlockSpec((1, KV_LORA), fixed)],
        out_specs=[pl.BlockSpec((tm, Q_LORA), row), pl.BlockSpec((tm, KV_LORA), row),
                   pl.BlockSpec((tm, KV_LORA), first), pl.BlockSpec((tm, KV_LORA), second),
                   pl.BlockSpec((tm, ROPE_DIM), first), pl.BlockSpec((tm, ROPE_DIM), second)],
        out_shape=[jax.ShapeDtypeStruct((n, Q_LORA), BF16), jax.ShapeDtypeStruct((n, KV_LORA), BF16),
                   jax.ShapeDtypeStruct((n1, KV_LORA), F32), jax.ShapeDtypeStruct((n2, KV_LORA), F32),
                   jax.ShapeDtypeStruct((n1, ROPE_DIM), F32), jax.ShapeDtypeStruct((n2, ROPE_DIM), F32)],
        compiler_params=_params("parallel"),
        name="mla_in",
    )(x, x2, g_mix.reshape(1, d), w_in_pad, g_q_lat.reshape(1, -1), g_kv_lat.reshape(1, -1))


def _mla_q_kernel(cq_ref, w_ref, gn_ref, g2_ref, tab_ref, q_ref):
    cq = cq_ref[...]
    gtab = g2_ref[...] * tab_ref[...]
    is_rope = lax.broadcasted_iota(jnp.int32, gtab.shape, 1) < ROPE_DIM
    for h in range(MLA_HEADS):
        cols = slice(h * QK_PAD, (h + 1) * QK_PAD)
        t = jnp.dot(cq, w_ref[:, cols], preferred_element_type=F32)
        t1 = t[:, :NOPE_DIM]
        t2 = t[:, NOPE_DIM:]
        ss = (jnp.sum(t1 * t1, axis=-1, keepdims=True)
              + jnp.sum(jnp.where(is_rope, t2 * t2, 0.0), axis=-1, keepdims=True))
        rs = lax.rsqrt(ss * (1.0 / QK_DIM) + EPS) * ATTN_SCALE
        u = t2 * gtab
        u = jnp.where(is_rope, u + pltpu.roll(u, ROPE_DIM, axis=1), 0.0)
        q_ref[:, cols] = jnp.concatenate([t1 * gn_ref[...] * rs, u * rs], axis=1).astype(BF16)


def _mla_q(cq, w_q_heads, g_q, tab):
    n = cq.shape[0]
    tm = ROW_TILE
    g1, g2 = g_q[NOPE_DIM:NOPE_DIM + ROPE_HALF], g_q[NOPE_DIM + ROPE_HALF:]
    g_rope = jnp.concatenate([g1, g2, g2, g1]).reshape(1, LANES)
    row = lambda i: (i, 0)
    fixed = lambda i: (0, 0)
    return pl.pallas_call(
        _mla_q_kernel,
        grid=(n // tm,),
        in_specs=[pl.BlockSpec((tm, Q_LORA), row),
                  pl.BlockSpec((Q_LORA, MLA_HEADS * QK_PAD), fixed),
                  pl.BlockSpec((1, NOPE_DIM), fixed),
                  pl.BlockSpec((1, LANES), fixed),
                  pl.BlockSpec((tm, LANES), row)],
        out_specs=pl.BlockSpec((tm, MLA_HEADS * QK_PAD), row),
        out_shape=jax.ShapeDtypeStruct((n, MLA_HEADS * QK_PAD), BF16),
        compiler_params=_params("parallel"),
        name="mla_q",
    )(cq, w_q_heads, g_q[:NOPE_DIM].reshape(1, NOPE_DIM), g_rope, tab)


def _rotate_half_rows(x):
    return jnp.concatenate([-x[ROPE_HALF:], x[:ROPE_HALF]], axis=0)


def _mla_kv_kernel(c_ref, kpe_ref, w_ref, gn_ref, gr_ref, tab_ref, k_ref, v_ref):
    c = c_ref[...]
    kpe = kpe_ref[...]
    sspe = jnp.sum(kpe * kpe, axis=-1, keepdims=True)
    kg = kpe * gr_ref[...]
    rot = jnp.concatenate([-kg[:, ROPE_HALF:], kg[:, :ROPE_HALF]], axis=1)
    tab = tab_ref[...]
    kr = kg * tab[:, :ROPE_DIM] + rot * tab[:, ROPE_DIM:]
    kr = jnp.concatenate([kr, jnp.zeros_like(kr)], axis=1)
    for h in range(MLA_HEADS):
        t = jnp.dot(c, w_ref[:, h * QK_PAD:(h + 1) * QK_PAD], preferred_element_type=F32)
        kn = t[:, :NOPE_DIM]
        rs = lax.rsqrt((jnp.sum(kn * kn, axis=-1, keepdims=True) + sspe) * (1.0 / QK_DIM) + EPS)
        k_ref[:, h * QK_PAD:(h + 1) * QK_PAD] = jnp.concatenate(
            [kn * gn_ref[...] * rs, kr * rs], axis=1).astype(BF16)
        v_ref[:, h * V_DIM:(h + 1) * V_DIM] = t[:, NOPE_DIM:].astype(BF16)


def _mla_kv(ckv_b, kpe, w_kv_heads, g_k, tab, n_rows):
    tm = ROW_TILE
    row = lambda i: (i, 0)
    fixed = lambda i: (0, 0)
    return pl.pallas_call(
        _mla_kv_kernel,
        grid=(n_rows // tm,),
        in_specs=[pl.BlockSpec((tm, KV_LORA), row),
                  pl.BlockSpec((tm, ROPE_DIM), row),
                  pl.BlockSpec((KV_LORA, MLA_HEADS * (NOPE_DIM + V_DIM)), fixed),
                  pl.BlockSpec((1, NOPE_DIM), fixed),
                  pl.BlockSpec((1, ROPE_DIM), fixed),
                  pl.BlockSpec((tm, LANES), row)],
        out_specs=[pl.BlockSpec((tm, MLA_HEADS * QK_PAD), row),
                   pl.BlockSpec((tm, MLA_HEADS * V_DIM), row)],
        out_shape=[jax.ShapeDtypeStruct((n_rows, MLA_HEADS * QK_PAD), BF16),
                   jax.ShapeDtypeStruct((n_rows, MLA_HEADS * V_DIM), BF16)],
        compiler_params=_params("parallel"),
        name="mla_kv",
    )(ckv_b, kpe, w_kv_heads, g_k[:NOPE_DIM].reshape(1, NOPE_DIM),
      g_k[NOPE_DIM:].reshape(1, ROPE_DIM), tab)


def _flash_kernel(q_ref, k_ref, v_ref, o_ref, *, tq, tk):
    qi = pl.program_id(2)

    def step(j, carries, masked):
        ks = pl.multiple_of(j * tk, tk)
        out = []
        for hh, (m, l, acc) in enumerate(carries):
            qk = slice(hh * QK_PAD, (hh + 1) * QK_PAD)
            vv = slice(hh * V_DIM, (hh + 1) * V_DIM)
            s = lax.dot_general(q_ref[:, qk], k_ref[pl.ds(ks, tk), qk], NT_DIMS,
                                preferred_element_type=F32)
            if masked:
                row = lax.broadcasted_iota(jnp.int32, s.shape, 0) // CHUNK
                col = lax.broadcasted_iota(jnp.int32, s.shape, 1) // CHUNK
                s = jnp.where(col <= row, s, -jnp.inf)
            m_new = jnp.maximum(m, jnp.max(s, axis=-1, keepdims=True))
            p = jnp.exp(s - m_new)
            alpha = jnp.exp(m - m_new)
            l = alpha * l + jnp.sum(p, axis=-1, keepdims=True)
            acc = alpha * acc + jnp.dot(p.astype(BF16), v_ref[pl.ds(ks, tk), vv],
                                        preferred_element_type=F32)
            out.append((m_new, l, acc))
        return tuple(out)

    init = tuple((jnp.full((tq, 1), -jnp.inf, F32), jnp.zeros((tq, 1), F32),
                  jnp.zeros((tq, V_DIM), F32)) for _ in range(FLASH_HEADS))
    carries = lax.fori_loop(0, qi, lambda j, c: step(j, c, False), init)
    carries = step(qi, carries, True)
    for hh, (_, l, acc) in enumerate(carries):
        o_ref[:, hh * V_DIM:(hh + 1) * V_DIM] = (acc / l).astype(BF16)


def _flash_prompt(q, k, v, batch, seq):
    tq, tk, hs = FLASH_TQ, FLASH_TK, FLASH_HEADS
    assert tq == tk
    nq = seq // tq
    return pl.pallas_call(
        functools.partial(_flash_kernel, tq=tq, tk=tk),
        grid=(batch, MLA_HEADS // hs, nq),
        in_specs=[pl.BlockSpec((tq, hs * QK_PAD), lambda b, h, i: (b * nq + i, h)),
                  pl.BlockSpec((seq, hs * QK_PAD), lambda b, h, i: (b, h)),
                  pl.BlockSpec((seq, hs * V_DIM), lambda b, h, i: (b, h))],
        out_specs=pl.BlockSpec((tq, hs * V_DIM), lambda b, h, i: (b * nq + i, h)),
        out_shape=jax.ShapeDtypeStruct((batch * seq, MLA_HEADS * V_DIM), BF16),
        compiler_params=_params("parallel", "parallel", "arbitrary"),
        name="flash_prompt",
    )(q, k, v)


def _sample_attn_kernel(q_ref, cache_ref, cnew_ref, kpet_ref, wukt_ref, gkn_ref, gkrt_ref, cost_ref,
                        sint_ref, olat_ref, c_sc, qabs_ref, qr_ref, *, n_keys, t_new):
    kp = c_sc.shape[0]
    past = n_keys - t_new
    c_sc[:past, :] = cache_ref[0, 0].astype(BF16)
    c_sc[past:n_keys, :] = cnew_ref[...]
    c_sc[n_keys:, :] = jnp.zeros((kp - n_keys, KV_LORA), BF16)
    c = c_sc[...]
    kpet = kpet_ref[0]
    sspe = jnp.sum(kpet * kpet, axis=0, keepdims=True)
    kg = kpet * gkrt_ref[...]
    krt = kg * cost_ref[...] + _rotate_half_rows(kg) * sint_ref[...]
    krt = jnp.concatenate([krt, jnp.zeros_like(krt)], axis=0).astype(BF16)
    gkn = gkn_ref[...]
    for h in range(MLA_HEADS):
        qh = q_ref[:, h * QK_PAD:(h + 1) * QK_PAD]
        qn = (qh[:, :NOPE_DIM].astype(F32) * gkn).astype(BF16)
        qa = jnp.dot(qn, wukt_ref[h * NOPE_DIM:(h + 1) * NOPE_DIM, :], preferred_element_type=F32)
        qabs_ref[h * t_new:(h + 1) * t_new, :] = qa.astype(BF16)
        qr_ref[h * t_new:(h + 1) * t_new, :] = qh[:, NOPE_DIM:]
    valid = lax.broadcasted_iota(jnp.int32, (t_new, kp), 1) < n_keys
    hg = SAMPLE_HEAD_GROUP
    for g in range(MLA_HEADS // hg):
        knt = lax.dot_general(wukt_ref[g * hg * NOPE_DIM:(g + 1) * hg * NOPE_DIM, :], c, NT_DIMS,
                              preferred_element_type=F32)
        ss = jnp.sum((knt * knt).reshape(hg, NOPE_DIM, kp), axis=1) + sspe
        rst = lax.rsqrt(ss * (1.0 / QK_DIM) + EPS)
        rows = slice(g * hg * t_new, (g + 1) * hg * t_new)
        s = (lax.dot_general(qabs_ref[rows, :], c, NT_DIMS, preferred_element_type=F32)
             + jnp.dot(qr_ref[rows, :], krt, preferred_element_type=F32))
        ps = []
        for hh in range(hg):
            sh = s[hh * t_new:(hh + 1) * t_new] * rst[hh:hh + 1]
            sh = jnp.where(valid, sh, -jnp.inf)
            e = jnp.exp(sh - jnp.max(sh, axis=-1, keepdims=True))
            ps.append((e / jnp.sum(e, axis=-1, keepdims=True)).astype(BF16))
        p = jnp.concatenate(ps, axis=0)
        olat_ref[0, rows, :] = jnp.dot(p, c, preferred_element_type=F32).astype(BF16)


def _sample_attn(q, cache_ckv, c_rows, kpet_all, w_ukt, g_k, cost, sint, *, row_block0, n_keys, t_new):
    nb, past = cache_ckv.shape[1], cache_ckv.shape[2]
    kp = kpet_all.shape[2]
    assert past + t_new == n_keys
    fixed = lambda b: (0, 0)
    return pl.pallas_call(
        functools.partial(_sample_attn_kernel, n_keys=n_keys, t_new=t_new),
        grid=(nb,),
        in_specs=[pl.BlockSpec((t_new, MLA_HEADS * QK_PAD), lambda b: (row_block0 + b, 0)),
                  pl.BlockSpec((1, 1, past, KV_LORA), lambda b: (0, b, 0, 0)),
                  pl.BlockSpec((t_new, KV_LORA), lambda b: (row_block0 + b, 0)),
                  pl.BlockSpec((1, ROPE_DIM, kp), lambda b: (b, 0, 0)),
                  pl.BlockSpec((MLA_HEADS * NOPE_DIM, KV_LORA), fixed),
                  pl.BlockSpec((1, NOPE_DIM), fixed),
                  pl.BlockSpec((ROPE_DIM, 1), fixed),
                  pl.BlockSpec((ROPE_DIM, kp), fixed),
                  pl.BlockSpec((ROPE_DIM, kp), fixed)],
        out_specs=pl.BlockSpec((1, MLA_HEADS * t_new, KV_LORA), lambda b: (b, 0, 0)),
        out_shape=jax.ShapeDtypeStruct((nb, MLA_HEADS * t_new, KV_LORA), BF16),
        scratch_shapes=[pltpu.VMEM((kp, KV_LORA), BF16),
                        pltpu.VMEM((MLA_HEADS * t_new, KV_LORA), BF16),
                        pltpu.VMEM((MLA_HEADS * t_new, LANES), BF16)],
        compiler_params=_params("parallel"),
        name="sample_attn",
    )(q, cache_ckv, c_rows, kpet_all, w_ukt, g_k[:NOPE_DIM].reshape(1, NOPE_DIM),
      g_k[NOPE_DIM:].reshape(ROPE_DIM, 1), cost, sint)


def _head_mm_kernel(x_ref, w_ref, o_ref):
    nb, t, r = x_ref.shape
    o_ref[...] = jnp.dot(x_ref[...].reshape(nb * t, r), w_ref[...],
                         preferred_element_type=F32).astype(o_ref.dtype)


def _latent_to_values(o_lat, w_uv2d, t_new):
    nb = o_lat.shape[0]
    return pl.pallas_call(
        _head_mm_kernel,
        grid=(MLA_HEADS,),
        in_specs=[pl.BlockSpec((nb, t_new, KV_LORA), lambda h: (0, h, 0)),
                  pl.BlockSpec((KV_LORA, V_DIM), lambda h: (0, h))],
        out_specs=pl.BlockSpec((nb * t_new, V_DIM), lambda h: (0, h)),
        out_shape=jax.ShapeDtypeStruct((nb * t_new, MLA_HEADS * V_DIM), BF16),
        compiler_params=_params("parallel"),
        name="latent_to_values",
    )(o_lat, w_uv2d)


def _gla_kernel(q_ref, k_ref, v_ref, gate_ref, a_ref, wa_ref, ba_ref, go_ref, s0_ref,
                o_ref, sout_ref, st_ref, *, nsub):
    c = pl.program_id(1)

    @pl.when(c == 0)
    def _():
        for h in range(GLA_HEADS):
            st_ref[h] = s0_ref[0, h].T

    tril = (lax.broadcasted_iota(jnp.int32, (CHUNK, CHUNK), 0)
            >= lax.broadcasted_iota(jnp.int32, (CHUNK, CHUNK), 1))
    tril_b = jnp.where(tril, 1.0, 0.0).astype(BF16)
    for j in range(nsub):
        sl = slice(j * CHUNK, (j + 1) * CHUNK)
        x = jnp.dot(a_ref[sl, :], wa_ref[...], preferred_element_type=F32) + ba_ref[...]
        la = (jnp.minimum(x, 0.0) - jnp.log(1.0 + jnp.exp(-jnp.abs(x)))) * (1.0 / GATE_TAU)
        la_hi = la.astype(BF16)
        la_lo = (la - la_hi.astype(F32)).astype(BF16)
        b_all = (jnp.dot(tril_b, la_hi, preferred_element_type=F32)
                 + jnp.dot(tril_b, la_lo, preferred_element_type=F32))
        for h in range(GLA_HEADS):
            kc = slice(h * GLA_DK, (h + 1) * GLA_DK)
            vc = slice(h * GLA_DV, (h + 1) * GLA_DV)
            b = b_all[:, kc]
            b_last = b[CHUNK - 1:CHUNK, :]
            q = q_ref[sl, kc].astype(F32) * (GLA_DK ** -0.5)
            k = k_ref[sl, kc].astype(F32)
            v = v_ref[sl, vc]
            qs = (q * jnp.exp(b)).astype(BF16)
            ks = (k * jnp.exp(-b)).astype(BF16)
            att = lax.dot_general(qs, ks, NT_DIMS, preferred_element_type=F32)
            att = jnp.where(tril, att, 0.0).astype(BF16)
            st = st_ref[h]
            o = (jnp.dot(att, v, preferred_element_type=F32)
                 + lax.dot_general(qs, st.astype(BF16), NT_DIMS, preferred_element_type=F32))
            kd = (k * jnp.exp(b_last - b)).astype(BF16)
            st_ref[h] = st * jnp.exp(b_last) + lax.dot_general(v, kd, TN_DIMS,
                                                               preferred_element_type=F32)
            gt = gate_ref[sl, vc].astype(F32)
            o_ref[sl, vc] = (_rms(o, go_ref[...]) * (gt / (1.0 + jnp.exp(-gt)))).astype(BF16)

    @pl.when(c == pl.num_programs(1) - 1)
    def _():
        for h in range(GLA_HEADS):
            sout_ref[0, h] = st_ref[h].T


def _gla_scan(z, w_a_pad, b_a, g_o, s0, *, row0, n_streams, t_len, nsub):
    tc = nsub * CHUNK
    nc = t_len // tc
    rb0 = row0 // tc
    rows = lambda b, c: rb0 + b * nc + c
    hk, hv = GLA_HEADS * GLA_DK, GLA_HEADS * GLA_DV
    a_blk = (2 * hk + 2 * hv) // LANES
    fixed = lambda b, c: (0, 0)
    state = lambda b, c: (b, 0, 0, 0)
    return pl.pallas_call(
        functools.partial(_gla_kernel, nsub=nsub),
        grid=(n_streams, nc),
        in_specs=[pl.BlockSpec((tc, hk), lambda b, c: (rows(b, c), 0)),
                  pl.BlockSpec((tc, hk), lambda b, c: (rows(b, c), 1)),
                  pl.BlockSpec((tc, hv), lambda b, c: (rows(b, c), 2 * hk // hv)),
                  pl.BlockSpec((tc, hv), lambda b, c: (rows(b, c), 2 * hk // hv + 1)),
                  pl.BlockSpec((tc, LANES), lambda b, c: (rows(b, c), a_blk)),
                  pl.BlockSpec((LANES, hk), fixed),
                  pl.BlockSpec((1, hk), fixed),
                  pl.BlockSpec((1, GLA_DV), fixed),
                  pl.BlockSpec((1, GLA_HEADS, GLA_DK, GLA_DV), state)],
        out_specs=[pl.BlockSpec((tc, hv), lambda b, c: (b * nc + c, 0)),
                   pl.BlockSpec((1, GLA_HEADS, GLA_DK, GLA_DV), state)],
        out_shape=[jax.ShapeDtypeStruct((n_streams * t_len, hv), BF16),
                   jax.ShapeDtypeStruct((n_streams, GLA_HEADS, GLA_DK, GLA_DV), F32)],
        scratch_shapes=[pltpu.VMEM((GLA_HEADS, GLA_DV, GLA_DK), F32)],
        compiler_params=_params("parallel", "arbitrary"),
        name="gla_scan",
    )(z, z, z, z, z, w_a_pad, b_a.reshape(1, hk), g_o.reshape(1, GLA_DV), s0)


def _router_kernel(x_ref, g_ref, w_ref, b_ref, ids_ref, gates_ref):
    xn = _rms(x_ref[...], g_ref[...]).astype(BF16)
    logits = jnp.dot(xn, w_ref[...], preferred_element_type=F32) + b_ref[...]
    lane = lax.broadcasted_iota(jnp.int32, logits.shape, 1)
    neg = -jnp.inf

    def top(mask):
        vals = jnp.where(mask, logits, neg)
        m = jnp.max(vals, axis=-1, keepdims=True)
        idx = jnp.min(jnp.where(vals == m, lane, LANES), axis=-1, keepdims=True)
        return m, idx

    is_grp = lane < N_GROUPS
    m_g, grp = top(is_grp)
    p_grp = 1.0 / jnp.sum(jnp.where(is_grp, jnp.exp(logits - m_g), 0.0), axis=-1, keepdims=True)
    lo = N_GROUPS + grp * EXPERTS_PER_GROUP
    in_grp = (lane >= lo) & (lane < lo + EXPERTS_PER_GROUP)
    m1, i1 = top(in_grp)
    m2, i2 = top(in_grp & (lane != i1))
    e2 = jnp.exp(m2 - m1)
    g1 = p_grp / (1.0 + e2)
    g2 = p_grp * e2 / (1.0 + e2)
    ids_ref[...] = jnp.where(lane == 0, i1 - N_GROUPS, jnp.where(lane == 1, i2 - N_GROUPS, 0))
    gates_ref[...] = jnp.where(lane == 0, g1, jnp.where(lane == 1, g2, 0.0))


def _router(x, g_ffn, w_r_pad, b_r_pad):
    n, d = x.shape
    tm = ROW_TILE
    row = lambda i: (i, 0)
    fixed = lambda i: (0, 0)
    return pl.pallas_call(
        _router_kernel,
        grid=(n // tm,),
        in_specs=[pl.BlockSpec((tm, d), row), pl.BlockSpec((1, d), fixed),
                  pl.BlockSpec((d, LANES), fixed), pl.BlockSpec((1, LANES), fixed)],
        out_specs=[pl.BlockSpec((tm, LANES), row), pl.BlockSpec((tm, LANES), row)],
        out_shape=[jax.ShapeDtypeStruct((n, LANES), jnp.int32), jax.ShapeDtypeStruct((n, LANES), F32)],
        compiler_params=_params("parallel"),
        name="router",
    )(x, g_ffn.reshape(1, d), w_r_pad, b_r_pad)


def _row_copy(src_hbm, row, dst_vmem, r, sem):
    return pltpu.make_async_copy(src_hbm.at[pl.ds(row, 1)], dst_vmem.at[pl.ds(r, 1)], sem)


def _ffn_kernel(tile0_ref, tile_e_ref, start_ref, order_ref, x_hbm, g_ref, wg_ref, wu_ref, wd_ref,
                y_hbm, pos_ref, xbuf, ybuf, wg_b, wu_b, wd_b, gsem, osem):
    e = pl.program_id(0)
    tm = ybuf.shape[1]
    t_lo, t_hi, n_used = tile0_ref[e], tile0_ref[e + 1], tile0_ref[N_EXPERTS]
    nbuf = xbuf.shape[0]
    ahead = nbuf - 1

    def gather_start(t, slot, unrolled):
        ee = tile_e_ref[t]
        base = start_ref[ee]
        last = start_ref[ee + 1] - base - 1
        off0 = (t - tile0_ref[ee]) * tm

        def one(r):
            a = order_ref[base + jnp.minimum(off0 + r, last)]
            tok = lax.shift_right_logical(a, TOP_K_SHIFT)
            _row_copy(x_hbm, tok, xbuf.at[slot], r, gsem.at[slot]).start(priority=1)
            pos_ref[a] = t * tm + r

        if unrolled:
            for r in range(tm):
                one(r)
        else:
            lax.fori_loop(0, tm, lambda r, c: (one(r), c)[1], 0)

    def gather_wait(slot):
        pltpu.make_async_copy(x_hbm.at[pl.ds(0, tm)], xbuf.at[slot], gsem.at[slot]).wait()

    nout = ybuf.shape[0]

    def out_copy(t):
        s = lax.rem(t, nout)
        return pltpu.make_async_copy(ybuf.at[s], y_hbm.at[pl.ds(t * tm, tm)], osem.at[s])

    @pl.when((e == 0) & (n_used > 0))
    def _():
        for k in range(ahead):
            gather_start(jnp.minimum(k, n_used - 1), k, False)

    @pl.when(t_hi > t_lo)
    def _():
        wg_b[...] = wg_ref[0, 0].astype(BF16)
        wu_b[...] = wu_ref[0, 0].astype(BF16)
        wd_b[...] = wd_ref[0, 0].astype(BF16)

        def tile(t, _):
            slot = lax.rem(t, nbuf)
            gather_wait(slot)

            @pl.when(t >= nout)
            def _():
                out_copy(t - nout).wait()

            xn = _rms(xbuf[slot], g_ref[...]).astype(BF16)
            gather_start(jnp.minimum(t + ahead, n_used - 1), lax.rem(t + ahead, nbuf), True)
            hg = jnp.dot(xn, wg_b[...], preferred_element_type=F32)
            hu = jnp.dot(xn, wu_b[...], preferred_element_type=F32)
            hid = (hg / (1.0 + jnp.exp(-hg)) * hu).astype(BF16)
            ybuf[lax.rem(t, nout)] = jnp.dot(hid, wd_b[...], preferred_element_type=F32)
            out_copy(t).start()
            return 0

        lax.fori_loop(t_lo, t_hi, tile, 0)

    @pl.when(e == pl.num_programs(0) - 1)
    def _():
        @pl.when(n_used >= 1)
        def _():
            for k in range(ahead):
                gather_wait(lax.rem(n_used + k, nbuf))

        for back in range(1, nout + 1):
            @pl.when(n_used >= back)
            def _():
                out_copy(n_used - back).wait()

        n_tiles = y_hbm.shape[0] // tm
        ybuf[0] = jnp.zeros(ybuf.shape[1:], F32)

        def zero_copy(t):
            return pltpu.make_async_copy(ybuf.at[0], y_hbm.at[pl.ds(t * tm, tm)], osem.at[0])

        lax.fori_loop(n_used, n_tiles, lambda t, c: (zero_copy(t).start(), c)[1], 0)
        lax.fori_loop(n_used, n_tiles, lambda t, c: (zero_copy(t).wait(), c)[1], 0)


def _expert_ffn(x, g_ffn, w_gate, w_up, w_down, layer, tile0, tile_e, start, order):
    n, d = x.shape
    tm = EXPERT_TILE
    de = w_gate.shape[3]
    fixed = lambda e, *_: (0, 0)
    w_map = lambda e, *_: (layer, e, 0, 0)
    grid_spec = pltpu.PrefetchScalarGridSpec(
        num_scalar_prefetch=4,
        grid=(N_EXPERTS,),
        in_specs=[pl.BlockSpec(memory_space=pl.ANY),
                  pl.BlockSpec((1, d), fixed),
                  pl.BlockSpec((1, 1, d, de), w_map),
                  pl.BlockSpec((1, 1, d, de), w_map),
                  pl.BlockSpec((1, 1, de, d), w_map)],
        out_specs=[pl.BlockSpec(memory_space=pl.ANY), pl.BlockSpec(memory_space=pltpu.SMEM)],
        scratch_shapes=[pltpu.VMEM((GATHER_BUFS, tm, d), F32), pltpu.VMEM((OUT_BUFS, tm, d), F32),
                        pltpu.VMEM((d, de), BF16), pltpu.VMEM((d, de), BF16), pltpu.VMEM((de, d), BF16),
                        pltpu.SemaphoreType.DMA((GATHER_BUFS,)), pltpu.SemaphoreType.DMA((OUT_BUFS,))],
    )
    return pl.pallas_call(
        _ffn_kernel,
        grid_spec=grid_spec,
        out_shape=[jax.ShapeDtypeStruct((tile_e.shape[0] * tm, d), F32),
                   jax.ShapeDtypeStruct(order.shape, jnp.int32)],
        compiler_params=_params("arbitrary"),
        name="expert_ffn",
    )(tile0, tile_e, start, order, x, g_ffn.reshape(1, d), w_gate, w_up, w_down)


def _combine_kernel(pos_ref, y_hbm, gates_ref, x_ref, *rest, split):
    if split:
        go_ref, o_ref, o2_ref, buf0, buf1, sem = rest
    else:
        o_ref, buf0, buf1, sem = rest
    i = pl.program_id(0)
    tc = buf0.shape[1]

    last = pl.num_programs(0) - 1

    def gather_start(t, slot, unrolled):
        def one(r):
            a = TOP_K * (t * tc + r)
            _row_copy(y_hbm, pos_ref[a], buf0.at[slot], r, sem.at[slot]).start(priority=0)
            _row_copy(y_hbm, pos_ref[a + 1], buf1.at[slot], r, sem.at[slot]).start(priority=1)

        if unrolled:
            for r in range(tc):
                one(r)
        else:
            lax.fori_loop(0, tc, lambda r, c: (one(r), c)[1], 0)

    def gather_wait(slot):
        for buf in (buf0, buf1):
            pltpu.make_async_copy(y_hbm.at[pl.ds(0, tc)], buf.at[slot], sem.at[slot]).wait()

    nbuf = buf0.shape[0]
    ahead = nbuf - 1

    @pl.when(i == 0)
    def _():
        for k in range(ahead):
            gather_start(jnp.minimum(k, last), k, False)

    slot = lax.rem(i, nbuf)
    gather_wait(slot)
    gates = gates_ref[...]
    out = x_ref[...] + buf0[slot] * gates[:, 0:1] + buf1[slot] * gates[:, 1:2]
    gather_start(jnp.minimum(i + ahead, last), lax.rem(i + ahead, nbuf), True)
    if split:
        out = _rms(out, go_ref[...])

        @pl.when(i < split)
        def _():
            o_ref[...] = out

        @pl.when(i >= split)
        def _():
            o2_ref[...] = out
    else:
        o_ref[...] = out

    @pl.when(i == last)
    def _():
        for k in range(1, nbuf):
            gather_wait(lax.rem(i + k, nbuf))


def _combine(x, y_slots, gates, pos, g_out=None, n_first=None):
    n, d = x.shape
    tc = COMBINE_TILE
    split = 0 if g_out is None else n_first // tc
    in_specs = [pl.BlockSpec(memory_space=pl.ANY),
                pl.BlockSpec((tc, LANES), lambda i, p: (i, 0)),
                pl.BlockSpec((tc, d), lambda i, p: (i, 0))]
    args = [pos, y_slots, gates, x]
    out_specs = pl.BlockSpec((tc, d), lambda i, p: (i, 0))
    out_shape = jax.ShapeDtypeStruct((n, d), F32)
    if split:
        in_specs.append(pl.BlockSpec((1, d), lambda i, p: (0, 0)))
        args.append(g_out.reshape(1, d))
        out_specs = [pl.BlockSpec((tc, d), lambda i, p: (jnp.minimum(i, split - 1), 0)),
                     pl.BlockSpec((tc, d), lambda i, p: (jnp.maximum(i - split, 0), 0))]
        out_shape = [jax.ShapeDtypeStruct((n_first, d), F32), jax.ShapeDtypeStruct((n - n_first, d), F32)]
    grid_spec = pltpu.PrefetchScalarGridSpec(
        num_scalar_prefetch=1,
        grid=(n // tc,),
        in_specs=in_specs,
        out_specs=out_specs,
        scratch_shapes=[pltpu.VMEM((COMBINE_BUFS, tc, d), F32), pltpu.VMEM((COMBINE_BUFS, tc, d), F32),
                        pltpu.SemaphoreType.DMA((COMBINE_BUFS,))],
    )
    return pl.pallas_call(
        functools.partial(_combine_kernel, split=split),
        grid_spec=grid_spec,
        out_shape=out_shape,
        compiler_params=_params("arbitrary"),
        name="moe_combine",
    )(*args)


def _dispatch_plan(ids):
    n = ids.shape[0]
    a = n * TOP_K
    tm = EXPERT_TILE
    i32 = jnp.int32
    e_flat = ids[:, :TOP_K].reshape(a)
    order = jnp.argsort(e_flat, stable=True).astype(i32)
    experts = jnp.arange(N_EXPERTS, dtype=i32)
    counts = jnp.sum((e_flat[:, None] == experts[None, :]).astype(i32), axis=0)
    zero = jnp.zeros((1,), i32)
    start = jnp.concatenate([zero, jnp.cumsum(counts)]).astype(i32)
    tile_end = jnp.cumsum((counts + tm - 1) // tm)
    tile0 = jnp.concatenate([zero, tile_end]).astype(i32)
    n_tiles = (a + N_EXPERTS * (tm - 1)) // tm
    tile_ids = jnp.arange(n_tiles, dtype=i32)
    tile_e = jnp.minimum(jnp.sum((tile_end[None, :] <= tile_ids[:, None]).astype(i32), axis=1),
                         N_EXPERTS - 1).astype(i32)
    return tile0, tile_e, start, order


def _hier_moe(x, g_ffn, w_r_pad, b_r_pad, w_gate, w_up, w_down, layer, g_out=None, n_first=None):
    ids, gates = _router(x, g_ffn, w_r_pad, b_r_pad)
    tile0, tile_e, start, order = _dispatch_plan(ids)
    y_slots, pos = _expert_ffn(x, g_ffn, w_gate, w_up, w_down, layer, tile0, tile_e, start, order)
    return _combine(x, y_slots, gates, pos, g_out, n_first)


def _rope_table(pos):
    inv = jnp.power(ROPE_THETA, -jnp.arange(ROPE_HALF, dtype=F32) * (2.0 / ROPE_DIM))
    ang = pos[:, None] * inv[None, :]
    return jnp.cos(ang), jnp.sin(ang)


def _pad_cols(w, n):
    return jnp.pad(w, ((0, 0), (0, n - w.shape[1])))


def _q_head_weights(w_uq):
    w = w_uq.reshape(Q_LORA, MLA_HEADS, QK_DIM)
    r1 = w[:, :, NOPE_DIM:NOPE_DIM + ROPE_HALF]
    r2 = w[:, :, NOPE_DIM + ROPE_HALF:]
    return jnp.concatenate([w, -r2, r1], axis=-1).reshape(Q_LORA, MLA_HEADS * QK_PAD).astype(BF16)


def kernel(x_prompt, x_sample, cache_mla_ckv, cache_mla_kpe, state_gla, norm_mix, norm_ffn, norm_out, mla_w_in, mla_g_q_lat, mla_g_kv_lat, mla_w_uq, mla_w_uk, mla_w_uv, mla_g_q, mla_g_k, mla_w_o, gla_w_in, gla_w_a, gla_b_a, gla_g_o, gla_w_o, moe_w_router, moe_b_router, moe_w_gate, moe_w_up, moe_w_down):
    batch, seq, d = x_prompt.shape
    nb, t_new, _ = x_sample.shape
    past = cache_mla_ckv.shape[2]
    n_p, n_s = batch * seq, nb * t_new
    x_p, x_s = x_prompt.reshape(n_p, d), x_sample.reshape(n_s, d)

    pos_rows = jnp.concatenate([jnp.tile(jnp.arange(seq, dtype=F32), batch),
                                jnp.tile(past + jnp.arange(t_new, dtype=F32), nb)])
    cos_r, sin_r = _rope_table(pos_rows)
    tab = jnp.concatenate([cos_r, cos_r, sin_r, sin_r], axis=1)
    w_in_pad = _pad_cols(mla_w_in[0], 9 * LANES).astype(BF16)
    c_q, c_kv_b, ckv_p, ckv_s, kpe_p, kpe_s = _mla_in(x_p, x_s, norm_mix[0], w_in_pad, mla_g_q_lat[0],
                                                      mla_g_kv_lat[0])
    q = _mla_q(c_q, _q_head_weights(mla_w_uq[0]), mla_g_q[0], tab)

    w_uk, w_uv = mla_w_uk[0], mla_w_uv[0]
    w_kv_heads = jnp.concatenate([w_uk, w_uv], axis=-1).reshape(KV_LORA, -1).astype(BF16)
    k_p, v_p = _mla_kv(c_kv_b, kpe_p, w_kv_heads, mla_g_k[0], tab, n_p)
    o_p = _flash_prompt(q, k_p, v_p, batch, seq)

    n_keys = past + t_new
    kp = (n_keys + LANES - 1) // LANES * LANES
    kpe_new = kpe_s.reshape(nb, t_new, ROPE_DIM)
    kpe_all = jnp.concatenate([cache_mla_kpe[0], kpe_new,
                               jnp.zeros((nb, kp - n_keys, ROPE_DIM), F32)], axis=1)
    kpet_all = kpe_all.transpose(0, 2, 1)
    cos_k, sin_k = _rope_table(jnp.arange(kp, dtype=F32))
    cost = jnp.concatenate([cos_k, cos_k], axis=1).T
    sint = jnp.concatenate([sin_k, sin_k], axis=1).T
    w_ukt = w_uk.reshape(KV_LORA, -1).T.astype(BF16)
    o_lat = _sample_attn(q, cache_mla_ckv, c_kv_b, kpet_all, w_ukt, mla_g_k[0], cost, sint,
                         row_block0=n_p // t_new, n_keys=n_keys, t_new=t_new)
    o_s = _latent_to_values(o_lat, w_uv.reshape(KV_LORA, -1).astype(BF16), t_new)

    x = _mm(o_p, mla_w_o[0].astype(BF16), x2=o_s, res=x_p, res2=x_s, out_dtype=F32,
            tm=MM_ROW_TILE, tn=1024)

    w_r_pad = [_pad_cols(moe_w_router[i], LANES).astype(BF16) for i in range(2)]
    b_r_pad = [_pad_cols(moe_b_router[i].reshape(1, -1), LANES) for i in range(2)]
    x = _hier_moe(x, norm_ffn[0], w_r_pad[0], b_r_pad[0], moe_w_gate, moe_w_up, moe_w_down, 0)

    hk, hv = GLA_HEADS * GLA_DK, GLA_HEADS * GLA_DV
    wg = gla_w_in[0]
    w_gla = jnp.concatenate([wg[:, :2 * hk + hv], wg[:, 2 * hk + hv + GATE_RANK:],
                             _pad_cols(wg[:, 2 * hk + hv:2 * hk + hv + GATE_RANK], MXU_WIDTH)],
                            axis=1).astype(BF16)
    z = _mm(x, w_gla, gain=norm_mix[1], out_dtype=BF16, tm=MM_ROW_TILE, tn=5 * MXU_WIDTH)
    w_a_pad = jnp.pad(gla_w_a[0], ((0, LANES - GATE_RANK), (0, 0))).astype(BF16)
    s0_p = jnp.zeros((batch, GLA_HEADS, GLA_DK, GLA_DV), F32)
    og_p, st_p = _gla_scan(z, w_a_pad, gla_b_a[0], gla_g_o[0], s0_p,
                           row0=0, n_streams=batch, t_len=seq, nsub=GLA_SUBCHUNKS)
    og_s, st_s = _gla_scan(z, w_a_pad, gla_b_a[0], gla_g_o[0], state_gla[0],
                           row0=n_p, n_streams=nb, t_len=t_new, nsub=t_new // CHUNK)
    x = _mm(og_p, gla_w_o[0].astype(BF16), x2=og_s, res=x, out_dtype=F32, tm=MM_ROW_TILE, tn=1024)
    y_p, y_s = _hier_moe(x, norm_ffn[1], w_r_pad[1], b_r_pad[1], moe_w_gate, moe_w_up, moe_w_down, 1,
                         g_out=norm_out, n_first=n_p)

    return (y_p.reshape(batch, seq, d),
            y_s.reshape(nb, t_new, d),
            ckv_p.reshape(1, batch, seq, KV_LORA),
            kpe_p.reshape(1, batch, seq, ROPE_DIM),
            st_p[None],
            ckv_s.reshape(1, nb, t_new, KV_LORA),
            kpe_s.reshape(1, nb, t_new, ROPE_DIM),
            st_s[None])
```

```python
import functools

import jax
import jax.numpy as jnp
from jax import lax
from jax.experimental import pallas as pl
from jax.experimental.pallas import tpu as pltpu

F32 = jnp.float32
BF16 = jnp.bfloat16

LANES = 128
MXU_WIDTH = 256
V7X_VMEM_BYTES = 64 * 1024 * 1024
VMEM_LIMIT = V7X_VMEM_BYTES * 7 // 8

EPS = 1e-6
CHUNK = 64
MLA_HEADS = 16
NOPE_DIM = 128
ROPE_DIM = 64
ROPE_HALF = ROPE_DIM // 2
QK_DIM = NOPE_DIM + ROPE_DIM
QK_PAD = 2 * LANES
V_DIM = 128
Q_LORA = 512
KV_LORA = 512
ROPE_THETA = 10000.0
ATTN_SCALE = QK_DIM ** -0.5
GLA_HEADS = 4
GLA_DK = 256
GLA_DV = 512
GATE_RANK = 16
GATE_TAU = 16.0
N_GROUPS = 8
EXPERTS_PER_GROUP = 8
N_EXPERTS = N_GROUPS * EXPERTS_PER_GROUP
TOP_K = 2
TOP_K_SHIFT = 1
assert 1 << TOP_K_SHIFT == TOP_K

ROW_TILE = 512
MM_ROW_TILE = 1024
FLASH_TQ = 512
FLASH_TK = 512
FLASH_HEADS = 4
SAMPLE_HEAD_GROUP = 4
GLA_SUBCHUNKS = 4
EXPERT_TILE = 176
COMBINE_TILE = 128
GATHER_BUFS = 6
COMBINE_BUFS = 3
OUT_BUFS = 3

NT_DIMS = (((1,), (1,)), ((), ()))
TN_DIMS = (((0,), (0,)), ((), ()))


def _params(*sem):
    return pltpu.CompilerParams(dimension_semantics=sem, vmem_limit_bytes=VMEM_LIMIT)


def _rms(x, g):
    return x * lax.rsqrt(jnp.mean(x * x, axis=-1, keepdims=True) + EPS) * g


def _mm_kernel(*refs, has_norm, has_res, split, res_split):
    it = iter(refs)
    x_ref = next(it)
    x2_ref = next(it) if split else None
    g_ref = next(it) if has_norm else None
    w_ref = next(it)
    r_ref = next(it) if has_res else None
    r2_ref = next(it) if res_split else r_ref
    o_ref = next(it)
    if has_norm:
        xn_ref = next(it)

        @pl.when(pl.program_id(1) == 0)
        def _():
            xn_ref[...] = _rms(x_ref[...], g_ref[...]).astype(BF16)

        x_ref = xn_ref

    def emit(src_ref, res_ref):
        acc = jnp.dot(src_ref[...], w_ref[...], preferred_element_type=F32)
        if has_res:
            acc = acc + res_ref[...]
        o_ref[...] = acc.astype(o_ref.dtype)

    if split:
        pl.when(pl.program_id(0) < split)(lambda: emit(x_ref, r_ref))
        pl.when(pl.program_id(0) >= split)(lambda: emit(x2_ref, r2_ref))
    else:
        emit(x_ref, r_ref)


def _mm(x, w, *, x2=None, gain=None, res=None, res2=None, out_dtype, tm, tn):
    m, k = x.shape
    n = w.shape[1]
    has_norm, has_res = gain is not None, res is not None
    split = 0
    in_specs = [pl.BlockSpec((tm, k), lambda i, j: (i, 0))]
    args = [x]
    if x2 is not None:
        assert not has_norm
        split = m // tm
        m += x2.shape[0]
        first = lambda i: jnp.minimum(i, split - 1)
        second = lambda i: jnp.maximum(i - split, 0)
        in_specs = [pl.BlockSpec((tm, k), lambda i, j: (first(i), 0)),
                    pl.BlockSpec((tm, k), lambda i, j: (second(i), 0))]
        args.append(x2)
    if has_norm:
        in_specs.append(pl.BlockSpec((1, k), lambda i, j: (0, 0)))
        args.append(gain.reshape(1, k))
    in_specs.append(pl.BlockSpec((k, tn), lambda i, j: (0, j)))
    args.append(w)
    res_split = res2 is not None
    if res_split:
        assert split and res.shape[0] == split * tm
        in_specs += [pl.BlockSpec((tm, tn), lambda i, j: (first(i), j)),
                     pl.BlockSpec((tm, tn), lambda i, j: (second(i), j))]
        args += [res, res2]
    elif has_res:
        in_specs.append(pl.BlockSpec((tm, tn), lambda i, j: (i, j)))
        args.append(res)
    return pl.pallas_call(
        functools.partial(_mm_kernel, has_norm=has_norm, has_res=has_res, split=split,
                          res_split=res_split),
        grid=(m // tm, n // tn),
        in_specs=in_specs,
        out_specs=pl.BlockSpec((tm, tn), lambda i, j: (i, j)),
        out_shape=jax.ShapeDtypeStruct((m, n), out_dtype),
        scratch_shapes=[pltpu.VMEM((tm, k), BF16)] if has_norm else [],
        compiler_params=_params("parallel", "arbitrary"),
        name="mm",
    )(*args)


def _mla_in_kernel(x_ref, x2_ref, g_ref, w_ref, gq_ref, gkv_ref, cq_ref, ckvb_ref, ckv_ref, ckv2_ref,
                   kpe_ref, kpe2_ref, *, split):
    def emit(src_ref, ckv_out, kpe_out):
        xn = _rms(src_ref[...], g_ref[...]).astype(BF16)
        z = jnp.dot(xn, w_ref[...], preferred_element_type=F32)
        cq_ref[...] = _rms(z[:, :Q_LORA], gq_ref[...]).astype(BF16)
        ckv = _rms(z[:, Q_LORA:Q_LORA + KV_LORA], gkv_ref[...])
        ckv_out[...] = ckv
        ckvb_ref[...] = ckv.astype(BF16)
        kpe_out[...] = z[:, Q_LORA + KV_LORA:Q_LORA + KV_LORA + ROPE_DIM]

    pl.when(pl.program_id(0) < split)(lambda: emit(x_ref, ckv_ref, kpe_ref))
    pl.when(pl.program_id(0) >= split)(lambda: emit(x2_ref, ckv2_ref, kpe2_ref))


def _mla_in(x, x2, g_mix, w_in_pad, g_q_lat, g_kv_lat):
    d = x.shape[1]
    tm = ROW_TILE
    split = x.shape[0] // tm
    n1, n2 = x.shape[0], x2.shape[0]
    n = n1 + n2
    wn = w_in_pad.shape[1]
    row = lambda i: (i, 0)
    first = lambda i: (jnp.minimum(i, split - 1), 0)
    second = lambda i: (jnp.maximum(i - split, 0), 0)
    fixed = lambda i: (0, 0)
    return pl.pallas_call(
        functools.partial(_mla_in_kernel, split=split),
        grid=(n // tm,),
        in_specs=[pl.BlockSpec((tm, d), first),
                  pl.BlockSpec((tm, d), second),
                  pl.BlockSpec((1, d), fixed),
                  pl.BlockSpec((d, wn), fixed), pl.BlockSpec((1, Q_LORA), fixed),
                  pl.BlockSpec((1, KV_LORA), fixed)],
        out_specs=[pl.BlockSpec((tm, Q_LORA), row), pl.BlockSpec((tm, KV_LORA), row),
                   pl.BlockSpec((tm, KV_LORA), first), pl.BlockSpec((tm, KV_LORA), second),
                   pl.BlockSpec((tm, ROPE_DIM), first), pl.BlockSpec((tm, ROPE_DIM), second)],
        out_shape=[jax.ShapeDtypeStruct((n, Q_LORA), BF16), jax.ShapeDtypeStruct((n, KV_LORA), BF16),
                   jax.ShapeDtypeStruct((n1, KV_LORA), F32), jax.ShapeDtypeStruct((n2, KV_LORA), F32),
                   jax.ShapeDtypeStruct((n1, ROPE_DIM), F32), jax.ShapeDtypeStruct((n2, ROPE_DIM), F32)],
        compiler_params=_params("parallel"),
        name="mla_in",
    )(x, x2, g_mix.reshape(1, d), w_in_pad, g_q_lat.reshape(1, -1), g_kv_lat.reshape(1, -1))


def _mla_q_kernel(cq_ref, w_ref, gn_ref, g2_ref, tab_ref, q_ref):
    cq = cq_ref[...]
    gtab = g2_ref[...] * tab_ref[...]
    is_rope = lax.broadcasted_iota(jnp.int32, gtab.shape, 1) < ROPE_DIM
    for h in range(MLA_HEADS):
        cols = slice(h * QK_PAD, (h + 1) * QK_PAD)
        t = jnp.dot(cq, w_ref[:, cols], preferred_element_type=F32)
        t1 = t[:, :NOPE_DIM]
        t2 = t[:, NOPE_DIM:]
        ss = (jnp.sum(t1 * t1, axis=-1, keepdims=True)
              + jnp.sum(jnp.where(is_rope, t2 * t2, 0.0), axis=-1, keepdims=True))
        rs = lax.rsqrt(ss * (1.0 / QK_DIM) + EPS) * ATTN_SCALE
        u = t2 * gtab
        u = jnp.where(is_rope, u + pltpu.roll(u, ROPE_DIM, axis=1), 0.0)
        q_ref[:, cols] = jnp.concatenate([t1 * gn_ref[...] * rs, u * rs], axis=1).astype(BF16)


def _mla_q(cq, w_q_heads, g_q, tab):
    n = cq.shape[0]
    tm = ROW_TILE
    g1, g2 = g_q[NOPE_DIM:NOPE_DIM + ROPE_HALF], g_q[NOPE_DIM + ROPE_HALF:]
    g_rope = jnp.concatenate([g1, g2, g2, g1]).reshape(1, LANES)
    row = lambda i: (i, 0)
    fixed = lambda i: (0, 0)
    return pl.pallas_call(
        _mla_q_kernel,
        grid=(n // tm,),
        in_specs=[pl.BlockSpec((tm, Q_LORA), row),
                  pl.BlockSpec((Q_LORA, MLA_HEADS * QK_PAD), fixed),
                  pl.BlockSpec((1, NOPE_DIM), fixed),
                  pl.BlockSpec((1, LANES), fixed),
                  pl.BlockSpec((tm, LANES), row)],
        out_specs=pl.BlockSpec((tm, MLA_HEADS * QK_PAD), row),
        out_shape=jax.ShapeDtypeStruct((n, MLA_HEADS * QK_PAD), BF16),
        compiler_params=_params("parallel"),
        name="mla_q",
    )(cq, w_q_heads, g_q[:NOPE_DIM].reshape(1, NOPE_DIM), g_rope, tab)


def _rotate_half_rows(x):
    return jnp.concatenate([-x[ROPE_HALF:], x[:ROPE_HALF]], axis=0)


def _mla_kv_kernel(c_ref, kpe_ref, w_ref, gn_ref, gr_ref, tab_ref, k_ref, v_ref):
    c = c_ref[...]
    kpe = kpe_ref[...]
    sspe = jnp.sum(kpe * kpe, axis=-1, keepdims=True)
    kg = kpe * gr_ref[...]
    rot = jnp.concatenate([-kg[:, ROPE_HALF:], kg[:, :ROPE_HALF]], axis=1)
    tab = tab_ref[...]
    kr = kg * tab[:, :ROPE_DIM] + rot * tab[:, ROPE_DIM:]
    kr = jnp.concatenate([kr, jnp.zeros_like(kr)], axis=1)
    for h in range(MLA_HEADS):
        t = jnp.dot(c, w_ref[:, h * QK_PAD:(h + 1) * QK_PAD], preferred_element_type=F32)
        kn = t[:, :NOPE_DIM]
        rs = lax.rsqrt((jnp.sum(kn * kn, axis=-1, keepdims=True) + sspe) * (1.0 / QK_DIM) + EPS)
        k_ref[:, h * QK_PAD:(h + 1) * QK_PAD] = jnp.concatenate(
            [kn * gn_ref[...] * rs, kr * rs], axis=1).astype(BF16)
        v_ref[:, h * V_DIM:(h + 1) * V_DIM] = t[:, NOPE_DIM:].astype(BF16)


def _mla_kv(ckv_b, kpe, w_kv_heads, g_k, tab, n_rows):
    tm = ROW_TILE
    row = lambda i: (i, 0)
    fixed = lambda i: (0, 0)
    return pl.pallas_call(
        _mla_kv_kernel,
        grid=(n_rows // tm,),
        in_specs=[pl.BlockSpec((tm, KV_LORA), row),
                  pl.BlockSpec((tm, ROPE_DIM), row),
                  pl.BlockSpec((KV_LORA, MLA_HEADS * (NOPE_DIM + V_DIM)), fixed),
                  pl.BlockSpec((1, NOPE_DIM), fixed),
                  pl.BlockSpec((1, ROPE_DIM), fixed),
                  pl.BlockSpec((tm, LANES), row)],
        out_specs=[pl.BlockSpec((tm, MLA_HEADS * QK_PAD), row),
                   pl.BlockSpec((tm, MLA_HEADS * V_DIM), row)],
        out_shape=[jax.ShapeDtypeStruct((n_rows, MLA_HEADS * QK_PAD), BF16),
                   jax.ShapeDtypeStruct((n_rows, MLA_HEADS * V_DIM), BF16)],
        compiler_params=_params("parallel"),
        name="mla_kv",
    )(ckv_b, kpe, w_kv_heads, g_k[:NOPE_DIM].reshape(1, NOPE_DIM),
      g_k[NOPE_DIM:].reshape(1, ROPE_DIM), tab)


def _flash_kernel(q_ref, k_ref, v_ref, o_ref, *, tq, tk):
    qi = pl.program_id(2)

    def step(j, carries, masked):
        ks = pl.multiple_of(j * tk, tk)
        out = []
        for hh, (m, l, acc) in enumerate(carries):
            qk = slice(hh * QK_PAD, (hh + 1) * QK_PAD)
            vv = slice(hh * V_DIM, (hh + 1) * V_DIM)
            s = lax.dot_general(q_ref[:, qk], k_ref[pl.ds(ks, tk), qk], NT_DIMS,
                                preferred_element_type=F32)
            if masked:
                row = lax.broadcasted_iota(jnp.int32, s.shape, 0) // CHUNK
                col = lax.broadcasted_iota(jnp.int32, s.shape, 1) // CHUNK
                s = jnp.where(col <= row, s, -jnp.inf)
            m_new = jnp.maximum(m, jnp.max(s, axis=-1, keepdims=True))
            p = jnp.exp(s - m_new)
            alpha = jnp.exp(m - m_new)
            l = alpha * l + jnp.sum(p, axis=-1, keepdims=True)
            acc = alpha * acc + jnp.dot(p.astype(BF16), v_ref[pl.ds(ks, tk), vv],
                                        preferred_element_type=F32)
            out.append((m_new, l, acc))
        return tuple(out)

    init = tuple((jnp.full((tq, 1), -jnp.inf, F32), jnp.zeros((tq, 1), F32),
                  jnp.zeros((tq, V_DIM), F32)) for _ in range(FLASH_HEADS))
    carries = lax.fori_loop(0, qi, lambda j, c: step(j, c, False), init)
    carries = step(qi, carries, True)
    for hh, (_, l, acc) in enumerate(carries):
        o_ref[:, hh * V_DIM:(hh + 1) * V_DIM] = (acc / l).astype(BF16)


def _flash_prompt(q, k, v, batch, seq):
    tq, tk, hs = FLASH_TQ, FLASH_TK, FLASH_HEADS
    assert tq == tk
    nq = seq // tq
    return pl.pallas_call(
        functools.partial(_flash_kernel, tq=tq, tk=tk),
        grid=(batch, MLA_HEADS // hs, nq),
        in_specs=[pl.BlockSpec((tq, hs * QK_PAD), lambda b, h, i: (b * nq + i, h)),
                  pl.BlockSpec((seq, hs * QK_PAD), lambda b, h, i: (b, h)),
                  pl.BlockSpec((seq, hs * V_DIM), lambda b, h, i: (b, h))],
        out_specs=pl.BlockSpec((tq, hs * V_DIM), lambda b, h, i: (b * nq + i, h)),
        out_shape=jax.ShapeDtypeStruct((batch * seq, MLA_HEADS * V_DIM), BF16),
        compiler_params=_params("parallel", "parallel", "arbitrary"),
        name="flash_prompt",
    )(q, k, v)


def _sample_attn_kernel(q_ref, cache_ref, cnew_ref, kpet_ref, wukt_ref, gkn_ref, gkrt_ref, cost_ref,
                        sint_ref, olat_ref, c_sc, qabs_ref, qr_ref, *, n_keys, t_new):
    kp = c_sc.shape[0]
    past = n_keys - t_new
    c_sc[:past, :] = cache_ref[0, 0].astype(BF16)
    c_sc[past:n_keys, :] = cnew_ref[...]
    c_sc[n_keys:, :] = jnp.zeros((kp - n_keys, KV_LORA), BF16)
    c = c_sc[...]
    kpet = kpet_ref[0]
    sspe = jnp.sum(kpet * kpet, axis=0, keepdims=True)
    kg = kpet * gkrt_ref[...]
    krt = kg * cost_ref[...] + _rotate_half_rows(kg) * sint_ref[...]
    krt = jnp.concatenate([krt, jnp.zeros_like(krt)], axis=0).astype(BF16)
    gkn = gkn_ref[...]
    for h in range(MLA_HEADS):
        qh = q_ref[:, h * QK_PAD:(h + 1) * QK_PAD]
        qn = (qh[:, :NOPE_DIM].astype(F32) * gkn).astype(BF16)
        qa = jnp.dot(qn, wukt_ref[h * NOPE_DIM:(h + 1) * NOPE_DIM, :], preferred_element_type=F32)
        qabs_ref[h * t_new:(h + 1) * t_new, :] = qa.astype(BF16)
        qr_ref[h * t_new:(h + 1) * t_new, :] = qh[:, NOPE_DIM:]
    valid = lax.broadcasted_iota(jnp.int32, (t_new, kp), 1) < n_keys
    hg = SAMPLE_HEAD_GROUP
    for g in range(MLA_HEADS // hg):
        knt = lax.dot_general(wukt_ref[g * hg * NOPE_DIM:(g + 1) * hg * NOPE_DIM, :], c, NT_DIMS,
                              preferred_element_type=F32)
        ss = jnp.sum((knt * knt).reshape(hg, NOPE_DIM, kp), axis=1) + sspe
        rst = lax.rsqrt(ss * (1.0 / QK_DIM) + EPS)
        rows = slice(g * hg * t_new, (g + 1) * hg * t_new)
        s = (lax.dot_general(qabs_ref[rows, :], c, NT_DIMS, preferred_element_type=F32)
             + jnp.dot(qr_ref[rows, :], krt, preferred_element_type=F32))
        ps = []
        for hh in range(hg):
            sh = s[hh * t_new:(hh + 1) * t_new] * rst[hh:hh + 1]
            sh = jnp.where(valid, sh, -jnp.inf)
            e = jnp.exp(sh - jnp.max(sh, axis=-1, keepdims=True))
            ps.append((e / jnp.sum(e, axis=-1, keepdims=True)).astype(BF16))
        p = jnp.concatenate(ps, axis=0)
        olat_ref[0, rows, :] = jnp.dot(p, c, preferred_element_type=F32).astype(BF16)


def _sample_attn(q, cache_ckv, c_rows, kpet_all, w_ukt, g_k, cost, sint, *, row_block0, n_keys, t_new):
    nb, past = cache_ckv.shape[1], cache_ckv.shape[2]
    kp = kpet_all.shape[2]
    assert past + t_new == n_keys
    fixed = lambda b: (0, 0)
    return pl.pallas_call(
        functools.partial(_sample_attn_kernel, n_keys=n_keys, t_new=t_new),
        grid=(nb,),
        in_specs=[pl.BlockSpec((t_new, MLA_HEADS * QK_PAD), lambda b: (row_block0 + b, 0)),
                  pl.BlockSpec((1, 1, past, KV_LORA), lambda b: (0, b, 0, 0)),
                  pl.BlockSpec((t_new, KV_LORA), lambda b: (row_block0 + b, 0)),
                  pl.BlockSpec((1, ROPE_DIM, kp), lambda b: (b, 0, 0)),
                  pl.BlockSpec((MLA_HEADS * NOPE_DIM, KV_LORA), fixed),
                  pl.BlockSpec((1, NOPE_DIM), fixed),
                  pl.BlockSpec((ROPE_DIM, 1), fixed),
                  pl.BlockSpec((ROPE_DIM, kp), fixed),
                  pl.BlockSpec((ROPE_DIM, kp), fixed)],
        out_specs=pl.BlockSpec((1, MLA_HEADS * t_new, KV_LORA), lambda b: (b, 0, 0)),
        out_shape=jax.ShapeDtypeStruct((nb, MLA_HEADS * t_new, KV_LORA), BF16),
        scratch_shapes=[pltpu.VMEM((kp, KV_LORA), BF16),
                        pltpu.VMEM((MLA_HEADS * t_new, KV_LORA), BF16),
                        pltpu.VMEM((MLA_HEADS * t_new, LANES), BF16)],
        compiler_params=_params("parallel"),
        name="sample_attn",
    )(q, cache_ckv, c_rows, kpet_all, w_ukt, g_k[:NOPE_DIM].reshape(1, NOPE_DIM),
      g_k[NOPE_DIM:].reshape(ROPE_DIM, 1), cost, sint)


def _head_mm_kernel(x_ref, w_ref, o_ref):
    nb, t, r = x_ref.shape
    o_ref[...] = jnp.dot(x_ref[...].reshape(nb * t, r), w_ref[...],
                         preferred_element_type=F32).astype(o_ref.dtype)


def _latent_to_values(o_lat, w_uv2d, t_new):
    nb = o_lat.shape[0]
    return pl.pallas_call(
        _head_mm_kernel,
        grid=(MLA_HEADS,),
        in_specs=[pl.BlockSpec((nb, t_new, KV_LORA), lambda h: (0, h, 0)),
                  pl.BlockSpec((KV_LORA, V_DIM), lambda h: (0, h))],
        out_specs=pl.BlockSpec((nb * t_new, V_DIM), lambda h: (0, h)),
        out_shape=jax.ShapeDtypeStruct((nb * t_new, MLA_HEADS * V_DIM), BF16),
        compiler_params=_params("parallel"),
        name="latent_to_values",
    )(o_lat, w_uv2d)


def _gla_kernel(q_ref, k_ref, v_ref, gate_ref, a_ref, wa_ref, ba_ref, go_ref, s0_ref,
                o_ref, sout_ref, st_ref, *, nsub):
    c = pl.program_id(1)

    @pl.when(c == 0)
    def _():
        for h in range(GLA_HEADS):
            st_ref[h] = s0_ref[0, h].T

    tril = (lax.broadcasted_iota(jnp.int32, (CHUNK, CHUNK), 0)
            >= lax.broadcasted_iota(jnp.int32, (CHUNK, CHUNK), 1))
    tril_b = jnp.where(tril, 1.0, 0.0).astype(BF16)
    for j in range(nsub):
        sl = slice(j * CHUNK, (j + 1) * CHUNK)
        x = jnp.dot(a_ref[sl, :], wa_ref[...], preferred_element_type=F32) + ba_ref[...]
        la = (jnp.minimum(x, 0.0) - jnp.log(1.0 + jnp.exp(-jnp.abs(x)))) * (1.0 / GATE_TAU)
        la_hi = la.astype(BF16)
        la_lo = (la - la_hi.astype(F32)).astype(BF16)
        b_all = (jnp.dot(tril_b, la_hi, preferred_element_type=F32)
                 + jnp.dot(tril_b, la_lo, preferred_element_type=F32))
        for h in range(GLA_HEADS):
            kc = slice(h * GLA_DK, (h + 1) * GLA_DK)
            vc = slice(h * GLA_DV, (h + 1) * GLA_DV)
            b = b_all[:, kc]
            b_last = b[CHUNK - 1:CHUNK, :]
            q = q_ref[sl, kc].astype(F32) * (GLA_DK ** -0.5)
            k = k_ref[sl, kc].astype(F32)
            v = v_ref[sl, vc]
            qs = (q * jnp.exp(b)).astype(BF16)
            ks = (k * jnp.exp(-b)).astype(BF16)
            att = lax.dot_general(qs, ks, NT_DIMS, preferred_element_type=F32)
            att = jnp.where(tril, att, 0.0).astype(BF16)
            st = st_ref[h]
            o = (jnp.dot(att, v, preferred_element_type=F32)
                 + lax.dot_general(qs, st.astype(BF16), NT_DIMS, preferred_element_type=F32))
            kd = (k * jnp.exp(b_last - b)).astype(BF16)
            st_ref[h] = st * jnp.exp(b_last) + lax.dot_general(v, kd, TN_DIMS,
                                                               preferred_element_type=F32)
            gt = gate_ref[sl, vc].astype(F32)
            o_ref[sl, vc] = (_rms(o, go_ref[...]) * (gt / (1.0 + jnp.exp(-gt)))).astype(BF16)

    @pl.when(c == pl.num_programs(1) - 1)
    def _():
        for h in range(GLA_HEADS):
            sout_ref[0, h] = st_ref[h].T


def _gla_scan(z, w_a_pad, b_a, g_o, s0, *, row0, n_streams, t_len, nsub):
    tc = nsub * CHUNK
    nc = t_len // tc
    rb0 = row0 // tc
    rows = lambda b, c: rb0 + b * nc + c
    hk, hv = GLA_HEADS * GLA_DK, GLA_HEADS * GLA_DV
    a_blk = (2 * hk + 2 * hv) // LANES
    fixed = lambda b, c: (0, 0)
    state = lambda b, c: (b, 0, 0, 0)
    return pl.pallas_call(
        functools.partial(_gla_kernel, nsub=nsub),
        grid=(n_streams, nc),
        in_specs=[pl.BlockSpec((tc, hk), lambda b, c: (rows(b, c), 0)),
                  pl.BlockSpec((tc, hk), lambda b, c: (rows(b, c), 1)),
                  pl.BlockSpec((tc, hv), lambda b, c: (rows(b, c), 2 * hk // hv)),
                  pl.BlockSpec((tc, hv), lambda b, c: (rows(b, c), 2 * hk // hv + 1)),
                  pl.BlockSpec((tc, LANES), lambda b, c: (rows(b, c), a_blk)),
                  pl.BlockSpec((LANES, hk), fixed),
                  pl.BlockSpec((1, hk), fixed),
                  pl.BlockSpec((1, GLA_DV), fixed),
                  pl.BlockSpec((1, GLA_HEADS, GLA_DK, GLA_DV), state)],
        out_specs=[pl.BlockSpec((tc, hv), lambda b, c: (b * nc + c, 0)),
                   pl.BlockSpec((1, GLA_HEADS, GLA_DK, GLA_DV), state)],
        out_shape=[jax.ShapeDtypeStruct((n_streams * t_len, hv), BF16),
                   jax.ShapeDtypeStruct((n_streams, GLA_HEADS, GLA_DK, GLA_DV), F32)],
        scratch_shapes=[pltpu.VMEM((GLA_HEADS, GLA_DV, GLA_DK), F32)],
        compiler_params=_params("parallel", "arbitrary"),
        name="gla_scan",
    )(z, z, z, z, z, w_a_pad, b_a.reshape(1, hk), g_o.reshape(1, GLA_DV), s0)


def _router_kernel(x_ref, g_ref, w_ref, b_ref, ids_ref, gates_ref):
    xn = _rms(x_ref[...], g_ref[...]).astype(BF16)
    logits = jnp.dot(xn, w_ref[...], preferred_element_type=F32) + b_ref[...]
    lane = lax.broadcasted_iota(jnp.int32, logits.shape, 1)
    neg = -jnp.inf

    def top(mask):
        vals = jnp.where(mask, logits, neg)
        m = jnp.max(vals, axis=-1, keepdims=True)
        idx = jnp.min(jnp.where(vals == m, lane, LANES), axis=-1, keepdims=True)
        return m, idx

    is_grp = lane < N_GROUPS
    m_g, grp = top(is_grp)
    p_grp = 1.0 / jnp.sum(jnp.where(is_grp, jnp.exp(logits - m_g), 0.0), axis=-1, keepdims=True)
    lo = N_GROUPS + grp * EXPERTS_PER_GROUP
    in_grp = (lane >= lo) & (lane < lo + EXPERTS_PER_GROUP)
    m1, i1 = top(in_grp)
    m2, i2 = top(in_grp & (lane != i1))
    e2 = jnp.exp(m2 - m1)
    g1 = p_grp / (1.0 + e2)
    g2 = p_grp * e2 / (1.0 + e2)
    ids_ref[...] = jnp.where(lane == 0, i1 - N_GROUPS, jnp.where(lane == 1, i2 - N_GROUPS, 0))
    gates_ref[...] = jnp.where(lane == 0, g1, jnp.where(lane == 1, g2, 0.0))


def _router(x, g_ffn, w_r_pad, b_r_pad):
    n, d = x.shape
    tm = ROW_TILE
    row = lambda i: (i, 0)
    fixed = lambda i: (0, 0)
    return pl.pallas_call(
        _router_kernel,
        grid=(n // tm,),
        in_specs=[pl.BlockSpec((tm, d), row), pl.BlockSpec((1, d), fixed),
                  pl.BlockSpec((d, LANES), fixed), pl.BlockSpec((1, LANES), fixed)],
        out_specs=[pl.BlockSpec((tm, LANES), row), pl.BlockSpec((tm, LANES), row)],
        out_shape=[jax.ShapeDtypeStruct((n, LANES), jnp.int32), jax.ShapeDtypeStruct((n, LANES), F32)],
        compiler_params=_params("parallel"),
        name="router",
    )(x, g_ffn.reshape(1, d), w_r_pad, b_r_pad)


def _row_copy(src_hbm, row, dst_vmem, r, sem):
    return pltpu.make_async_copy(src_hbm.at[pl.ds(row, 1)], dst_vmem.at[pl.ds(r, 1)], sem)


def _ffn_kernel(tile0_ref, tile_e_ref, start_ref, order_ref, x_hbm, g_ref, wg_ref, wu_ref, wd_ref,
                y_hbm, pos_ref, xbuf, ybuf, wg_b, wu_b, wd_b, gsem, osem):
    e = pl.program_id(0)
    tm = ybuf.shape[1]
    t_lo, t_hi, n_used = tile0_ref[e], tile0_ref[e + 1], tile0_ref[N_EXPERTS]
    nbuf = xbuf.shape[0]
    ahead = nbuf - 1

    def gather_start(t, slot, unrolled):
        ee = tile_e_ref[t]
        base = start_ref[ee]
        last = start_ref[ee + 1] - base - 1
        off0 = (t - tile0_ref[ee]) * tm

        def one(r):
            a = order_ref[base + jnp.minimum(off0 + r, last)]
            tok = lax.shift_right_logical(a, TOP_K_SHIFT)
            _row_copy(x_hbm, tok, xbuf.at[slot], r, gsem.at[slot]).start()
            pos_ref[a] = t * tm + r

        if unrolled:
            for r in range(tm):
                one(r)
        else:
            lax.fori_loop(0, tm, lambda r, c: (one(r), c)[1], 0)

    def gather_wait(slot):
        pltpu.make_async_copy(x_hbm.at[pl.ds(0, tm)], xbuf.at[slot], gsem.at[slot]).wait()

    nout = ybuf.shape[0]

    def out_copy(t):
        s = lax.rem(t, nout)
        return pltpu.make_async_copy(ybuf.at[s], y_hbm.at[pl.ds(t * tm, tm)], osem.at[s])

    @pl.when((e == 0) & (n_used > 0))
    def _():
        for k in range(ahead):
            gather_start(jnp.minimum(k, n_used - 1), k, False)

    @pl.when(t_hi > t_lo)
    def _():
        wg_b[...] = wg_ref[0, 0].astype(BF16)
        wu_b[...] = wu_ref[0, 0].astype(BF16)
        wd_b[...] = wd_ref[0, 0].astype(BF16)

        def tile(t, _):
            slot = lax.rem(t, nbuf)
            gather_wait(slot)

            @pl.when(t >= nout)
            def _():
                out_copy(t - nout).wait()

            xn = _rms(xbuf[slot], g_ref[...]).astype(BF16)
            gather_start(jnp.minimum(t + ahead, n_used - 1), lax.rem(t + ahead, nbuf), True)
            hg = jnp.dot(xn, wg_b[...], preferred_element_type=F32)
            hu = jnp.dot(xn, wu_b[...], preferred_element_type=F32)
            hid = (hg / (1.0 + jnp.exp(-hg)) * hu).astype(BF16)
            ybuf[lax.rem(t, nout)] = jnp.dot(hid, wd_b[...], preferred_element_type=F32)
            out_copy(t).start()
            return 0

        lax.fori_loop(t_lo, t_hi, tile, 0)

    @pl.when(e == pl.num_programs(0) - 1)
    def _():
        @pl.when(n_used >= 1)
        def _():
            for k in range(ahead):
                gather_wait(lax.rem(n_used + k, nbuf))

        for back in range(1, nout + 1):
            @pl.when(n_used >= back)
            def _():
                out_copy(n_used - back).wait()

        n_tiles = y_hbm.shape[0] // tm
        ybuf[0] = jnp.zeros(ybuf.shape[1:], F32)

        def zero_copy(t):
            return pltpu.make_async_copy(ybuf.at[0], y_hbm.at[pl.ds(t * tm, tm)], osem.at[0])

        lax.fori_loop(n_used, n_tiles, lambda t, c: (zero_copy(t).start(), c)[1], 0)
        lax.fori_loop(n_used, n_tiles, lambda t, c: (zero_copy(t).wait(), c)[1], 0)


def _expert_ffn(x, g_ffn, w_gate, w_up, w_down, layer, tile0, tile_e, start, order):
    n, d = x.shape
    tm = EXPERT_TILE
    de = w_gate.shape[3]
    fixed = lambda e, *_: (0, 0)
    w_map = lambda e, *_: (layer, e, 0, 0)
    grid_spec = pltpu.PrefetchScalarGridSpec(
        num_scalar_prefetch=4,
        grid=(N_EXPERTS,),
        in_specs=[pl.BlockSpec(memory_space=pl.ANY),
                  pl.BlockSpec((1, d), fixed),
                  pl.BlockSpec((1, 1, d, de), w_map),
                  pl.BlockSpec((1, 1, d, de), w_map),
                  pl.BlockSpec((1, 1, de, d), w_map)],
        out_specs=[pl.BlockSpec(memory_space=pl.ANY), pl.BlockSpec(memory_space=pltpu.SMEM)],
        scratch_shapes=[pltpu.VMEM((GATHER_BUFS, tm, d), F32), pltpu.VMEM((OUT_BUFS, tm, d), F32),
                        pltpu.VMEM((d, de), BF16), pltpu.VMEM((d, de), BF16), pltpu.VMEM((de, d), BF16),
                        pltpu.SemaphoreType.DMA((GATHER_BUFS,)), pltpu.SemaphoreType.DMA((OUT_BUFS,))],
    )
    return pl.pallas_call(
        _ffn_kernel,
        grid_spec=grid_spec,
        out_shape=[jax.ShapeDtypeStruct((tile_e.shape[0] * tm, d), F32),
                   jax.ShapeDtypeStruct(order.shape, jnp.int32)],
        compiler_params=_params("arbitrary"),
        name="expert_ffn",
    )(tile0, tile_e, start, order, x, g_ffn.reshape(1, d), w_gate, w_up, w_down)


def _combine_kernel(pos_ref, y_hbm, gates_ref, x_ref, *rest, split):
    if split:
        go_ref, o_ref, o2_ref, buf0, buf1, sem = rest
    else:
        o_ref, buf0, buf1, sem = rest
    i = pl.program_id(0)
    tc = buf0.shape[1]

    last = pl.num_programs(0) - 1

    def gather_start(t, slot, unrolled):
        def one(r):
            a = TOP_K * (t * tc + r)
            _row_copy(y_hbm, pos_ref[a], buf0.at[slot], r, sem.at[slot]).start(priority=0)
            _row_copy(y_hbm, pos_ref[a + 1], buf1.at[slot], r, sem.at[slot]).start(priority=1)

        if unrolled:
            for r in range(tc):
                one(r)
        else:
            lax.fori_loop(0, tc, lambda r, c: (one(r), c)[1], 0)

    def gather_wait(slot):
        for buf in (buf0, buf1):
            pltpu.make_async_copy(y_hbm.at[pl.ds(0, tc)], buf.at[slot], sem.at[slot]).wait()

    nbuf = buf0.shape[0]
    ahead = nbuf - 1

    @pl.when(i == 0)
    def _():
        for k in range(ahead):
            gather_start(jnp.minimum(k, last), k, False)

    slot = lax.rem(i, nbuf)
    gather_wait(slot)
    gates = gates_ref[...]
    out = x_ref[...] + buf0[slot] * gates[:, 0:1] + buf1[slot] * gates[:, 1:2]
    gather_start(jnp.minimum(i + ahead, last), lax.rem(i + ahead, nbuf), True)
    if split:
        out = _rms(out, go_ref[...])

        @pl.when(i < split)
        def _():
            o_ref[...] = out

        @pl.when(i >= split)
        def _():
            o2_ref[...] = out
    else:
        o_ref[...] = out

    @pl.when(i == last)
    def _():
        for k in range(1, nbuf):
            gather_wait(lax.rem(i + k, nbuf))


def _combine(x, y_slots, gates, pos, g_out=None, n_first=None):
    n, d = x.shape
    tc = COMBINE_TILE
    split = 0 if g_out is None else n_first // tc
    in_specs = [pl.BlockSpec(memory_space=pl.ANY),
                pl.BlockSpec((tc, LANES), lambda i, p: (i, 0)),
                pl.BlockSpec((tc, d), lambda i, p: (i, 0))]
    args = [pos, y_slots, gates, x]
    out_specs = pl.BlockSpec((tc, d), lambda i, p: (i, 0))
    out_shape = jax.ShapeDtypeStruct((n, d), F32)
    if split:
        in_specs.append(pl.BlockSpec((1, d), lambda i, p: (0, 0)))
        args.append(g_out.reshape(1, d))
        out_specs = [pl.BlockSpec((tc, d), lambda i, p: (jnp.minimum(i, split - 1), 0)),
                     pl.BlockSpec((tc, d), lambda i, p: (jnp.maximum(i - split, 0), 0))]
        out_shape = [jax.ShapeDtypeStruct((n_first, d), F32), jax.ShapeDtypeStruct((n - n_first, d), F32)]
    grid_spec = pltpu.PrefetchScalarGridSpec(
        num_scalar_prefetch=1,
        grid=(n // tc,),
        in_specs=in_specs,
        out_specs=out_specs,
        scratch_shapes=[pltpu.VMEM((COMBINE_BUFS, tc, d), F32), pltpu.VMEM((COMBINE_BUFS, tc, d), F32),
                        pltpu.SemaphoreType.DMA((COMBINE_BUFS,))],
    )
    return pl.pallas_call(
        functools.partial(_combine_kernel, split=split),
        grid_spec=grid_spec,
        out_shape=out_shape,
        compiler_params=_params("arbitrary"),
        name="moe_combine",
    )(*args)


def _dispatch_plan(ids):
    n = ids.shape[0]
    a = n * TOP_K
    tm = EXPERT_TILE
    i32 = jnp.int32
    e_flat = ids[:, :TOP_K].reshape(a)
    order = jnp.argsort(e_flat, stable=True).astype(i32)
    experts = jnp.arange(N_EXPERTS, dtype=i32)
    counts = jnp.sum((e_flat[:, None] == experts[None, :]).astype(i32), axis=0)
    zero = jnp.zeros((1,), i32)
    start = jnp.concatenate([zero, jnp.cumsum(counts)]).astype(i32)
    tile_end = jnp.cumsum((counts + tm - 1) // tm)
    tile0 = jnp.concatenate([zero, tile_end]).astype(i32)
    n_tiles = (a + N_EXPERTS * (tm - 1)) // tm
    tile_ids = jnp.arange(n_tiles, dtype=i32)
    tile_e = jnp.minimum(jnp.sum((tile_end[None, :] <= tile_ids[:, None]).astype(i32), axis=1),
                         N_EXPERTS - 1).astype(i32)
    return tile0, tile_e, start, order


def _hier_moe(x, g_ffn, w_r_pad, b_r_pad, w_gate, w_up, w_down, layer, g_out=None, n_first=None):
    ids, gates = _router(x, g_ffn, w_r_pad, b_r_pad)
    tile0, tile_e, start, order = _dispatch_plan(ids)
    y_slots, pos = _expert_ffn(x, g_ffn, w_gate, w_up, w_down, layer, tile0, tile_e, start, order)
    return _combine(x, y_slots, gates, pos, g_out, n_first)


def _rope_table(pos):
    inv = jnp.power(ROPE_THETA, -jnp.arange(ROPE_HALF, dtype=F32) * (2.0 / ROPE_DIM))
    ang = pos[:, None] * inv[None, :]
    return jnp.cos(ang), jnp.sin(ang)


def _pad_cols(w, n):
    return jnp.pad(w, ((0, 0), (0, n - w.shape[1])))


def _q_head_weights(w_uq):
    w = w_uq.reshape(Q_LORA, MLA_HEADS, QK_DIM)
    r1 = w[:, :, NOPE_DIM:NOPE_DIM + ROPE_HALF]
    r2 = w[:, :, NOPE_DIM + ROPE_HALF:]
    return jnp.concatenate([w, -r2, r1], axis=-1).reshape(Q_LORA, MLA_HEADS * QK_PAD).astype(BF16)


def kernel(x_prompt, x_sample, cache_mla_ckv, cache_mla_kpe, state_gla, norm_mix, norm_ffn, norm_out, mla_w_in, mla_g_q_lat, mla_g_kv_lat, mla_w_uq, mla_w_uk, mla_w_uv, mla_g_q, mla_g_k, mla_w_o, gla_w_in, gla_w_a, gla_b_a, gla_g_o, gla_w_o, moe_w_router, moe_b_router, moe_w_gate, moe_w_up, moe_w_down):
    batch, seq, d = x_prompt.shape
    nb, t_new, _ = x_sample.shape
    past = cache_mla_ckv.shape[2]
    n_p, n_s = batch * seq, nb * t_new
    x_p, x_s = x_prompt.reshape(n_p, d), x_sample.reshape(n_s, d)

    pos_rows = jnp.concatenate([jnp.tile(jnp.arange(seq, dtype=F32), batch),
                                jnp.tile(past + jnp.arange(t_new, dtype=F32), nb)])
    cos_r, sin_r = _rope_table(pos_rows)
    tab = jnp.concatenate([cos_r, cos_r, sin_r, sin_r], axis=1)
    w_in_pad = _pad_cols(mla_w_in[0], 9 * LANES).astype(BF16)
    c_q, c_kv_b, ckv_p, ckv_s, kpe_p, kpe_s = _mla_in(x_p, x_s, norm_mix[0], w_in_pad, mla_g_q_lat[0],
                                                      mla_g_kv_lat[0])
    q = _mla_q(c_q, _q_head_weights(mla_w_uq[0]), mla_g_q[0], tab)

    w_uk, w_uv = mla_w_uk[0], mla_w_uv[0]
    w_kv_heads = jnp.concatenate([w_uk, w_uv], axis=-1).reshape(KV_LORA, -1).astype(BF16)
    k_p, v_p = _mla_kv(c_kv_b, kpe_p, w_kv_heads, mla_g_k[0], tab, n_p)
    o_p = _flash_prompt(q, k_p, v_p, batch, seq)

    n_keys = past + t_new
    kp = (n_keys + LANES - 1) // LANES * LANES
    kpe_new = kpe_s.reshape(nb, t_new, ROPE_DIM)
    kpe_all = jnp.concatenate([cache_mla_kpe[0], kpe_new,
                               jnp.zeros((nb, kp - n_keys, ROPE_DIM), F32)], axis=1)
    kpet_all = kpe_all.transpose(0, 2, 1)
    cos_k, sin_k = _rope_table(jnp.arange(kp, dtype=F32))
    cost = jnp.concatenate([cos_k, cos_k], axis=1).T
    sint = jnp.concatenate([sin_k, sin_k], axis=1).T
    w_ukt = w_uk.reshape(KV_LORA, -1).T.astype(BF16)
    o_lat = _sample_attn(q, cache_mla_ckv, c_kv_b, kpet_all, w_ukt, mla_g_k[0], cost, sint,
                         row_block0=n_p // t_new, n_keys=n_keys, t_new=t_new)
    o_s = _latent_to_values(o_lat, w_uv.reshape(KV_LORA, -1).astype(BF16), t_new)

    x = _mm(o_p, mla_w_o[0].astype(BF16), x2=o_s, res=x_p, res2=x_s, out_dtype=F32,
            tm=MM_ROW_TILE, tn=1024)

    w_r_pad = [_pad_cols(moe_w_router[i], LANES).astype(BF16) for i in range(2)]
    b_r_pad = [_pad_cols(moe_b_router[i].reshape(1, -1), LANES) for i in range(2)]
    x = _hier_moe(x, norm_ffn[0], w_r_pad[0], b_r_pad[0], moe_w_gate, moe_w_up, moe_w_down, 0)

    hk, hv = GLA_HEADS * GLA_DK, GLA_HEADS * GLA_DV
    wg = gla_w_in[0]
    w_gla = jnp.concatenate([wg[:, :2 * hk + hv], wg[:, 2 * hk + hv + GATE_RANK:],
                             _pad_cols(wg[:, 2 * hk + hv:2 * hk + hv + GATE_RANK], MXU_WIDTH)],
                            axis=1).astype(BF16)
    z = _mm(x, w_gla, gain=norm_mix[1], out_dtype=BF16, tm=MM_ROW_TILE, tn=5 * MXU_WIDTH)
    w_a_pad = jnp.pad(gla_w_a[0], ((0, LANES - GATE_RANK), (0, 0))).astype(BF16)
    s0_p = jnp.zeros((batch, GLA_HEADS, GLA_DK, GLA_DV), F32)
    og_p, st_p = _gla_scan(z, w_a_pad, gla_b_a[0], gla_g_o[0], s0_p,
                           row0=0, n_streams=batch, t_len=seq, nsub=GLA_SUBCHUNKS)
    og_s, st_s = _gla_scan(z, w_a_pad, gla_b_a[0], gla_g_o[0], state_gla[0],
                           row0=n_p, n_streams=nb, t_len=t_new, nsub=t_new // CHUNK)
    x = _mm(og_p, gla_w_o[0].astype(BF16), x2=og_s, res=x, out_dtype=F32, tm=MM_ROW_TILE, tn=1024)
    y_p, y_s = _hier_moe(x, norm_ffn[1], w_r_pad[1], b_r_pad[1], moe_w_gate, moe_w_up, moe_w_down, 1,
                         g_out=norm_out, n_first=n_p)

    return (y_p.reshape(batch, seq, d),
            y_s.reshape(nb, t_new, d),
            ckv_p.reshape(1, batch, seq, KV_LORA),
            kpe_p.reshape(1, batch, seq, ROPE_DIM),
            st_p[None],
            ckv_s.reshape(1, nb, t_new, KV_LORA),
            kpe_s.reshape(1, nb, t_new, ROPE_DIM),
            st_s[None])
```

```python
import functools

import jax
import jax.numpy as jnp
from jax import lax
from jax.experimental import pallas as pl
from jax.experimental.pallas import tpu as pltpu

F32 = jnp.float32
BF16 = jnp.bfloat16

LANES = 128
MXU_WIDTH = 256
V7X_VMEM_BYTES = 64 * 1024 * 1024
VMEM_LIMIT = V7X_VMEM_BYTES * 7 // 8

EPS = 1e-6
CHUNK = 64
MLA_HEADS = 16
NOPE_DIM = 128
ROPE_DIM = 64
ROPE_HALF = ROPE_DIM // 2
QK_DIM = NOPE_DIM + ROPE_DIM
QK_PAD = 2 * LANES
V_DIM = 128
Q_LORA = 512
KV_LORA = 512
ROPE_THETA = 10000.0
ATTN_SCALE = QK_DIM ** -0.5
GLA_HEADS = 4
GLA_DK = 256
GLA_DV = 512
GATE_RANK = 16
GATE_TAU = 16.0
N_GROUPS = 8
EXPERTS_PER_GROUP = 8
N_EXPERTS = N_GROUPS * EXPERTS_PER_GROUP
TOP_K = 2
TOP_K_SHIFT = 1
assert 1 << TOP_K_SHIFT == TOP_K

ROW_TILE = 512
MM_ROW_TILE = 1024
FLASH_TQ = 1024
FLASH_TK = 1024
FLASH_HEADS = 2
SAMPLE_HEAD_GROUP = 4
GLA_SUBCHUNKS = 4
EXPERT_TILE = 176
COMBINE_TILE = 128
GATHER_BUFS = 6
COMBINE_BUFS = 3
OUT_BUFS = 3

NT_DIMS = (((1,), (1,)), ((), ()))
TN_DIMS = (((0,), (0,)), ((), ()))


def _params(*sem):
    return pltpu.CompilerParams(dimension_semantics=sem, vmem_limit_bytes=VMEM_LIMIT)


def _rms(x, g):
    return x * lax.rsqrt(jnp.mean(x * x, axis=-1, keepdims=True) + EPS) * g


def _mm_kernel(*refs, has_norm, has_res, split, res_split):
    it = iter(refs)
    x_ref = next(it)
    x2_ref = next(it) if split else None
    g_ref = next(it) if has_norm else None
    w_ref = next(it)
    r_ref = next(it) if has_res else None
    r2_ref = next(it) if res_split else r_ref
    o_ref = next(it)
    if has_norm:
        xn_ref = next(it)

        @pl.when(pl.program_id(1) == 0)
        def _():
            xn_ref[...] = _rms(x_ref[...], g_ref[...]).astype(BF16)

        x_ref = xn_ref

    def emit(src_ref, res_ref):
        acc = jnp.dot(src_ref[...], w_ref[...], preferred_element_type=F32)
        if has_res:
            acc = acc + res_ref[...]
        o_ref[...] = acc.astype(o_ref.dtype)

    if split:
        pl.when(pl.program_id(0) < split)(lambda: emit(x_ref, r_ref))
        pl.when(pl.program_id(0) >= split)(lambda: emit(x2_ref, r2_ref))
    else:
        emit(x_ref, r_ref)


def _mm(x, w, *, x2=None, gain=None, res=None, res2=None, out_dtype, tm, tn):
    m, k = x.shape
    n = w.shape[1]
    has_norm, has_res = gain is not None, res is not None
    split = 0
    in_specs = [pl.BlockSpec((tm, k), lambda i, j: (i, 0))]
    args = [x]
    if x2 is not None:
        assert not has_norm
        split = m // tm
        m += x2.shape[0]
        first = lambda i: jnp.minimum(i, split - 1)
        second = lambda i: jnp.maximum(i - split, 0)
        in_specs = [pl.BlockSpec((tm, k), lambda i, j: (first(i), 0)),
                    pl.BlockSpec((tm, k), lambda i, j: (second(i), 0))]
        args.append(x2)
    if has_norm:
        in_specs.append(pl.BlockSpec((1, k), lambda i, j: (0, 0)))
        args.append(gain.reshape(1, k))
    in_specs.append(pl.BlockSpec((k, tn), lambda i, j: (0, j)))
    args.append(w)
    res_split = res2 is not None
    if res_split:
        assert split and res.shape[0] == split * tm
        in_specs += [pl.BlockSpec((tm, tn), lambda i, j: (first(i), j)),
                     pl.BlockSpec((tm, tn), lambda i, j: (second(i), j))]
        args += [res, res2]
    elif has_res:
        in_specs.append(pl.BlockSpec((tm, tn), lambda i, j: (i, j)))
        args.append(res)
    return pl.pallas_call(
        functools.partial(_mm_kernel, has_norm=has_norm, has_res=has_res, split=split,
                          res_split=res_split),
        grid=(m // tm, n // tn),
        in_specs=in_specs,
        out_specs=pl.BlockSpec((tm, tn), lambda i, j: (i, j)),
        out_shape=jax.ShapeDtypeStruct((m, n), out_dtype),
        scratch_shapes=[pltpu.VMEM((tm, k), BF16)] if has_norm else [],
        compiler_params=_params("parallel", "arbitrary"),
        name="mm",
    )(*args)


def _mla_in_kernel(x_ref, x2_ref, g_ref, w_ref, gq_ref, gkv_ref, cq_ref, ckvb_ref, ckv_ref, ckv2_ref,
                   kpe_ref, kpe2_ref, *, split):
    def emit(src_ref, ckv_out, kpe_out):
        xn = _rms(src_ref[...], g_ref[...]).astype(BF16)
        z = jnp.dot(xn, w_ref[...], preferred_element_type=F32)
        cq_ref[...] = _rms(z[:, :Q_LORA], gq_ref[...]).astype(BF16)
        ckv = _rms(z[:, Q_LORA:Q_LORA + KV_LORA], gkv_ref[...])
        ckv_out[...] = ckv
        ckvb_ref[...] = ckv.astype(BF16)
        kpe_out[...] = z[:, Q_LORA + KV_LORA:Q_LORA + KV_LORA + ROPE_DIM]

    pl.when(pl.program_id(0) < split)(lambda: emit(x_ref, ckv_ref, kpe_ref))
    pl.when(pl.program_id(0) >= split)(lambda: emit(x2_ref, ckv2_ref, kpe2_ref))


def _mla_in(x, x2, g_mix, w_in_pad, g_q_lat, g_kv_lat):
    d = x.shape[1]
    tm = ROW_TILE
    split = x.shape[0] // tm
    n1, n2 = x.shape[0], x2.shape[0]
    n = n1 + n2
    wn = w_in_pad.shape[1]
    row = lambda i: (i, 0)
    first = lambda i: (jnp.minimum(i, split - 1), 0)
    second = lambda i: (jnp.maximum(i - split, 0), 0)
    fixed = lambda i: (0, 0)
    return pl.pallas_call(
        functools.partial(_mla_in_kernel, split=split),
        grid=(n // tm,),
        in_specs=[pl.BlockSpec((tm, d), first),
                  pl.BlockSpec((tm, d), second),
                  pl.BlockSpec((1, d), fixed),
                  pl.BlockSpec((d, wn), fixed), pl.BlockSpec((1, Q_LORA), fixed),
                  pl.BlockSpec((1, KV_LORA), fixed)],
        out_specs=[pl.BlockSpec((tm, Q_LORA), row), pl.BlockSpec((tm, KV_LORA), row),
                   pl.BlockSpec((tm, KV_LORA), first), pl.BlockSpec((tm, KV_LORA), second),
                   pl.BlockSpec((tm, ROPE_DIM), first), pl.BlockSpec((tm, ROPE_DIM), second)],
        out_shape=[jax.ShapeDtypeStruct((n, Q_LORA), BF16), jax.ShapeDtypeStruct((n, KV_LORA), BF16),
                   jax.ShapeDtypeStruct((n1, KV_LORA), F32), jax.ShapeDtypeStruct((n2, KV_LORA), F32),
                   jax.ShapeDtypeStruct((n1, ROPE_DIM), F32), jax.ShapeDtypeStruct((n2, ROPE_DIM), F32)],
        compiler_params=_params("parallel"),
        name="mla_in",
    )(x, x2, g_mix.reshape(1, d), w_in_pad, g_q_lat.reshape(1, -1), g_kv_lat.reshape(1, -1))


def _mla_q_kernel(cq_ref, w_ref, gn_ref, g2_ref, tab_ref, q_ref):
    cq = cq_ref[...]
    gtab = g2_ref[...] * tab_ref[...]
    is_rope = lax.broadcasted_iota(jnp.int32, gtab.shape, 1) < ROPE_DIM
    for h in range(MLA_HEADS):
        cols = slice(h * QK_PAD, (h + 1) * QK_PAD)
        t = jnp.dot(cq, w_ref[:, cols], preferred_element_type=F32)
        t1 = t[:, :NOPE_DIM]
        t2 = t[:, NOPE_DIM:]
        ss = (jnp.sum(t1 * t1, axis=-1, keepdims=True)
              + jnp.sum(jnp.where(is_rope, t2 * t2, 0.0), axis=-1, keepdims=True))
        rs = lax.rsqrt(ss * (1.0 / QK_DIM) + EPS) * ATTN_SCALE
        u = t2 * gtab
        u = jnp.where(is_rope, u + pltpu.roll(u, ROPE_DIM, axis=1), 0.0)
        q_ref[:, cols] = jnp.concatenate([t1 * gn_ref[...] * rs, u * rs], axis=1).astype(BF16)


def _mla_q(cq, w_q_heads, g_q, tab):
    n = cq.shape[0]
    tm = ROW_TILE
    g1, g2 = g_q[NOPE_DIM:NOPE_DIM + ROPE_HALF], g_q[NOPE_DIM + ROPE_HALF:]
    g_rope = jnp.concatenate([g1, g2, g2, g1]).reshape(1, LANES)
    row = lambda i: (i, 0)
    fixed = lambda i: (0, 0)
    return pl.pallas_call(
        _mla_q_kernel,
        grid=(n // tm,),
        in_specs=[pl.BlockSpec((tm, Q_LORA), row),
                  pl.BlockSpec((Q_LORA, MLA_HEADS * QK_PAD), fixed),
                  pl.BlockSpec((1, NOPE_DIM), fixed),
                  pl.BlockSpec((1, LANES), fixed),
                  pl.BlockSpec((tm, LANES), row)],
        out_specs=pl.BlockSpec((tm, MLA_HEADS * QK_PAD), row),
        out_shape=jax.ShapeDtypeStruct((n, MLA_HEADS * QK_PAD), BF16),
        compiler_params=_params("parallel"),
        name="mla_q",
    )(cq, w_q_heads, g_q[:NOPE_DIM].reshape(1, NOPE_DIM), g_rope, tab)


def _rotate_half_rows(x):
    return jnp.concatenate([-x[ROPE_HALF:], x[:ROPE_HALF]], axis=0)


def _mla_kv_kernel(c_ref, kpe_ref, w_ref, gn_ref, gr_ref, tab_ref, k_ref, v_ref):
    c = c_ref[...]
    kpe = kpe_ref[...]
    sspe = jnp.sum(kpe * kpe, axis=-1, keepdims=True)
    kg = kpe * gr_ref[...]
    rot = jnp.concatenate([-kg[:, ROPE_HALF:], kg[:, :ROPE_HALF]], axis=1)
    tab = tab_ref[...]
    kr = kg * tab[:, :ROPE_DIM] + rot * tab[:, ROPE_DIM:]
    kr = jnp.concatenate([kr, jnp.zeros_like(kr)], axis=1)
    for h in range(MLA_HEADS):
        t = jnp.dot(c, w_ref[:, h * QK_PAD:(h + 1) * QK_PAD], preferred_element_type=F32)
        kn = t[:, :NOPE_DIM]
        rs = lax.rsqrt((jnp.sum(kn * kn, axis=-1, keepdims=True) + sspe) * (1.0 / QK_DIM) + EPS)
        k_ref[:, h * QK_PAD:(h + 1) * QK_PAD] = jnp.concatenate(
            [kn * gn_ref[...] * rs, kr * rs], axis=1).astype(BF16)
        v_ref[:, h * V_DIM:(h + 1) * V_DIM] = t[:, NOPE_DIM:].astype(BF16)


def _mla_kv(ckv_b, kpe, w_kv_heads, g_k, tab, n_rows):
    tm = ROW_TILE
    row = lambda i: (i, 0)
    fixed = lambda i: (0, 0)
    return pl.pallas_call(
        _mla_kv_kernel,
        grid=(n_rows // tm,),
        in_specs=[pl.BlockSpec((tm, KV_LORA), row),
                  pl.BlockSpec((tm, ROPE_DIM), row),
                  pl.BlockSpec((KV_LORA, MLA_HEADS * (NOPE_DIM + V_DIM)), fixed),
                  pl.BlockSpec((1, NOPE_DIM), fixed),
                  pl.BlockSpec((1, ROPE_DIM), fixed),
                  pl.BlockSpec((tm, LANES), row)],
        out_specs=[pl.BlockSpec((tm, MLA_HEADS * QK_PAD), row),
                   pl.BlockSpec((tm, MLA_HEADS * V_DIM), row)],
        out_shape=[jax.ShapeDtypeStruct((n_rows, MLA_HEADS * QK_PAD), BF16),
                   jax.ShapeDtypeStruct((n_rows, MLA_HEADS * V_DIM), BF16)],
        compiler_params=_params("parallel"),
        name="mla_kv",
    )(ckv_b, kpe, w_kv_heads, g_k[:NOPE_DIM].reshape(1, NOPE_DIM),
      g_k[NOPE_DIM:].reshape(1, ROPE_DIM), tab)


def _flash_kernel(q_ref, k_ref, v_ref, o_ref, *, tq, tk):
    qi = pl.program_id(2)

    def step(j, carries, masked):
        ks = pl.multiple_of(j * tk, tk)
        out = []
        for hh, (m, l, acc) in enumerate(carries):
            qk = slice(hh * QK_PAD, (hh + 1) * QK_PAD)
            vv = slice(hh * V_DIM, (hh + 1) * V_DIM)
            s = lax.dot_general(q_ref[:, qk], k_ref[pl.ds(ks, tk), qk], NT_DIMS,
                                preferred_element_type=F32)
            if masked:
                row = lax.broadcasted_iota(jnp.int32, s.shape, 0) // CHUNK
                col = lax.broadcasted_iota(jnp.int32, s.shape, 1) // CHUNK
                s = jnp.where(col <= row, s, -jnp.inf)
            m_new = jnp.maximum(m, jnp.max(s, axis=-1, keepdims=True))
            p = jnp.exp(s - m_new)
            alpha = jnp.exp(m - m_new)
            l = alpha * l + jnp.sum(p, axis=-1, keepdims=True)
            acc = alpha * acc + jnp.dot(p.astype(BF16), v_ref[pl.ds(ks, tk), vv],
                                        preferred_element_type=F32)
            out.append((m_new, l, acc))
        return tuple(out)

    init = tuple((jnp.full((tq, 1), -jnp.inf, F32), jnp.zeros((tq, 1), F32),
                  jnp.zeros((tq, V_DIM), F32)) for _ in range(FLASH_HEADS))
    carries = lax.fori_loop(0, qi, lambda j, c: step(j, c, False), init)
    carries = step(qi, carries, True)
    for hh, (_, l, acc) in enumerate(carries):
        o_ref[:, hh * V_DIM:(hh + 1) * V_DIM] = (acc / l).astype(BF16)


def _flash_prompt(q, k, v, batch, seq):
    tq, tk, hs = FLASH_TQ, FLASH_TK, FLASH_HEADS
    assert tq == tk
    nq = seq // tq
    return pl.pallas_call(
        functools.partial(_flash_kernel, tq=tq, tk=tk),
        grid=(batch, MLA_HEADS // hs, nq),
        in_specs=[pl.BlockSpec((tq, hs * QK_PAD), lambda b, h, i: (b * nq + i, h)),
                  pl.BlockSpec((seq, hs * QK_PAD), lambda b, h, i: (b, h)),
                  pl.BlockSpec((seq, hs * V_DIM), lambda b, h, i: (b, h))],
        out_specs=pl.BlockSpec((tq, hs * V_DIM), lambda b, h, i: (b * nq + i, h)),
        out_shape=jax.ShapeDtypeStruct((batch * seq, MLA_HEADS * V_DIM), BF16),
        compiler_params=_params("parallel", "parallel", "arbitrary"),
        name="flash_prompt",
    )(q, k, v)


def _sample_attn_kernel(q_ref, cache_ref, cnew_ref, kpet_ref, wukt_ref, gkn_ref, gkrt_ref, cost_ref,
                        sint_ref, olat_ref, c_sc, qabs_ref, qr_ref, *, n_keys, t_new):
    kp = c_sc.shape[0]
    past = n_keys - t_new
    c_sc[:past, :] = cache_ref[0, 0].astype(BF16)
    c_sc[past:n_keys, :] = cnew_ref[...]
    c_sc[n_keys:, :] = jnp.zeros((kp - n_keys, KV_LORA), BF16)
    c = c_sc[...]
    kpet = kpet_ref[0]
    sspe = jnp.sum(kpet * kpet, axis=0, keepdims=True)
    kg = kpet * gkrt_ref[...]
    krt = kg * cost_ref[...] + _rotate_half_rows(kg) * sint_ref[...]
    krt = jnp.concatenate([krt, jnp.zeros_like(krt)], axis=0).astype(BF16)
    gkn = gkn_ref[...]
    for h in range(MLA_HEADS):
        qh = q_ref[:, h * QK_PAD:(h + 1) * QK_PAD]
        qn = (qh[:, :NOPE_DIM].astype(F32) * gkn).astype(BF16)
        qa = jnp.dot(qn, wukt_ref[h * NOPE_DIM:(h + 1) * NOPE_DIM, :], preferred_element_type=F32)
        qabs_ref[h * t_new:(h + 1) * t_new, :] = qa.astype(BF16)
        qr_ref[h * t_new:(h + 1) * t_new, :] = qh[:, NOPE_DIM:]
    valid = lax.broadcasted_iota(jnp.int32, (t_new, kp), 1) < n_keys
    hg = SAMPLE_HEAD_GROUP
    for g in range(MLA_HEADS // hg):
        knt = lax.dot_general(wukt_ref[g * hg * NOPE_DIM:(g + 1) * hg * NOPE_DIM, :], c, NT_DIMS,
                              preferred_element_type=F32)
        ss = jnp.sum((knt * knt).reshape(hg, NOPE_DIM, kp), axis=1) + sspe
        rst = lax.rsqrt(ss * (1.0 / QK_DIM) + EPS)
        rows = slice(g * hg * t_new, (g + 1) * hg * t_new)
        s = (lax.dot_general(qabs_ref[rows, :], c, NT_DIMS, preferred_element_type=F32)
             + jnp.dot(qr_ref[rows, :], krt, preferred_element_type=F32))
        ps = []
        for hh in range(hg):
            sh = s[hh * t_new:(hh + 1) * t_new] * rst[hh:hh + 1]
            sh = jnp.where(valid, sh, -jnp.inf)
            e = jnp.exp(sh - jnp.max(sh, axis=-1, keepdims=True))
            ps.append((e / jnp.sum(e, axis=-1, keepdims=True)).astype(BF16))
        p = jnp.concatenate(ps, axis=0)
        olat_ref[0, rows, :] = jnp.dot(p, c, preferred_element_type=F32).astype(BF16)


def _sample_attn(q, cache_ckv, c_rows, kpet_all, w_ukt, g_k, cost, sint, *, row_block0, n_keys, t_new):
    nb, past = cache_ckv.shape[1], cache_ckv.shape[2]
    kp = kpet_all.shape[2]
    assert past + t_new == n_keys
    fixed = lambda b: (0, 0)
    return pl.pallas_call(
        functools.partial(_sample_attn_kernel, n_keys=n_keys, t_new=t_new),
        grid=(nb,),
        in_specs=[pl.BlockSpec((t_new, MLA_HEADS * QK_PAD), lambda b: (row_block0 + b, 0)),
                  pl.BlockSpec((1, 1, past, KV_LORA), lambda b: (0, b, 0, 0)),
                  pl.BlockSpec((t_new, KV_LORA), lambda b: (row_block0 + b, 0)),
                  pl.BlockSpec((1, ROPE_DIM, kp), lambda b: (b, 0, 0)),
                  pl.BlockSpec((MLA_HEADS * NOPE_DIM, KV_LORA), fixed),
                  pl.BlockSpec((1, NOPE_DIM), fixed),
                  pl.BlockSpec((ROPE_DIM, 1), fixed),
                  pl.BlockSpec((ROPE_DIM, kp), fixed),
                  pl.BlockSpec((ROPE_DIM, kp), fixed)],
        out_specs=pl.BlockSpec((1, MLA_HEADS * t_new, KV_LORA), lambda b: (b, 0, 0)),
        out_shape=jax.ShapeDtypeStruct((nb, MLA_HEADS * t_new, KV_LORA), BF16),
        scratch_shapes=[pltpu.VMEM((kp, KV_LORA), BF16),
                        pltpu.VMEM((MLA_HEADS * t_new, KV_LORA), BF16),
                        pltpu.VMEM((MLA_HEADS * t_new, LANES), BF16)],
        compiler_params=_params("parallel"),
        name="sample_attn",
    )(q, cache_ckv, c_rows, kpet_all, w_ukt, g_k[:NOPE_DIM].reshape(1, NOPE_DIM),
      g_k[NOPE_DIM:].reshape(ROPE_DIM, 1), cost, sint)


def _head_mm_kernel(x_ref, w_ref, o_ref):
    nb, t, r = x_ref.shape
    o_ref[...] = jnp.dot(x_ref[...].reshape(nb * t, r), w_ref[...],
                         preferred_element_type=F32).astype(o_ref.dtype)


def _latent_to_values(o_lat, w_uv2d, t_new):
    nb = o_lat.shape[0]
    return pl.pallas_call(
        _head_mm_kernel,
        grid=(MLA_HEADS,),
        in_specs=[pl.BlockSpec((nb, t_new, KV_LORA), lambda h: (0, h, 0)),
                  pl.BlockSpec((KV_LORA, V_DIM), lambda h: (0, h))],
        out_specs=pl.BlockSpec((nb * t_new, V_DIM), lambda h: (0, h)),
        out_shape=jax.ShapeDtypeStruct((nb * t_new, MLA_HEADS * V_DIM), BF16),
        compiler_params=_params("parallel"),
        name="latent_to_values",
    )(o_lat, w_uv2d)


def _gla_kernel(q_ref, k_ref, v_ref, gate_ref, a_ref, wa_ref, ba_ref, go_ref, s0_ref,
                o_ref, sout_ref, st_ref, *, nsub):
    c = pl.program_id(1)

    @pl.when(c == 0)
    def _():
        for h in range(GLA_HEADS):
            st_ref[h] = s0_ref[0, h].T

    tril = (lax.broadcasted_iota(jnp.int32, (CHUNK, CHUNK), 0)
            >= lax.broadcasted_iota(jnp.int32, (CHUNK, CHUNK), 1))
    tril_b = jnp.where(tril, 1.0, 0.0).astype(BF16)
    for j in range(nsub):
        sl = slice(j * CHUNK, (j + 1) * CHUNK)
        x = jnp.dot(a_ref[sl, :], wa_ref[...], preferred_element_type=F32) + ba_ref[...]
        la = (jnp.minimum(x, 0.0) - jnp.log(1.0 + jnp.exp(-jnp.abs(x)))) * (1.0 / GATE_TAU)
        la_hi = la.astype(BF16)
        la_lo = (la - la_hi.astype(F32)).astype(BF16)
        b_all = (jnp.dot(tril_b, la_hi, preferred_element_type=F32)
                 + jnp.dot(tril_b, la_lo, preferred_element_type=F32))
        for h in range(GLA_HEADS):
            kc = slice(h * GLA_DK, (h + 1) * GLA_DK)
            vc = slice(h * GLA_DV, (h + 1) * GLA_DV)
            b = b_all[:, kc]
            b_last = b[CHUNK - 1:CHUNK, :]
            q = q_ref[sl, kc].astype(F32) * (GLA_DK ** -0.5)
            k = k_ref[sl, kc].astype(F32)
            v = v_ref[sl, vc]
            qs = (q * jnp.exp(b)).astype(BF16)
            ks = (k * jnp.exp(-b)).astype(BF16)
            att = lax.dot_general(qs, ks, NT_DIMS, preferred_element_type=F32)
            att = jnp.where(tril, att, 0.0).astype(BF16)
            st = st_ref[h]
            o = (jnp.dot(att, v, preferred_element_type=F32)
                 + lax.dot_general(qs, st.astype(BF16), NT_DIMS, preferred_element_type=F32))
            kd = (k * jnp.exp(b_last - b)).astype(BF16)
            st_ref[h] = st * jnp.exp(b_last) + lax.dot_general(v, kd, TN_DIMS,
                                                               preferred_element_type=F32)
            gt = gate_ref[sl, vc].astype(F32)
            o_ref[sl, vc] = (_rms(o, go_ref[...]) * (gt / (1.0 + jnp.exp(-gt)))).astype(BF16)

    @pl.when(c == pl.num_programs(1) - 1)
    def _():
        for h in range(GLA_HEADS):
            sout_ref[0, h] = st_ref[h].T


def _gla_scan(z, w_a_pad, b_a, g_o, s0, *, row0, n_streams, t_len, nsub):
    tc = nsub * CHUNK
    nc = t_len // tc
    rb0 = row0 // tc
    rows = lambda b, c: rb0 + b * nc + c
    hk, hv = GLA_HEADS * GLA_DK, GLA_HEADS * GLA_DV
    a_blk = (2 * hk + 2 * hv) // LANES
    fixed = lambda b, c: (0, 0)
    state = lambda b, c: (b, 0, 0, 0)
    return pl.pallas_call(
        functools.partial(_gla_kernel, nsub=nsub),
        grid=(n_streams, nc),
        in_specs=[pl.BlockSpec((tc, hk), lambda b, c: (rows(b, c), 0)),
                  pl.BlockSpec((tc, hk), lambda b, c: (rows(b, c), 1)),
                  pl.BlockSpec((tc, hv), lambda b, c: (rows(b, c), 2 * hk // hv)),
                  pl.BlockSpec((tc, hv), lambda b, c: (rows(b, c), 2 * hk // hv + 1)),
                  pl.BlockSpec((tc, LANES), lambda b, c: (rows(b, c), a_blk)),
                  pl.BlockSpec((LANES, hk), fixed),
                  pl.BlockSpec((1, hk), fixed),
                  pl.BlockSpec((1, GLA_DV), fixed),
                  pl.BlockSpec((1, GLA_HEADS, GLA_DK, GLA_DV), state)],
        out_specs=[pl.BlockSpec((tc, hv), lambda b, c: (b * nc + c, 0)),
                   pl.BlockSpec((1, GLA_HEADS, GLA_DK, GLA_DV), state)],
        out_shape=[jax.ShapeDtypeStruct((n_streams * t_len, hv), BF16),
                   jax.ShapeDtypeStruct((n_streams, GLA_HEADS, GLA_DK, GLA_DV), F32)],
        scratch_shapes=[pltpu.VMEM((GLA_HEADS, GLA_DV, GLA_DK), F32)],
        compiler_params=_params("parallel", "arbitrary"),
        name="gla_scan",
    )(z, z, z, z, z, w_a_pad, b_a.reshape(1, hk), g_o.reshape(1, GLA_DV), s0)


def _router_kernel(x_ref, g_ref, w_ref, b_ref, ids_ref, gates_ref):
    xn = _rms(x_ref[...], g_ref[...])
    w = w_ref[...]
    x_hi, w_hi = xn.astype(BF16), w.astype(BF16)
    x_lo = (xn - x_hi.astype(F32)).astype(BF16)
    w_lo = (w - w_hi.astype(F32)).astype(BF16)
    logits = (jnp.dot(x_hi, w_hi, preferred_element_type=F32)
              + jnp.dot(x_lo, w_hi, preferred_element_type=F32)
              + jnp.dot(x_hi, w_lo, preferred_element_type=F32)) + b_ref[...]
    lane = lax.broadcasted_iota(jnp.int32, logits.shape, 1)
    neg = -jnp.inf

    def top(mask):
        vals = jnp.where(mask, logits, neg)
        m = jnp.max(vals, axis=-1, keepdims=True)
        idx = jnp.min(jnp.where(vals == m, lane, LANES), axis=-1, keepdims=True)
        return m, idx

    is_grp = lane < N_GROUPS
    m_g, grp = top(is_grp)
    p_grp = 1.0 / jnp.sum(jnp.where(is_grp, jnp.exp(logits - m_g), 0.0), axis=-1, keepdims=True)
    lo = N_GROUPS + grp * EXPERTS_PER_GROUP
    in_grp = (lane >= lo) & (lane < lo + EXPERTS_PER_GROUP)
    m1, i1 = top(in_grp)
    m2, i2 = top(in_grp & (lane != i1))
    e2 = jnp.exp(m2 - m1)
    g1 = p_grp / (1.0 + e2)
    g2 = p_grp * e2 / (1.0 + e2)
    ids_ref[...] = jnp.where(lane == 0, i1 - N_GROUPS, jnp.where(lane == 1, i2 - N_GROUPS, 0))
    gates_ref[...] = jnp.where(lane == 0, g1, jnp.where(lane == 1, g2, 0.0))


def _router(x, g_ffn, w_r_pad, b_r_pad):
    n, d = x.shape
    tm = ROW_TILE
    row = lambda i: (i, 0)
    fixed = lambda i: (0, 0)
    return pl.pallas_call(
        _router_kernel,
        grid=(n // tm,),
        in_specs=[pl.BlockSpec((tm, d), row), pl.BlockSpec((1, d), fixed),
                  pl.BlockSpec((d, LANES), fixed), pl.BlockSpec((1, LANES), fixed)],
        out_specs=[pl.BlockSpec((tm, LANES), row), pl.BlockSpec((tm, LANES), row)],
        out_shape=[jax.ShapeDtypeStruct((n, LANES), jnp.int32), jax.ShapeDtypeStruct((n, LANES), F32)],
        compiler_params=_params("parallel"),
        name="router",
    )(x, g_ffn.reshape(1, d), w_r_pad, b_r_pad)


def _row_copy(src_hbm, row, dst_vmem, r, sem):
    return pltpu.make_async_copy(src_hbm.at[pl.ds(row, 1)], dst_vmem.at[pl.ds(r, 1)], sem)


def _ffn_kernel(tile0_ref, tile_e_ref, start_ref, order_ref, x_hbm, g_ref, wg_ref, wu_ref, wd_ref,
                y_hbm, pos_ref, xbuf, ybuf, wg_b, wu_b, wd_b, gsem, osem):
    e = pl.program_id(0)
    tm = ybuf.shape[1]
    t_lo, t_hi, n_used = tile0_ref[e], tile0_ref[e + 1], tile0_ref[N_EXPERTS]
    nbuf = xbuf.shape[0]
    ahead = nbuf - 1

    def gather_start(t, slot, unrolled):
        ee = tile_e_ref[t]
        base = start_ref[ee]
        last = start_ref[ee + 1] - base - 1
        off0 = (t - tile0_ref[ee]) * tm

        def one(r):
            a = order_ref[base + jnp.minimum(off0 + r, last)]
            tok = lax.shift_right_logical(a, TOP_K_SHIFT)
            _row_copy(x_hbm, tok, xbuf.at[slot], r, gsem.at[slot]).start()
            pos_ref[a] = t * tm + r

        if unrolled:
            for r in range(tm):
                one(r)
        else:
            lax.fori_loop(0, tm, lambda r, c: (one(r), c)[1], 0)

    def gather_wait(slot):
        pltpu.make_async_copy(x_hbm.at[pl.ds(0, tm)], xbuf.at[slot], gsem.at[slot]).wait()

    nout = ybuf.shape[0]

    def out_copy(t):
        s = lax.rem(t, nout)
        return pltpu.make_async_copy(ybuf.at[s], y_hbm.at[pl.ds(t * tm, tm)], osem.at[s])

    @pl.when((e == 0) & (n_used > 0))
    def _():
        for k in range(ahead):
            gather_start(jnp.minimum(k, n_used - 1), k, False)

    @pl.when(t_hi > t_lo)
    def _():
        wg_b[...] = wg_ref[0, 0].astype(BF16)
        wu_b[...] = wu_ref[0, 0].astype(BF16)
        wd_b[...] = wd_ref[0, 0].astype(BF16)

        def tile(t, _):
            slot = lax.rem(t, nbuf)
            gather_wait(slot)

            @pl.when(t >= nout)
            def _():
                out_copy(t - nout).wait()

            xn = _rms(xbuf[slot], g_ref[...]).astype(BF16)
            gather_start(jnp.minimum(t + ahead, n_used - 1), lax.rem(t + ahead, nbuf), True)
            hg = jnp.dot(xn, wg_b[...], preferred_element_type=F32)
            hu = jnp.dot(xn, wu_b[...], preferred_element_type=F32)
            hid = (hg / (1.0 + jnp.exp(-hg)) * hu).astype(BF16)
            ybuf[lax.rem(t, nout)] = jnp.dot(hid, wd_b[...], preferred_element_type=F32)
            out_copy(t).start()
            return 0

        lax.fori_loop(t_lo, t_hi, tile, 0)

    @pl.when(e == pl.num_programs(0) - 1)
    def _():
        @pl.when(n_used >= 1)
        def _():
            for k in range(ahead):
                gather_wait(lax.rem(n_used + k, nbuf))

        for back in range(1, nout + 1):
            @pl.when(n_used >= back)
            def _():
                out_copy(n_used - back).wait()

        n_tiles = y_hbm.shape[0] // tm
        ybuf[0] = jnp.zeros(ybuf.shape[1:], F32)

        def zero_copy(t):
            return pltpu.make_async_copy(ybuf.at[0], y_hbm.at[pl.ds(t * tm, tm)], osem.at[0])

        lax.fori_loop(n_used, n_tiles, lambda t, c: (zero_copy(t).start(), c)[1], 0)
        lax.fori_loop(n_used, n_tiles, lambda t, c: (zero_copy(t).wait(), c)[1], 0)


def _expert_ffn(x, g_ffn, w_gate, w_up, w_down, layer, tile0, tile_e, start, order):
    n, d = x.shape
    tm = EXPERT_TILE
    de = w_gate.shape[3]
    fixed = lambda e, *_: (0, 0)
    w_map = lambda e, *_: (layer, e, 0, 0)
    grid_spec = pltpu.PrefetchScalarGridSpec(
        num_scalar_prefetch=4,
        grid=(N_EXPERTS,),
        in_specs=[pl.BlockSpec(memory_space=pl.ANY),
                  pl.BlockSpec((1, d), fixed),
                  pl.BlockSpec((1, 1, d, de), w_map),
                  pl.BlockSpec((1, 1, d, de), w_map),
                  pl.BlockSpec((1, 1, de, d), w_map)],
        out_specs=[pl.BlockSpec(memory_space=pl.ANY), pl.BlockSpec(memory_space=pltpu.SMEM)],
        scratch_shapes=[pltpu.VMEM((GATHER_BUFS, tm, d), F32), pltpu.VMEM((OUT_BUFS, tm, d), F32),
                        pltpu.VMEM((d, de), BF16), pltpu.VMEM((d, de), BF16), pltpu.VMEM((de, d), BF16),
                        pltpu.SemaphoreType.DMA((GATHER_BUFS,)), pltpu.SemaphoreType.DMA((OUT_BUFS,))],
    )
    return pl.pallas_call(
        _ffn_kernel,
        grid_spec=grid_spec,
        out_shape=[jax.ShapeDtypeStruct((tile_e.shape[0] * tm, d), F32),
                   jax.ShapeDtypeStruct(order.shape, jnp.int32)],
        compiler_params=_params("arbitrary"),
        name="expert_ffn",
    )(tile0, tile_e, start, order, x, g_ffn.reshape(1, d), w_gate, w_up, w_down)


def _combine_kernel(pos_ref, y_hbm, gates_ref, x_ref, *rest, split):
    if split:
        go_ref, o_ref, o2_ref, buf0, buf1, sem = rest
    else:
        o_ref, buf0, buf1, sem = rest
    i = pl.program_id(0)
    tc = buf0.shape[1]

    last = pl.num_programs(0) - 1

    def gather_start(t, slot, unrolled):
        def one(r):
            a = TOP_K * (t * tc + r)
            _row_copy(y_hbm, pos_ref[a], buf0.at[slot], r, sem.at[slot]).start(priority=0)
            _row_copy(y_hbm, pos_ref[a + 1], buf1.at[slot], r, sem.at[slot]).start(priority=1)

        if unrolled:
            for r in range(tc):
                one(r)
        else:
            lax.fori_loop(0, tc, lambda r, c: (one(r), c)[1], 0)

    def gather_wait(slot):
        for buf in (buf0, buf1):
            pltpu.make_async_copy(y_hbm.at[pl.ds(0, tc)], buf.at[slot], sem.at[slot]).wait()

    nbuf = buf0.shape[0]
    ahead = nbuf - 1

    @pl.when(i == 0)
    def _():
        for k in range(ahead):
            gather_start(jnp.minimum(k, last), k, False)

    slot = lax.rem(i, nbuf)
    gather_wait(slot)
    gates = gates_ref[...]
    out = x_ref[...] + buf0[slot] * gates[:, 0:1] + buf1[slot] * gates[:, 1:2]
    gather_start(jnp.minimum(i + ahead, last), lax.rem(i + ahead, nbuf), True)
    if split:
        out = _rms(out, go_ref[...])

        @pl.when(i < split)
        def _():
            o_ref[...] = out

        @pl.when(i >= split)
        def _():
            o2_ref[...] = out
    else:
        o_ref[...] = out

    @pl.when(i == last)
    def _():
        for k in range(1, nbuf):
            gather_wait(lax.rem(i + k, nbuf))


def _combine(x, y_slots, gates, pos, g_out=None, n_first=None):
    n, d = x.shape
    tc = COMBINE_TILE
    split = 0 if g_out is None else n_first // tc
    in_specs = [pl.BlockSpec(memory_space=pl.ANY),
                pl.BlockSpec((tc, LANES), lambda i, p: (i, 0)),
                pl.BlockSpec((tc, d), lambda i, p: (i, 0))]
    args = [pos, y_slots, gates, x]
    out_specs = pl.BlockSpec((tc, d), lambda i, p: (i, 0))
    out_shape = jax.ShapeDtypeStruct((n, d), F32)
    if split:
        in_specs.append(pl.BlockSpec((1, d), lambda i, p: (0, 0)))
        args.append(g_out.reshape(1, d))
        out_specs = [pl.BlockSpec((tc, d), lambda i, p: (jnp.minimum(i, split - 1), 0)),
                     pl.BlockSpec((tc, d), lambda i, p: (jnp.maximum(i - split, 0), 0))]
        out_shape = [jax.ShapeDtypeStruct((n_first, d), F32), jax.ShapeDtypeStruct((n - n_first, d), F32)]
    grid_spec = pltpu.PrefetchScalarGridSpec(
        num_scalar_prefetch=1,
        grid=(n // tc,),
        in_specs=in_specs,
        out_specs=out_specs,
        scratch_shapes=[pltpu.VMEM((COMBINE_BUFS, tc, d), F32), pltpu.VMEM((COMBINE_BUFS, tc, d), F32),
                        pltpu.SemaphoreType.DMA((COMBINE_BUFS,))],
    )
    return pl.pallas_call(
        functools.partial(_combine_kernel, split=split),
        grid_spec=grid_spec,
        out_shape=out_shape,
        compiler_params=_params("arbitrary"),
        name="moe_combine",
    )(*args)


def _dispatch_plan(ids):
    n = ids.shape[0]
    a = n * TOP_K
    tm = EXPERT_TILE
    i32 = jnp.int32
    e_flat = ids[:, :TOP_K].reshape(a)
    order = jnp.argsort(e_flat, stable=True).astype(i32)
    experts = jnp.arange(N_EXPERTS, dtype=i32)
    counts = jnp.sum((e_flat[:, None] == experts[None, :]).astype(i32), axis=0)
    zero = jnp.zeros((1,), i32)
    start = jnp.concatenate([zero, jnp.cumsum(counts)]).astype(i32)
    tile_end = jnp.cumsum((counts + tm - 1) // tm)
    tile0 = jnp.concatenate([zero, tile_end]).astype(i32)
    n_tiles = (a + N_EXPERTS * (tm - 1)) // tm
    tile_ids = jnp.arange(n_tiles, dtype=i32)
    tile_e = jnp.minimum(jnp.sum((tile_end[None, :] <= tile_ids[:, None]).astype(i32), axis=1),
                         N_EXPERTS - 1).astype(i32)
    return tile0, tile_e, start, order


def _hier_moe(x, g_ffn, w_r_pad, b_r_pad, w_gate, w_up, w_down, layer, g_out=None, n_first=None):
    ids, gates = _router(x, g_ffn, w_r_pad, b_r_pad)
    tile0, tile_e, start, order = _dispatch_plan(ids)
    y_slots, pos = _expert_ffn(x, g_ffn, w_gate, w_up, w_down, layer, tile0, tile_e, start, order)
    return _combine(x, y_slots, gates, pos, g_out, n_first)


def _rope_table(pos):
    inv = jnp.power(ROPE_THETA, -jnp.arange(ROPE_HALF, dtype=F32) * (2.0 / ROPE_DIM))
    ang = pos[:, None] * inv[None, :]
    return jnp.cos(ang), jnp.sin(ang)


def _pad_cols(w, n):
    return jnp.pad(w, ((0, 0), (0, n - w.shape[1])))


def _q_head_weights(w_uq):
    w = w_uq.reshape(Q_LORA, MLA_HEADS, QK_DIM)
    r1 = w[:, :, NOPE_DIM:NOPE_DIM + ROPE_HALF]
    r2 = w[:, :, NOPE_DIM + ROPE_HALF:]
    return jnp.concatenate([w, -r2, r1], axis=-1).reshape(Q_LORA, MLA_HEADS * QK_PAD).astype(BF16)


def kernel(x_prompt, x_sample, cache_mla_ckv, cache_mla_kpe, state_gla, norm_mix, norm_ffn, norm_out, mla_w_in, mla_g_q_lat, mla_g_kv_lat, mla_w_uq, mla_w_uk, mla_w_uv, mla_g_q, mla_g_k, mla_w_o, gla_w_in, gla_w_a, gla_b_a, gla_g_o, gla_w_o, moe_w_router, moe_b_router, moe_w_gate, moe_w_up, moe_w_down):
    batch, seq, d = x_prompt.shape
    nb, t_new, _ = x_sample.shape
    past = cache_mla_ckv.shape[2]
    n_p, n_s = batch * seq, nb * t_new
    x_p, x_s = x_prompt.reshape(n_p, d), x_sample.reshape(n_s, d)

    pos_rows = jnp.concatenate([jnp.tile(jnp.arange(seq, dtype=F32), batch),
                                jnp.tile(past + jnp.arange(t_new, dtype=F32), nb)])
    cos_r, sin_r = _rope_table(pos_rows)
    tab = jnp.concatenate([cos_r, cos_r, sin_r, sin_r], axis=1)
    w_in_pad = _pad_cols(mla_w_in[0], 9 * LANES).astype(BF16)
    c_q, c_kv_b, ckv_p, ckv_s, kpe_p, kpe_s = _mla_in(x_p, x_s, norm_mix[0], w_in_pad, mla_g_q_lat[0],
                                                      mla_g_kv_lat[0])
    q = _mla_q(c_q, _q_head_weights(mla_w_uq[0]), mla_g_q[0], tab)

    w_uk, w_uv = mla_w_uk[0], mla_w_uv[0]
    w_kv_heads = jnp.concatenate([w_uk, w_uv], axis=-1).reshape(KV_LORA, -1).astype(BF16)
    k_p, v_p = _mla_kv(c_kv_b, kpe_p, w_kv_heads, mla_g_k[0], tab, n_p)
    o_p = _flash_prompt(q, k_p, v_p, batch, seq)

    n_keys = past + t_new
    kp = (n_keys + LANES - 1) // LANES * LANES
    kpe_new = kpe_s.reshape(nb, t_new, ROPE_DIM)
    kpe_all = jnp.concatenate([cache_mla_kpe[0], kpe_new,
                               jnp.zeros((nb, kp - n_keys, ROPE_DIM), F32)], axis=1)
    kpet_all = kpe_all.transpose(0, 2, 1)
    cos_k, sin_k = _rope_table(jnp.arange(kp, dtype=F32))
    cost = jnp.concatenate([cos_k, cos_k], axis=1).T
    sint = jnp.concatenate([sin_k, sin_k], axis=1).T
    w_ukt = w_uk.reshape(KV_LORA, -1).T.astype(BF16)
    o_lat = _sample_attn(q, cache_mla_ckv, c_kv_b, kpet_all, w_ukt, mla_g_k[0], cost, sint,
                         row_block0=n_p // t_new, n_keys=n_keys, t_new=t_new)
    o_s = _latent_to_values(o_lat, w_uv.reshape(KV_LORA, -1).astype(BF16), t_new)

    x = _mm(o_p, mla_w_o[0].astype(BF16), x2=o_s, res=x_p, res2=x_s, out_dtype=F32,
            tm=MM_ROW_TILE, tn=1024)

    w_r_pad = [_pad_cols(moe_w_router[i], LANES) for i in range(2)]
    b_r_pad = [_pad_cols(moe_b_router[i].reshape(1, -1), LANES) for i in range(2)]
    x = _hier_moe(x, norm_ffn[0], w_r_pad[0], b_r_pad[0], moe_w_gate, moe_w_up, moe_w_down, 0)

    hk, hv = GLA_HEADS * GLA_DK, GLA_HEADS * GLA_DV
    wg = gla_w_in[0]
    w_gla = jnp.concatenate([wg[:, :2 * hk + hv], wg[:, 2 * hk + hv + GATE_RANK:],
                             _pad_cols(wg[:, 2 * hk + hv:2 * hk + hv + GATE_RANK], MXU_WIDTH)],
                            axis=1).astype(BF16)
    z = _mm(x, w_gla, gain=norm_mix[1], out_dtype=BF16, tm=MM_ROW_TILE, tn=5 * MXU_WIDTH)
    w_a_pad = jnp.pad(gla_w_a[0], ((0, LANES - GATE_RANK), (0, 0))).astype(BF16)
    s0_p = jnp.zeros((batch, GLA_HEADS, GLA_DK, GLA_DV), F32)
    og_p, st_p = _gla_scan(z, w_a_pad, gla_b_a[0], gla_g_o[0], s0_p,
                           row0=0, n_streams=batch, t_len=seq, nsub=GLA_SUBCHUNKS)
    og_s, st_s = _gla_scan(z, w_a_pad, gla_b_a[0], gla_g_o[0], state_gla[0],
                           row0=n_p, n_streams=nb, t_len=t_new, nsub=t_new // CHUNK)
    x = _mm(og_p, gla_w_o[0].astype(BF16), x2=og_s, res=x, out_dtype=F32, tm=MM_ROW_TILE, tn=1024)
    y_p, y_s = _hier_moe(x, norm_ffn[1], w_r_pad[1], b_r_pad[1], moe_w_gate, moe_w_up, moe_w_down, 1,
                         g_out=norm_out, n_first=n_p)

    return (y_p.reshape(batch, seq, d),
            y_s.reshape(nb, t_new, d),
            ckv_p.reshape(1, batch, seq, KV_LORA),
            kpe_p.reshape(1, batch, seq, ROPE_DIM),
            st_p[None],
            ckv_s.reshape(1, nb, t_new, KV_LORA),
            kpe_s.reshape(1, nb, t_new, ROPE_DIM),
            st_s[None])
```
